```python
import math
import jax, jax.numpy as jnp
from jax import lax
import numpy as np

D_MODEL = 2048
BATCH = 8
SEQ = 2048
DEPTH = 1
DEC_BATCH = 32
DEC_SEQ = 1
PAST_LEN = 16384
PAGE_SIZE = 128

HEAD_DIM = 128
N_ATT = D_MODEL // 2
H_ATT = N_ATT // HEAD_DIM
N_SSM = D_MODEL // 2
SSM_GROUP = 16
N_GROUPS = N_SSM // SSM_GROUP
SSM_STATE = 64
H_IDX = 16
D_IDX = 64
TOPK_MAX = 256
Q_BLOCK = 64
ROPE_THETA = 500000.0
ROPE_FRACTION = 4
D_FF = ((8 * D_MODEL // 3 + 255) // 256) * 256
CONV_W = 3
RMS_EPS = 1e-6
NEG_INF = -1e30

kernel_name = "hybrid_s5_dsa_convffn_step"


def rms_norm(x, g):
    x32 = x.astype(jnp.float32)
    y = x32 * lax.rsqrt(jnp.mean(x32 * x32, axis=-1, keepdims=True) + RMS_EPS)
    return (y * g.astype(jnp.float32)).astype(x.dtype)


def rope_partial(x, pos):
    d = x.shape[-1]
    r = d // ROPE_FRACTION
    half = r // 2
    inv = ROPE_THETA ** (-jnp.arange(half, dtype=jnp.float32) * 2.0 / r)
    ang = pos.astype(jnp.float32)[:, None] * inv[None, :]
    shape = (1, x.shape[1]) + (1,) * (x.ndim - 3) + (half,)
    cos = jnp.cos(ang).reshape(shape)
    sin = jnp.sin(ang).reshape(shape)
    x1 = x[..., :half].astype(jnp.float32)
    x2 = x[..., half:r].astype(jnp.float32)
    return jnp.concatenate([(x1 * cos - x2 * sin).astype(x.dtype),
                            (x2 * cos + x1 * sin).astype(x.dtype),
                            x[..., r:]], axis=-1)


def project_inputs(x, pos, norm_mix, w_in):
    B, T, _ = x.shape
    h = rms_norm(x, norm_mix)
    z = h @ w_in
    sizes = [N_SSM, N_ATT, N_ATT, N_ATT, H_IDX * D_IDX, D_IDX, H_IDX, D_MODEL, D_MODEL]
    u, q, k, v, qi, ki, wi, ga, gb = jnp.split(z, np.cumsum(sizes)[:-1].tolist(), axis=-1)
    q = rope_partial(q.reshape(B, T, H_ATT, HEAD_DIM), pos)
    k = rope_partial(k.reshape(B, T, H_ATT, HEAD_DIM), pos)
    v = v.reshape(B, T, H_ATT, HEAD_DIM)
    qi = rope_partial(qi.reshape(B, T, H_IDX, D_IDX), pos)
    ki = rope_partial(ki, pos)
    wi = wi * (H_IDX ** -0.5)
    return u, q, k, v, qi, ki, wi, jax.nn.sigmoid(ga), jax.nn.sigmoid(gb)


def select_keys(qi, wi, ki, q_pos, top_k):
    L = ki.shape[1]
    s = jnp.einsum('bthd,bsd->bths', qi, ki, preferred_element_type=jnp.float32) * (D_IDX ** -0.5)
    score = jnp.einsum('bths,bth->bts', jax.nn.relu(s), wi.astype(jnp.float32))
    causal = jnp.arange(L, dtype=jnp.int32)[None, :] <= q_pos[:, None]
    score = jnp.where(causal[None], score, -jnp.inf)
    _, idx = lax.top_k(score, top_k)
    valid = idx <= q_pos[None, :, None]
    return idx, valid


def attend_selected(q, k_sel, v_sel, valid):
    logits = jnp.einsum('bthd,btkhd->bthk', q, k_sel, preferred_element_type=jnp.float32) * (HEAD_DIM ** -0.5)
    logits = jnp.where(valid[:, :, None, :], logits, NEG_INF)
    p = jax.nn.softmax(logits, axis=-1)
    out = jnp.einsum('bthk,btkhd->bthd', p.astype(v_sel.dtype), v_sel, preferred_element_type=jnp.float32)
    return out.astype(q.dtype)


rows_take = jax.vmap(lambda rows, i: rows[i])


def prompt_sparse_attention(q, k, v, qi, ki, wi, pos):
    B, T = q.shape[:2]
    nb = T // Q_BLOCK
    top_k = min(TOPK_MAX, T // 4)

    def blocks(a):
        return jnp.swapaxes(a.reshape((B, nb, Q_BLOCK) + a.shape[2:]), 0, 1)

    def one_block(args):
        q_b, qi_b, wi_b, pos_b = args
        idx, valid = select_keys(qi_b, wi_b, ki, pos_b, top_k)
        return attend_selected(q_b, rows_take(k, idx), rows_take(v, idx), valid)

    out = lax.map(one_block, (blocks(q), blocks(qi), blocks(wi), pos.reshape(nb, Q_BLOCK)))
    return jnp.swapaxes(out, 0, 1).reshape(B, T, N_ATT)


def sample_sparse_attention(q, k_new, v_new, qi, ki_new, wi, pos, cache_k, cache_v, cache_idx_k, page_table):
    Bd, T = q.shape[:2]
    ki_past = cache_idx_k[page_table].reshape(Bd, PAST_LEN, D_IDX)
    ki_all = jnp.concatenate([ki_past.astype(ki_new.dtype), ki_new], axis=1)
    top_k = min(TOPK_MAX, (PAST_LEN + T) // 4)
    idx, valid = select_keys(qi, wi, ki_all, pos, top_k)
    in_past = idx < PAST_LEN
    idx_p = jnp.minimum(idx, PAST_LEN - 1)
    phys = jax.vmap(lambda pt, i: pt[i])(page_table, idx_p // PAGE_SIZE)
    off = idx_p % PAGE_SIZE
    idx_n = jnp.clip(idx - PAST_LEN, 0, T - 1)

    def fetch(pool, new):
        return jnp.where(in_past[..., None, None], pool[phys, off].astype(new.dtype), rows_take(new, idx_n))

    return attend_selected(q, fetch(cache_k, k_new), fetch(cache_v, v_new), valid).reshape(Bd, T, N_ATT)


def _ssm_combine(e1, e2):
    a1, b1 = e1
    a2, b2 = e2
    return a1 * a2, a2 * b1 + b2


def s5_mixer(u, s0_re, s0_im, A_re, A_im, log_dt, B_re, B_im, C_re, C_im, D_skip, w_glu):
    Bn, T, _ = u.shape
    f32 = jnp.float32
    u32 = u.astype(f32)
    A = lax.complex(A_re.astype(f32), A_im.astype(f32))
    dt = jnp.exp(log_dt.astype(f32))[:, None]
    A_bar = jnp.exp(A * dt)
    B_bar = ((A_bar - 1.0) / A)[..., None] * lax.complex(B_re.astype(f32), B_im.astype(f32))
    ug = u32.reshape(Bn, T, N_GROUPS, SSM_GROUP).astype(jnp.complex64)
    bu = jnp.einsum('btgc,gpc->btgp', ug, B_bar)
    a = jnp.broadcast_to(A_bar, (1, T, N_GROUPS, SSM_STATE))
    a_cum, s = lax.associative_scan(_ssm_combine, (a, bu), axis=1)
    s = s + a_cum * lax.complex(s0_re.astype(f32), s0_im.astype(f32))[:, None]
    C = lax.complex(C_re.astype(f32), C_im.astype(f32))
    y = jnp.real(jnp.einsum('gcp,btgp->btgc', C, s)).reshape(Bn, T, N_SSM) + D_skip.astype(f32) * u32
    y = jax.nn.gelu(y).astype(u.dtype)
    z = y @ w_glu
    out = z[..., :N_SSM] * jax.nn.sigmoid(z[..., N_SSM:])
    s_last = s[:, -1]
    return out, jnp.real(s_last), jnp.imag(s_last)


def conv_ffn(h, conv_prev, w_up, conv_w, conv_b, w_down):
    T = h.shape[1]
    up = h @ w_up
    xp = jnp.concatenate([conv_prev.astype(up.dtype), up], axis=1)
    c = conv_b
    for j in range(CONV_W):
        c = c + conv_w[j] * xp[:, j:j + T]
    a, b = jnp.split(c, 2, axis=-1)
    return (jax.nn.silu(a) * b) @ w_down, xp[:, -(CONV_W - 1):]


def merge_branches(x, ssm_out, att_out, g_ssm, g_att, w_branch_ssm, w_branch_att, w_out):
    merged = g_ssm * (ssm_out @ w_branch_ssm) + g_att * (att_out @ w_branch_att)
    return x + merged @ w_out


def setup_inputs(seed: int = 0) -> dict:
    key = jax.random.key(seed)
    ks = jax.random.split(key, 40)
    f32 = jnp.float32

    def nrm(k, shape, scale):
        return jax.random.normal(k, shape, f32) * scale

    n_pages = PAST_LEN // PAGE_SIZE
    n_used = DEC_BATCH * n_pages
    n_pool = (5 * n_used + 3) // 4
    page_table = jax.random.permutation(ks[0], n_pool)[:n_used].reshape(DEC_BATCH, n_pages).astype(jnp.int32)
    in_width = N_SSM + 3 * N_ATT + H_IDX * D_IDX + D_IDX + H_IDX + 2 * D_MODEL
    n_idx = jnp.arange(SSM_STATE, dtype=f32)
    return {
        "x_prompt": nrm(ks[1], (BATCH, SEQ, D_MODEL), 1.0),
        "x_sample": nrm(ks[2], (DEC_BATCH, DEC_SEQ, D_MODEL), 1.0),
        "cache_k": nrm(ks[3], (n_pool, PAGE_SIZE, H_ATT, HEAD_DIM), 1.0),
        "cache_v": nrm(ks[4], (n_pool, PAGE_SIZE, H_ATT, HEAD_DIM), 1.0),
        "cache_idx_k": nrm(ks[5], (n_pool, PAGE_SIZE, D_IDX), 1.0),
        "page_table": page_table,
        "state_ssm_re": nrm(ks[6], (DEC_BATCH, N_GROUPS, SSM_STATE), 0.3),
        "state_ssm_im": nrm(ks[7], (DEC_BATCH, N_GROUPS, SSM_STATE), 0.3),
        "state_ffn_conv": nrm(ks[8], (DEC_BATCH, CONV_W - 1, 2 * D_FF), 1.0),
        "norm_mix": 1.0 + nrm(ks[9], (D_MODEL,), 0.01),
        "w_in": nrm(ks[10], (D_MODEL, in_width), D_MODEL ** -0.5),
        "ssm_A_re": -0.5 + nrm(ks[11], (N_GROUPS, SSM_STATE), 0.01),
        "ssm_A_im": jnp.pi * n_idx[None, :] + nrm(ks[12], (N_GROUPS, SSM_STATE), 0.01),
        "ssm_log_dt": jax.random.uniform(ks[13], (N_GROUPS,), f32, math.log(0.001), math.log(0.1)),
        "ssm_B_re": nrm(ks[14], (N_GROUPS, SSM_STATE, SSM_GROUP), (2 * SSM_GROUP) ** -0.5),
        "ssm_B_im": nrm(ks[15], (N_GROUPS, SSM_STATE, SSM_GROUP), (2 * SSM_GROUP) ** -0.5),
        "ssm_C_re": nrm(ks[16], (N_GROUPS, SSM_GROUP, SSM_STATE), SSM_STATE ** -0.5),
        "ssm_C_im": nrm(ks[17], (N_GROUPS, SSM_GROUP, SSM_STATE), SSM_STATE ** -0.5),
        "ssm_D": nrm(ks[18], (N_SSM,), 0.5),
        "w_glu": nrm(ks[19], (N_SSM, 2 * N_SSM), N_SSM ** -0.5),
        "w_branch_ssm": nrm(ks[20], (N_SSM, D_MODEL), N_SSM ** -0.5),
        "w_branch_att": nrm(ks[21], (N_ATT, D_MODEL), N_ATT ** -0.5),
        "w_out": nrm(ks[22], (D_MODEL, D_MODEL), D_MODEL ** -0.5),
        "norm_ffn": 1.0 + nrm(ks[23], (D_MODEL,), 0.01),
        "w_up": nrm(ks[24], (D_MODEL, 2 * D_FF), D_MODEL ** -0.5),
        "ffn_conv_w": nrm(ks[25], (CONV_W, 2 * D_FF), CONV_W ** -0.5),
        "ffn_conv_b": nrm(ks[26], (2 * D_FF,), 0.01),
        "w_down": nrm(ks[27], (D_FF, D_MODEL), D_FF ** -0.5),
        "norm_final": 1.0 + nrm(ks[28], (D_MODEL,), 0.01),
    }


def reference(x_prompt, x_sample, cache_k, cache_v, cache_idx_k, page_table, state_ssm_re, state_ssm_im,
              state_ffn_conv, norm_mix, w_in, ssm_A_re, ssm_A_im, ssm_log_dt, ssm_B_re, ssm_B_im, ssm_C_re,
              ssm_C_im, ssm_D, w_glu, w_branch_ssm, w_branch_att, w_out, norm_ffn, w_up, ffn_conv_w,
              ffn_conv_b, w_down, norm_final):
    pos_p = jnp.arange(SEQ, dtype=jnp.int32)
    y = x_prompt
    s_re_p = s_im_p = conv_p = k_p = v_p = ki_p = None
    for _ in range(DEPTH):
        u, q, k_p, v_p, qi, ki_p, wi, g_s, g_a = project_inputs(y, pos_p, norm_mix, w_in)
        zeros_s = jnp.zeros((y.shape[0], N_GROUPS, SSM_STATE), jnp.float32)
        ssm_out, s_re_p, s_im_p = s5_mixer(u, zeros_s, zeros_s, ssm_A_re, ssm_A_im, ssm_log_dt, ssm_B_re,
                                           ssm_B_im, ssm_C_re, ssm_C_im, ssm_D, w_glu)
        att = prompt_sparse_attention(q, k_p, v_p, qi, ki_p, wi, pos_p)
        y = merge_branches(y, ssm_out, att, g_s, g_a, w_branch_ssm, w_branch_att, w_out)
        zeros_c = jnp.zeros((y.shape[0], CONV_W - 1, 2 * D_FF), y.dtype)
        f, conv_p = conv_ffn(rms_norm(y, norm_ffn), zeros_c, w_up, ffn_conv_w, ffn_conv_b, w_down)
        y = y + f
    y_prompt = rms_norm(y, norm_final)

    pos_s = PAST_LEN + jnp.arange(DEC_SEQ, dtype=jnp.int32)
    y = x_sample
    s_re_s = s_im_s = conv_s = k_s = v_s = ki_s = None
    for _ in range(DEPTH):
        u, q, k_s, v_s, qi, ki_s, wi, g_s, g_a = project_inputs(y, pos_s, norm_mix, w_in)
        ssm_out, s_re_s, s_im_s = s5_mixer(u, state_ssm_re, state_ssm_im, ssm_A_re, ssm_A_im, ssm_log_dt,
                                           ssm_B_re, ssm_B_im, ssm_C_re, ssm_C_im, ssm_D, w_glu)
        att = sample_sparse_attention(q, k_s, v_s, qi, ki_s, wi, pos_s, cache_k, cache_v, cache_idx_k, page_table)
        y = merge_branches(y, ssm_out, att, g_s, g_a, w_branch_ssm, w_branch_att, w_out)
        f, conv_s = conv_ffn(rms_norm(y, norm_ffn), state_ffn_conv, w_up, ffn_conv_w, ffn_conv_b, w_down)
        y = y + f
    y_sample = rms_norm(y, norm_final)

    return (y_prompt, y_sample, k_p, v_p, ki_p, k_s, v_s, ki_s, s_re_p, s_im_p, s_re_s, s_im_s, conv_p, conv_s)
```

```python
import functools
import math

import numpy as np
import jax
import jax.numpy as jnp
from jax import lax
from jax.experimental import pallas as pl
from jax.experimental.pallas import tpu as pltpu

F32 = jnp.float32
BF16 = jnp.bfloat16

HEAD_DIM = 128
SSM_GROUP = 16
SSM_STATE = 64
H_IDX = 16
D_IDX = 64
TOPK_MAX = 256
ROPE_THETA = 500000.0
ROPE_FRACTION = 4
CONV_W = 3
RMS_EPS = 1e-6
NEG_INF = -1e30
PAGE_SIZE = 128

LANES = 128
SSM_CHUNK = 16
VMEM_LIMIT = 56 * 1024 * 1024
INT_MIN = -2 ** 31


def _cparams(sem):
    return pltpu.CompilerParams(dimension_semantics=sem, vmem_limit_bytes=VMEM_LIMIT)


def _rmsnorm_kernel(x_ref, g_ref, o_ref):
    x = x_ref[...]
    ms = jnp.mean(x * x, axis=-1, keepdims=True)
    o_ref[...] = (x * lax.rsqrt(ms + RMS_EPS) * g_ref[...]).astype(o_ref.dtype)


def rmsnorm_bf16(x2d, g, tm):
    m, d = x2d.shape
    return pl.pallas_call(
        _rmsnorm_kernel,
        grid=(m // tm,),
        in_specs=[pl.BlockSpec((tm, d), lambda i: (i, 0)), pl.BlockSpec((1, d), lambda i: (0, 0))],
        out_specs=pl.BlockSpec((tm, d), lambda i: (i, 0)),
        out_shape=jax.ShapeDtypeStruct((m, d), BF16),
        compiler_params=_cparams(("parallel",)),
        name="rmsnorm",
    )(x2d, g.reshape(1, d))


def _proj_kernel(h_ref, w_ref, *rest, mode, shift):
    o_ref = rest[-1]
    z = jnp.dot(h_ref[...], w_ref[...], preferred_element_type=F32)
    if mode == "sigmoid":
        z = jax.nn.sigmoid(z)
    elif mode == "rope":
        c_ref, s1_ref, s2_ref = rest[:3]
        tn = z.shape[-1]
        z = z * c_ref[...] + pltpu.roll(z, tn - shift, 1) * s1_ref[...] + pltpu.roll(z, shift, 1) * s2_ref[...]
    o_ref[...] = z.astype(o_ref.dtype)


def project(h, w, tm, tn, mode="plain", shift=0, tables=None, n_pos_tiles=1):
    m, k = h.shape
    n = w.shape[1]
    in_specs = [pl.BlockSpec((tm, k), lambda i, j: (i, 0)), pl.BlockSpec((k, tn), lambda i, j: (0, j))]
    args = [h, w]
    if mode == "rope":
        for t in tables:
            in_specs.append(pl.BlockSpec((tm, tn), lambda i, j: (i % n_pos_tiles, 0)))
            args.append(t)
    return pl.pallas_call(
        functools.partial(_proj_kernel, mode=mode, shift=shift),
        grid=(m // tm, n // tn),
        in_specs=in_specs,
        out_specs=pl.BlockSpec((tm, tn), lambda i, j: (i, j)),
        out_shape=jax.ShapeDtypeStruct((m, n), F32),
        compiler_params=_cparams(("parallel", "parallel")),
        name="proj_" + mode,
    )(*args)


def rope_tables(pos, head, tn, extra=None):
    r = head // ROPE_FRACTION
    half = r // 2
    inv = ROPE_THETA ** (-jnp.arange(half, dtype=F32) * 2.0 / r)
    ang = pos.astype(F32)[:, None] * inv[None, :]
    cos, sin = jnp.cos(ang), jnp.sin(ang)
    t = pos.shape[0]
    zeros = jnp.zeros((t, head - r), F32)
    zh = jnp.zeros((t, half), F32)
    c = jnp.concatenate([cos, cos, jnp.ones((t, head - r), F32)], axis=1)
    s1 = jnp.concatenate([-sin, zh, zeros], axis=1)
    s2 = jnp.concatenate([zh, sin, zeros], axis=1)
    if extra is None:
        reps = tn // head
        return tuple(jnp.tile(a, (1, reps)) for a in (c, s1, s2))
    pad = jnp.zeros((t, tn - head), F32)
    return (jnp.concatenate([c, jnp.broadcast_to(extra[None, :], (t, tn - head))], axis=1),
            jnp.concatenate([s1, pad], axis=1), jnp.concatenate([s2, pad], axis=1))


def ssm_matrices(a_re, a_im, log_dt, b_re, b_im, c_re, c_im, chunk, width):
    hp = lax.Precision.HIGHEST
    g, p = a_re.shape
    a = lax.complex(a_re.astype(F32), a_im.astype(F32))
    dt = jnp.exp(log_dt.astype(F32))[:, None]
    adt = a * dt
    a_bar = jnp.exp(adt)
    b_bar = ((a_bar - 1.0) / a)[..., None] * lax.complex(b_re.astype(F32), b_im.astype(F32))
    cc = lax.complex(c_re.astype(F32), c_im.astype(F32))
    steps = jnp.arange(chunk + 1, dtype=F32)
    pw = jnp.exp(adt[:, None, :] * steps[None, :, None].astype(jnp.complex64))
    kd = jnp.real(jnp.einsum("gcp,gdp,gpe->gdce", cc, pw[:, :chunk], b_bar, precision=hp))
    jj = jnp.arange(chunk)[:, None]
    tt = jnp.arange(chunk)[None, :]
    dd = tt - jj
    kt = jnp.where((dd >= 0)[None, :, :, None, None], kd[:, jnp.maximum(dd, 0)], 0.0)
    kt = kt.transpose(0, 1, 4, 2, 3).reshape(g, chunk * SSM_GROUP, chunk * SSM_GROUP)
    w1 = pw[:, chunk - 1 - jnp.arange(chunk)][:, :, None, :] * b_bar.transpose(0, 2, 1)[:, None]
    w1 = w1.reshape(g, chunk * SSM_GROUP, p)
    w1 = jnp.concatenate([jnp.real(w1), jnp.imag(w1), jnp.imag(w1), jnp.real(w1)], axis=-1)
    m2 = cc.transpose(0, 2, 1)[:, :, None, :] * pw[:, 1:chunk + 1].transpose(0, 2, 1)[:, :, :, None]
    m2 = m2.reshape(g, p, chunk * SSM_GROUP)
    w2 = jnp.concatenate([jnp.real(m2), -jnp.imag(m2)], axis=1)
    al = pw[:, chunk]
    ar, ai = jnp.real(al), jnp.imag(al)
    coef = jnp.stack([jnp.concatenate([ar, ar], -1), jnp.concatenate([-ai, ai], -1),
                      jnp.concatenate([ai, -ai], -1)], axis=1)
    padw = width - chunk * SSM_GROUP
    kt = jnp.pad(kt, ((0, 0), (0, padw), (0, padw)))
    w1 = jnp.pad(w1, ((0, 0), (0, padw), (0, 0)))
    w2 = jnp.pad(w2, ((0, 0), (0, 0), (0, padw)))
    return kt, w1, w2, coef


def _ssm_kernel(x_ref, kt_ref, w1_ref, w2_ref, coef_ref, s0_ref, y_ref, sfin_ref, sl_scr, sprev_scr,
                *, n_chunks, bn, precise):
    prec = lax.Precision.HIGHEST if precise else None
    x = x_ref[0]
    sl_scr[...] = jnp.dot(x, w1_ref[0], preferred_element_type=F32, precision=prec)
    half = sprev_scr.shape[-1]
    ar = jnp.broadcast_to(coef_ref[0, 0:1, :], (bn, half))
    aia = jnp.broadcast_to(coef_ref[0, 1:2, :], (bn, half))
    aib = jnp.broadcast_to(coef_ref[0, 2:3, :], (bn, half))

    def step(n, carry):
        sa, sb = carry
        r0 = pl.multiple_of(n * bn, bn)
        sprev_scr[pl.ds(r0, bn), :] = sa
        sl = sl_scr[pl.ds(r0, bn), :]
        return ar * sa + aia * sb + sl[:, :half], ar * sb + aib * sa + sl[:, half:]

    s0 = s0_ref[0]
    sa, _ = lax.fori_loop(0, n_chunks, step, (s0[:, :half], s0[:, half:]))
    sfin_ref[0] = sa
    sprev = sprev_scr[...]
    if not precise:
        sprev = sprev.astype(BF16)
    y_ref[0] = (jnp.dot(x, kt_ref[0], preferred_element_type=F32, precision=prec)
                + jnp.dot(sprev, w2_ref[0], preferred_element_type=F32, precision=prec))


def s5_scan(u, s0_re, s0_im, mats, chunk, precise):
    b, t, n = u.shape
    g = n // SSM_GROUP
    p = SSM_STATE
    kt, w1, w2, coef = mats
    width = kt.shape[-1]
    nc = t // chunk
    rows = nc * b
    wdt = F32 if precise else BF16
    x = u.reshape(b, nc, chunk, g, SSM_GROUP).transpose(3, 1, 0, 2, 4).reshape(g, rows, chunk * SSM_GROUP)
    x = jnp.pad(x, ((0, 0), (0, 0), (0, width - chunk * SSM_GROUP))).astype(wdt)
    s0r = s0_re.astype(F32).transpose(1, 0, 2)
    s0i = s0_im.astype(F32).transpose(1, 0, 2)
    s0 = jnp.concatenate([s0r, s0i, s0i, s0r], axis=-1)
    y, sfin = pl.pallas_call(
        functools.partial(_ssm_kernel, n_chunks=nc, bn=b, precise=precise),
        grid=(g,),
        in_specs=[pl.BlockSpec((1, rows, width), lambda i: (i, 0, 0)),
                  pl.BlockSpec((1, width, width), lambda i: (i, 0, 0)),
                  pl.BlockSpec((1, width, 4 * p), lambda i: (i, 0, 0)),
                  pl.BlockSpec((1, 2 * p, width), lambda i: (i, 0, 0)),
                  pl.BlockSpec((1, 3, 2 * p), lambda i: (i, 0, 0)),
                  pl.BlockSpec((1, b, 4 * p), lambda i: (i, 0, 0))],
        out_specs=[pl.BlockSpec((1, rows, width), lambda i: (i, 0, 0)),
                   pl.BlockSpec((1, b, 2 * p), lambda i: (i, 0, 0))],
        out_shape=[jax.ShapeDtypeStruct((g, rows, width), F32), jax.ShapeDtypeStruct((g, b, 2 * p), F32)],
        scratch_shapes=[pltpu.VMEM((rows, 4 * p), F32), pltpu.VMEM((rows, 2 * p), F32)],
        compiler_params=_cparams(("parallel",)),
        name="s5_scan",
    )(x, kt.astype(wdt), w1.astype(wdt), w2.astype(wdt), coef, s0)
    y = y[:, :, :chunk * SSM_GROUP].reshape(g, nc, b, chunk, SSM_GROUP).transpose(2, 1, 3, 0, 4).reshape(b, t, n)
    sfin = sfin.transpose(1, 0, 2)
    return y, sfin[..., :p], sfin[..., p:]


def _glu_kernel(y_ref, u_ref, d_ref, w_ref, o_ref):
    n = o_ref.shape[-1]
    y = jax.nn.gelu(y_ref[...] + d_ref[...] * u_ref[...]).astype(BF16)
    z = jnp.dot(y, w_ref[...], preferred_element_type=F32)
    o_ref[...] = (z[:, :n] * jax.nn.sigmoid(z[:, n:])).astype(o_ref.dtype)


def ssm_glu(y, u, d_skip, w_glu, tm):
    m, n = y.shape
    return pl.pallas_call(
        _glu_kernel,
        grid=(m // tm,),
        in_specs=[pl.BlockSpec((tm, n), lambda i: (i, 0)), pl.BlockSpec((tm, n), lambda i: (i, 0)),
                  pl.BlockSpec((1, n), lambda i: (0, 0)), pl.BlockSpec((n, 2 * n), lambda i: (0, 0))],
        out_specs=pl.BlockSpec((tm, n), lambda i: (i, 0)),
        out_shape=jax.ShapeDtypeStruct((m, n), BF16),
        compiler_params=_cparams(("parallel",)),
        name="ssm_glu",
    )(y, u, d_skip.reshape(1, n).astype(F32), w_glu)


def _sortable_key(score):
    bits = pltpu.bitcast(score, jnp.int32)
    return jnp.where(bits < 0, bits ^ jnp.int32(0x7FFFFFFF), bits)


def _kth_largest_key(key, k):
    rows = key.shape[0]

    def body(i, res):
        cand = res + jnp.left_shift(jnp.int32(1), 31 - i)
        cnt = jnp.sum((key >= cand).astype(jnp.int32), axis=-1, keepdims=True)
        return jnp.where(cnt >= k, cand, res)

    return lax.fori_loop(0, 32, body, jnp.full((rows, 1), INT_MIN, jnp.int32))


def _exclusive_cumsum_lanes(flags_bf16):
    rows, n = flags_bf16.shape
    r = lax.broadcasted_iota(jnp.int32, (LANES, LANES), 0)
    c = lax.broadcasted_iota(jnp.int32, (LANES, LANES), 1)
    tri = jnp.where(r < c, 1.0, 0.0).astype(BF16)
    off = jnp.zeros((rows, 1), F32)
    out = []
    for i in range(n // LANES):
        chunk = flags_bf16[:, i * LANES:(i + 1) * LANES]
        out.append(jnp.dot(chunk, tri, preferred_element_type=F32) + off)
        off = off + jnp.sum(chunk.astype(F32), axis=-1, keepdims=True)
    return jnp.concatenate(out, axis=-1)


def _prompt_attn_kernel(qi_ref, ki_ref, w_ref, q_ref, k_ref, v_ref, o_ref, bias_scr, *, tq, seq, topk):
    nt = (((1,), (1,)), ((), ()))
    qb = pl.program_id(1)
    ki = ki_ref[0]
    w = w_ref[0] * (D_IDX ** -0.5)
    acc = jnp.zeros((tq, seq), F32)
    for h in range(H_IDX):
        s = lax.dot_general(qi_ref[0, h], ki, nt, preferred_element_type=F32)
        acc = acc + jnp.maximum(s, 0.0) * w[:, h:h + 1]
    row = qb * tq + lax.broadcasted_iota(jnp.int32, (tq, seq), 0)
    col = lax.broadcasted_iota(jnp.int32, (tq, seq), 1)
    causal = col <= row
    key = _sortable_key(jnp.where(causal, acc, -jnp.inf))
    thr = _kth_largest_key(key, topk)
    gt = key > thr
    eq = jnp.logical_and(key == thr, causal)
    need = topk - jnp.sum(gt.astype(jnp.int32), axis=-1, keepdims=True)
    n_eq = jnp.sum(eq.astype(jnp.int32), axis=-1, keepdims=True)
    bias_scr[...] = jnp.where(jnp.logical_and(causal, key >= thr), 0.0, NEG_INF)

    @pl.when(jnp.max(n_eq - need) > 0)
    def _():
        rank = _exclusive_cumsum_lanes(jnp.where(eq, 1.0, 0.0).astype(BF16))
        keep = jnp.logical_or(gt, jnp.logical_and(eq, rank < need.astype(F32)))
        bias_scr[...] = jnp.where(jnp.logical_and(causal, keep), 0.0, NEG_INF)

    bias = bias_scr[...]
    scale = HEAD_DIM ** -0.5
    for h in range(q_ref.shape[-1] // HEAD_DIM):
        sl = slice(h * HEAD_DIM, (h + 1) * HEAD_DIM)
        logits = lax.dot_general(q_ref[0, :, sl], k_ref[0, :, sl], nt, preferred_element_type=F32) * scale + bias
        m = jnp.max(logits, axis=-1, keepdims=True)
        p = jnp.exp(logits - m)
        den = jnp.sum(p, axis=-1, keepdims=True)
        o = jnp.dot(p.astype(BF16), v_ref[0, :, sl], preferred_element_type=F32) / den
        o_ref[0, :, sl] = o.astype(o_ref.dtype)


def prompt_attention(qi, ki, wi, q, k, v, tq):
    b, t, n = q.shape
    topk = min(TOPK_MAX, t // 4)
    return pl.pallas_call(
        functools.partial(_prompt_attn_kernel, tq=tq, seq=t, topk=topk),
        grid=(b, t // tq),
        in_specs=[pl.BlockSpec((1, H_IDX, tq, D_IDX), lambda i, j: (i, 0, j, 0)),
                  pl.BlockSpec((1, t, D_IDX), lambda i, j: (i, 0, 0)),
                  pl.BlockSpec((1, tq, H_IDX), lambda i, j: (i, j, 0)),
                  pl.BlockSpec((1, tq, n), lambda i, j: (i, j, 0)),
                  pl.BlockSpec((1, t, n), lambda i, j: (i, 0, 0)),
                  pl.BlockSpec((1, t, n), lambda i, j: (i, 0, 0))],
        out_specs=pl.BlockSpec((1, tq, n), lambda i, j: (i, j, 0)),
        out_shape=jax.ShapeDtypeStruct((b, t, n), BF16),
        scratch_shapes=[pltpu.VMEM((tq, t), F32)],
        compiler_params=_cparams(("parallel", "parallel")),
        name="prompt_attn",
    )(qi, ki, wi, q, k, v)


def _sample_score_kernel(pt_ref, qi_ref, w_ref, kin_ref, cache_ref, o_ref, kbuf, sem, *, n_pages, n_batch):
    nt = (((1,), (1,)), ((), ()))
    b = pl.program_id(0)
    past = n_pages * PAGE_SIZE

    def page_copy(bb, slot, pg):
        return pltpu.make_async_copy(cache_ref.at[pt_ref[bb, pg]],
                                     kbuf.at[slot, pl.ds(pl.multiple_of(pg * PAGE_SIZE, PAGE_SIZE), PAGE_SIZE)],
                                     sem.at[slot])

    def start_all(bb, slot):
        lax.fori_loop(0, n_pages, lambda pg, c: (page_copy(bb, slot, pg).start(), c)[1], 0)

    @pl.when(b == 0)
    def _():
        start_all(0, 0)

    @pl.when(b + 1 < n_batch)
    def _():
        start_all(b + 1, (b + 1) % 2)

    slot = b % 2
    lax.fori_loop(0, n_pages, lambda pg, c: (page_copy(b, slot, pg).wait(), c)[1], 0)

    qi = qi_ref[0]
    w = w_ref[0] * (D_IDX ** -0.5)
    step = 2048
    for c in range(past // step):
        kc = kbuf[slot, c * step:(c + 1) * step, :].astype(BF16)
        s = lax.dot_general(qi, kc, nt, preferred_element_type=F32)
        o_ref[0, :, c * step:(c + 1) * step] = jnp.sum(jnp.maximum(s, 0.0) * w, axis=0, keepdims=True)
    s_new = jnp.sum(qi.astype(F32) * kin_ref[0].astype(BF16).astype(F32), axis=-1, keepdims=True)
    s_new = jnp.sum(jnp.maximum(s_new, 0.0) * w, axis=0, keepdims=True)
    lane = lax.broadcasted_iota(jnp.int32, (1, LANES), 1)
    o_ref[0, :, past:] = jnp.where(lane == 0, s_new, -jnp.inf)


def sample_scores(page_table, qi, wi, ki_new, cache_idx_k):
    bsz, n_pages = page_table.shape
    past = n_pages * PAGE_SIZE
    grid_spec = pltpu.PrefetchScalarGridSpec(
        num_scalar_prefetch=1,
        grid=(bsz,),
        in_specs=[pl.BlockSpec((1, H_IDX, D_IDX), lambda i, pt: (i, 0, 0)),
                  pl.BlockSpec((1, H_IDX, 1), lambda i, pt: (i, 0, 0)),
                  pl.BlockSpec((1, 1, D_IDX), lambda i, pt: (i, 0, 0)),
                  pl.BlockSpec(memory_space=pl.ANY)],
        out_specs=pl.BlockSpec((1, 1, past + LANES), lambda i, pt: (i, 0, 0)),
        scratch_shapes=[pltpu.VMEM((2, past, D_IDX), F32), pltpu.SemaphoreType.DMA((2,))],
    )
    return pl.pallas_call(
        functools.partial(_sample_score_kernel, n_pages=n_pages, n_batch=bsz),
        grid_spec=grid_spec,
        out_shape=jax.ShapeDtypeStruct((bsz, 1, past + LANES), F32),
        compiler_params=_cparams(("arbitrary",)),
        name="sample_scores",
    )(page_table, qi, wi, ki_new, cache_idx_k)


def _sample_select_kernel(score_ref, pos_ref, o_ref, rank_scr, *, topk):
    score = score_ref[...]
    bsz, n = score.shape
    key = _sortable_key(score)
    thr = _kth_largest_key(key, topk)
    gt = key > thr
    eq = key == thr
    need = topk - jnp.sum(gt.astype(jnp.int32), axis=-1, keepdims=True)
    rank_eq = _exclusive_cumsum_lanes(jnp.where(eq, 1.0, 0.0).astype(BF16))
    sel = jnp.logical_or(gt, jnp.logical_and(eq, rank_eq < need.astype(F32)))
    self32 = jnp.where(sel, 1.0, 0.0)
    rank = _exclusive_cumsum_lanes(self32.astype(BF16))
    rank_scr[...] = jnp.where(sel, rank, -1.0)
    slot_id = lax.broadcasted_iota(jnp.int32, (topk, 1), 0).astype(F32)
    n_split = 3
    step = n // n_split

    def body(bb, c):
        acc = jnp.zeros((topk, LANES), F32)
        for s in range(n_split):
            onehot = jnp.where(rank_scr[pl.ds(bb, 1), s * step:(s + 1) * step] == slot_id, 1.0, 0.0).astype(BF16)
            acc = acc + jnp.dot(onehot, pos_ref[s * step:(s + 1) * step, :], preferred_element_type=F32)
        o_ref[bb] = acc
        return c

    lax.fori_loop(0, bsz, body, 0)


def sample_select(scores, topk):
    bsz, n = scores.shape
    s = jnp.arange(n, dtype=jnp.int32)
    lane = jnp.arange(LANES, dtype=jnp.int32)[None, :]
    pos = jnp.where(lane == 0, (s // PAGE_SIZE)[:, None], jnp.where(lane == 1, (s % PAGE_SIZE)[:, None], 0)).astype(BF16)
    return pl.pallas_call(
        functools.partial(_sample_select_kernel, topk=topk),
        out_shape=jax.ShapeDtypeStruct((bsz, topk, LANES), F32),
        scratch_shapes=[pltpu.VMEM((bsz, n), F32)],
        compiler_params=pltpu.CompilerParams(vmem_limit_bytes=VMEM_LIMIT),
        name="sample_select",
    )(scores, pos)


def _sample_attend_kernel(idx_ref, pt_ref, q_ref, knew_ref, vnew_ref, ck_ref, cv_ref, o_ref, kbuf, vbuf, ksem, vsem,
                          *, topk, past, n_batch):
    b = pl.program_id(0)

    def row_copies(bb, slot, r, in_past):
        i = jnp.minimum(idx_ref[bb, r], past - 1)
        phys = pt_ref[bb, i // PAGE_SIZE]
        off = i % PAGE_SIZE
        ksrc = ck_ref.at[phys, off] if in_past else knew_ref.at[bb]
        vsrc = cv_ref.at[phys, off] if in_past else vnew_ref.at[bb]
        return (pltpu.make_async_copy(ksrc, kbuf.at[slot, r], ksem.at[slot]),
                pltpu.make_async_copy(vsrc, vbuf.at[slot, r], vsem.at[slot]))

    def start_all(bb, slot):
        def body(r, c):
            old = idx_ref[bb, r] < past

            @pl.when(old)
            def _():
                for cp in row_copies(bb, slot, r, True):
                    cp.start()

            @pl.when(jnp.logical_not(old))
            def _():
                for cp in row_copies(bb, slot, r, False):
                    cp.start()

            return c

        lax.fori_loop(0, topk, body, 0)

    @pl.when(b == 0)
    def _():
        start_all(0, 0)

    @pl.when(b + 1 < n_batch)
    def _():
        start_all(b + 1, (b + 1) % 2)

    slot = b % 2

    def wait_body(r, c):
        for cp in row_copies(b, slot, r, True):
            cp.wait()
        return c

    lax.fori_loop(0, topk, wait_body, 0)

    q = q_ref[0]
    k = kbuf[slot]
    logits = jnp.sum(k * q[None], axis=-1, keepdims=True) * (HEAD_DIM ** -0.5)
    m = jnp.max(logits, axis=0, keepdims=True)
    p = jnp.exp(logits - m)
    den = jnp.sum(p, axis=0)
    o_ref[0] = jnp.sum(p * vbuf[slot], axis=0) / den


def sample_attend(idx, page_table, q, k_new, v_new, cache_k, cache_v):
    bsz, h, dh = q.shape
    topk = idx.shape[1]
    past = page_table.shape[1] * PAGE_SIZE
    grid_spec = pltpu.PrefetchScalarGridSpec(
        num_scalar_prefetch=2,
        grid=(bsz,),
        in_specs=[pl.BlockSpec((1, h, dh), lambda i, a, c: (i, 0, 0)),
                  pl.BlockSpec(memory_space=pl.ANY), pl.BlockSpec(memory_space=pl.ANY),
                  pl.BlockSpec(memory_space=pl.ANY), pl.BlockSpec(memory_space=pl.ANY)],
        out_specs=pl.BlockSpec((1, h, dh), lambda i, a, c: (i, 0, 0)),
        scratch_shapes=[pltpu.VMEM((2, topk, h, dh), F32), pltpu.VMEM((2, topk, h, dh), F32),
                        pltpu.SemaphoreType.DMA((2,)), pltpu.SemaphoreType.DMA((2,))],
    )
    return pl.pallas_call(
        functools.partial(_sample_attend_kernel, topk=topk, past=past, n_batch=bsz),
        grid_spec=grid_spec,
        out_shape=jax.ShapeDtypeStruct((bsz, h, dh), F32),
        compiler_params=_cparams(("arbitrary",)),
        name="sample_attend",
    )(idx, page_table, q, k_new, v_new, cache_k, cache_v)


def _gate_merge_kernel(s_ref, a_ref, ws_ref, wa_ref, gs_ref, ga_ref, o_ref):
    ms = jnp.dot(s_ref[...], ws_ref[...], preferred_element_type=F32)
    ma = jnp.dot(a_ref[...], wa_ref[...], preferred_element_type=F32)
    o_ref[...] = (gs_ref[...] * ms + ga_ref[...] * ma).astype(o_ref.dtype)


def gate_merge(ssm_out, att, w_bs, w_ba, gates, tm, tn):
    m, kdim = ssm_out.shape
    d = w_bs.shape[1]
    nj = d // tn
    return pl.pallas_call(
        _gate_merge_kernel,
        grid=(m // tm, nj),
        in_specs=[pl.BlockSpec((tm, kdim), lambda i, j: (i, 0)), pl.BlockSpec((tm, kdim), lambda i, j: (i, 0)),
                  pl.BlockSpec((kdim, tn), lambda i, j: (0, j)), pl.BlockSpec((kdim, tn), lambda i, j: (0, j)),
                  pl.BlockSpec((tm, tn), lambda i, j: (i, j)), pl.BlockSpec((tm, tn), lambda i, j: (i, j + nj))],
        out_specs=pl.BlockSpec((tm, tn), lambda i, j: (i, j)),
        out_shape=jax.ShapeDtypeStruct((m, d), BF16),
        compiler_params=_cparams(("parallel", "parallel")),
        name="gate_merge",
    )(ssm_out, att, w_bs, w_ba, gates, gates)


def _out_residual_kernel(m_ref, w_ref, x_ref, o_ref):
    o_ref[...] = x_ref[...] + jnp.dot(m_ref[...], w_ref[...], preferred_element_type=F32)


def out_residual(merged, w_out, x2d, tm, tn):
    m, d = x2d.shape
    return pl.pallas_call(
        _out_residual_kernel,
        grid=(m // tm, d // tn),
        in_specs=[pl.BlockSpec((tm, d), lambda i, j: (i, 0)), pl.BlockSpec((d, tn), lambda i, j: (0, j)),
                  pl.BlockSpec((tm, tn), lambda i, j: (i, j))],
        out_specs=pl.BlockSpec((tm, tn), lambda i, j: (i, j)),
        out_shape=jax.ShapeDtypeStruct((m, d), F32),
        compiler_params=_cparams(("parallel", "parallel")),
        name="out_residual",
    )(merged, w_out, x2d)


def _ffn_kernel(*refs, seq_mode, tm, n_ff_tiles):
    if seq_mode:
        (x_ref, gf_ref, gfin_ref, wa_ref, wb_ref, cwa_ref, cwb_ref, cba_ref, cbb_ref, wd_ref, pa_ref, pb_ref,
         y_ref, ca_ref, cb_ref, h_scr, acc_scr, car_a, car_b) = refs
        i = pl.program_id(1)
        j = pl.program_id(2)
    else:
        (x_ref, gf_ref, gfin_ref, wa_ref, wb_ref, cwa_ref, cwb_ref, cba_ref, cbb_ref, wd_ref,
         p2a_ref, p1a_ref, p2b_ref, p1b_ref, y_ref, ca_ref, cb_ref, h_scr, acc_scr) = refs
        j = pl.program_id(1)

    @pl.when(j == 0)
    def _():
        x = x_ref[0]
        ms = jnp.mean(x * x, axis=-1, keepdims=True)
        h_scr[...] = (x * lax.rsqrt(ms + RMS_EPS) * gf_ref[...]).astype(BF16)
        acc_scr[...] = jnp.zeros_like(acc_scr)

    h = h_scr[...]

    def conv_half(w_ref, cw_ref, cbias_ref, out_ref, prev):
        up = jnp.dot(h, w_ref[...], preferred_element_type=F32)
        cw = cw_ref[...]
        if seq_mode:
            prev_ref, car = prev

            @pl.when(i == 0)
            def _():
                car[j] = prev_ref[0]

            c2 = car[j]
            row = lax.broadcasted_iota(jnp.int32, up.shape, 0)
            r6 = jnp.broadcast_to(c2[0:1, :], up.shape)
            r7 = jnp.broadcast_to(c2[1:2, :], up.shape)
            p1 = jnp.where(row == 0, r7, pltpu.roll(up, 1, 0))
            p2 = jnp.where(row == 0, r6, jnp.where(row == 1, r7, pltpu.roll(up, 2, 0)))
            car[j] = up[tm - 2:, :]
            out_ref[0, 0] = up[tm - 2:, :]
        else:
            p2_ref, p1_ref = prev
            p2, p1 = p2_ref[...], p1_ref[...]
            out_ref[...] = up
        return cbias_ref[...] + cw[0:1, :] * p2 + cw[1:2, :] * p1 + cw[2:3, :] * up

    if seq_mode:
        a = conv_half(wa_ref, cwa_ref, cba_ref, ca_ref, (pa_ref, car_a))
        g = conv_half(wb_ref, cwb_ref, cbb_ref, cb_ref, (pb_ref, car_b))
    else:
        a = conv_half(wa_ref, cwa_ref, cba_ref, ca_ref, (p2a_ref, p1a_ref))
        g = conv_half(wb_ref, cwb_ref, cbb_ref, cb_ref, (p2b_ref, p1b_ref))
    act = (jax.nn.silu(a) * g).astype(BF16)
    acc_scr[...] += jnp.dot(act, wd_ref[...], preferred_element_type=F32)

    @pl.when(j == n_ff_tiles - 1)
    def _():
        x2 = x_ref[0] + acc_scr[...]
        ms = jnp.mean(x2 * x2, axis=-1, keepdims=True)
        y_ref[0] = x2 * lax.rsqrt(ms + RMS_EPS) * gfin_ref[...]


def conv_ffn_final(x, conv_prev, g_ffn, g_final, w_up, conv_w, conv_b, w_down, tm, tf, seq_mode):
    b, t, d = x.shape
    f = w_down.shape[0]
    nf = f // tf
    gf = g_ffn.reshape(1, d).astype(F32)
    gfin = g_final.reshape(1, d).astype(F32)
    cb2 = conv_b.reshape(1, 2 * f).astype(F32)
    conv_w = conv_w.astype(F32)
    kern = functools.partial(_ffn_kernel, seq_mode=seq_mode, tm=tm, n_ff_tiles=nf)
    if seq_mode:
        nt = t // tm
        wspec = lambda blk, off: pl.BlockSpec(blk, lambda bi, i, j: (0, j + off))
        in_specs = [pl.BlockSpec((1, tm, d), lambda bi, i, j: (bi, i, 0)),
                    pl.BlockSpec((1, d), lambda bi, i, j: (0, 0)), pl.BlockSpec((1, d), lambda bi, i, j: (0, 0)),
                    wspec((d, tf), 0), wspec((d, tf), nf), wspec((CONV_W, tf), 0), wspec((CONV_W, tf), nf),
                    wspec((1, tf), 0), wspec((1, tf), nf),
                    pl.BlockSpec((tf, d), lambda bi, i, j: (j, 0)),
                    pl.BlockSpec((1, CONV_W - 1, tf), lambda bi, i, j: (bi, 0, j)),
                    pl.BlockSpec((1, CONV_W - 1, tf), lambda bi, i, j: (bi, 0, j + nf))]
        out_specs = [pl.BlockSpec((1, tm, d), lambda bi, i, j: (bi, i, 0)),
                     pl.BlockSpec((1, 1, CONV_W - 1, tf), lambda bi, i, j: (bi, i, 0, j)),
                     pl.BlockSpec((1, 1, CONV_W - 1, tf), lambda bi, i, j: (bi, i, 0, j))]
        out_shape = [jax.ShapeDtypeStruct((b, t, d), F32), jax.ShapeDtypeStruct((b, nt, CONV_W - 1, f), F32),
                     jax.ShapeDtypeStruct((b, nt, CONV_W - 1, f), F32)]
        scratch = [pltpu.VMEM((tm, d), BF16), pltpu.VMEM((tm, d), F32),
                   pltpu.VMEM((nf, CONV_W - 1, tf), F32), pltpu.VMEM((nf, CONV_W - 1, tf), F32)]
        y, ca, cb = pl.pallas_call(
            kern, grid=(b, nt, nf), in_specs=in_specs, out_specs=out_specs, out_shape=out_shape,
            scratch_shapes=scratch, compiler_params=_cparams(("arbitrary", "arbitrary", "arbitrary")),
            name="conv_ffn_seq",
        )(x, gf, gfin, w_up, w_up, conv_w, conv_w, cb2, cb2, w_down, conv_prev, conv_prev)
        return y, jnp.concatenate([ca[:, -1], cb[:, -1]], axis=-1)
    p2, p1 = conv_prev[:, 0, :], conv_prev[:, 1, :]
    wspec = lambda blk, off: pl.BlockSpec(blk, lambda i, j: (0, j + off))
    in_specs = [pl.BlockSpec((1, tm, d), lambda i, j: (0, i, 0)),
                pl.BlockSpec((1, d), lambda i, j: (0, 0)), pl.BlockSpec((1, d), lambda i, j: (0, 0)),
                wspec((d, tf), 0), wspec((d, tf), nf), wspec((CONV_W, tf), 0), wspec((CONV_W, tf), nf),
                wspec((1, tf), 0), wspec((1, tf), nf),
                pl.BlockSpec((tf, d), lambda i, j: (j, 0)),
                pl.BlockSpec((tm, tf), lambda i, j: (i, j)), pl.BlockSpec((tm, tf), lambda i, j: (i, j)),
                pl.BlockSpec((tm, tf), lambda i, j: (i, j + nf)), pl.BlockSpec((tm, tf), lambda i, j: (i, j + nf))]
    out_specs = [pl.BlockSpec((1, tm, d), lambda i, j: (0, i, 0)),
                 pl.BlockSpec((tm, tf), lambda i, j: (i, j)), pl.BlockSpec((tm, tf), lambda i, j: (i, j))]
    out_shape = [jax.ShapeDtypeStruct((1, t, d), F32), jax.ShapeDtypeStruct((t, f), F32),
                 jax.ShapeDtypeStruct((t, f), F32)]
    scratch = [pltpu.VMEM((tm, d), BF16), pltpu.VMEM((tm, d), F32)]
    y, ua, ub = pl.pallas_call(
        kern, grid=(t // tm, nf), in_specs=in_specs, out_specs=out_specs, out_shape=out_shape,
        scratch_shapes=scratch, compiler_params=_cparams(("arbitrary", "arbitrary")),
        name="conv_ffn_step",
    )(x, gf, gfin, w_up, w_up, conv_w, conv_w, cb2, cb2, w_down, p2, p1, p2, p1)
    return y, jnp.stack([p1, jnp.concatenate([ua, ub], axis=-1)], axis=1)


def _split_w_in(w_in, d_model, n_ssm, n_att):
    sizes = [n_ssm, n_att, n_att, n_att, H_IDX * D_IDX, D_IDX, H_IDX, d_model, d_model]
    offs = np.concatenate([[0], np.cumsum(sizes)]).tolist()
    col = lambda i: w_in[:, offs[i]:offs[i + 1]]
    w_uv = jnp.concatenate([col(0), col(3)], axis=1).astype(BF16)
    w_qk = jnp.concatenate([col(1), col(2)], axis=1).astype(BF16)
    w_qi = col(4).astype(BF16)
    pad = jnp.zeros((w_in.shape[0], LANES - D_IDX - H_IDX), w_in.dtype)
    w_kw = jnp.concatenate([col(5), col(6), pad], axis=1).astype(BF16)
    w_g = jnp.concatenate([col(7), col(8)], axis=1).astype(BF16)
    return w_uv, w_qk, w_qi, w_kw, w_g


def _mix_inputs(x2d, pos, n_pos_tiles, tm, norm_mix, w_groups, n_ssm, n_att):
    w_uv, w_qk, w_qi, w_kw, w_g = w_groups
    h = rmsnorm_bf16(x2d, norm_mix, tm)
    tn = 512
    uv = project(h, w_uv, tm, tn)
    qk = project(h, w_qk, tm, tn, "rope", HEAD_DIM // ROPE_FRACTION // 2,
                 rope_tables(pos, HEAD_DIM, tn), n_pos_tiles)
    qi = project(h, w_qi, tm, tn, "rope", D_IDX // ROPE_FRACTION // 2, rope_tables(pos, D_IDX, tn), n_pos_tiles)
    extra = jnp.concatenate([jnp.full((H_IDX,), H_IDX ** -0.5, F32), jnp.zeros((LANES - D_IDX - H_IDX,), F32)])
    kw = project(h, w_kw, tm, LANES, "rope", D_IDX // ROPE_FRACTION // 2,
                 rope_tables(pos, D_IDX, LANES, extra), n_pos_tiles)
    gates = project(h, w_g, tm, tn, "sigmoid")
    u, v = uv[:, :n_ssm], uv[:, n_ssm:]
    q, k = qk[:, :n_att], qk[:, n_att:]
    ki, wi = kw[:, :D_IDX], kw[:, D_IDX:D_IDX + H_IDX]
    return u, q, k, v, qi, ki, wi, gates


def kernel(x_prompt, x_sample, cache_k, cache_v, cache_idx_k, page_table, state_ssm_re, state_ssm_im,
           state_ffn_conv, norm_mix, w_in, ssm_A_re, ssm_A_im, ssm_log_dt, ssm_B_re, ssm_B_im, ssm_C_re,
           ssm_C_im, ssm_D, w_glu, w_branch_ssm, w_branch_att, w_out, norm_ffn, w_up, ffn_conv_w,
           ffn_conv_b, w_down, norm_final):
    bp, t, d = x_prompt.shape
    bs, ts, _ = x_sample.shape
    n_ssm = ssm_D.shape[0]
    n_att = w_branch_att.shape[0]
    g = n_ssm // SSM_GROUP
    h_att = n_att // HEAD_DIM
    past = page_table.shape[1] * PAGE_SIZE
    assert ts == 1, "sample group is a single decode step"

    w_groups = _split_w_in(w_in, d, n_ssm, n_att)
    w_glu_b, w_bs, w_ba, w_out_b = (w.astype(BF16) for w in (w_glu, w_branch_ssm, w_branch_att, w_out))
    w_up_b, w_down_b = w_up.astype(BF16), w_down.astype(BF16)
    ssm_params = (ssm_A_re, ssm_A_im, ssm_log_dt, ssm_B_re, ssm_B_im, ssm_C_re, ssm_C_im)

    mp = bp * t
    tm = 1024
    xp2 = x_prompt.reshape(mp, d)
    pos_p = jnp.arange(t, dtype=jnp.int32)
    u, q, k, v, qi, ki, wi, gates = _mix_inputs(xp2, pos_p, t // tm, tm, norm_mix, w_groups, n_ssm, n_att)
    width_p = max(LANES, SSM_CHUNK * SSM_GROUP)
    mats_p = ssm_matrices(*ssm_params, SSM_CHUNK, width_p)
    zeros_s = jnp.zeros((bp, g, SSM_STATE), F32)
    y_ssm, s_re_p, s_im_p = s5_scan(u.reshape(bp, t, n_ssm), zeros_s, zeros_s, mats_p, SSM_CHUNK, False)
    ssm_out = ssm_glu(y_ssm.reshape(mp, n_ssm), u, ssm_D, w_glu_b, 512)
    qi_h = qi.astype(BF16).reshape(bp, t, H_IDX, D_IDX).transpose(0, 2, 1, 3)
    att = prompt_attention(qi_h, ki.astype(BF16).reshape(bp, t, D_IDX), wi.reshape(bp, t, H_IDX),
                           q.astype(BF16).reshape(bp, t, n_att), k.astype(BF16).reshape(bp, t, n_att),
                           v.astype(BF16).reshape(bp, t, n_att), 128)
    merged = gate_merge(ssm_out, att.reshape(mp, n_att), w_bs, w_ba, gates, 512, 512)
    x1 = out_residual(merged, w_out_b, xp2, 512, 512)
    zeros_c = jnp.zeros((bp, CONV_W - 1, w_up.shape[1]), F32)
    y_prompt, conv_p = conv_ffn_final(x1.reshape(bp, t, d), zeros_c, norm_ffn, norm_final, w_up_b, ffn_conv_w,
                                      ffn_conv_b, w_down_b, 512, 512, True)
    k_p = k.reshape(bp, t, h_att, HEAD_DIM)
    v_p = v.reshape(bp, t, h_att, HEAD_DIM)
    ki_p = ki.reshape(bp, t, D_IDX)

    xs2 = x_sample.reshape(bs, d)
    pos_s = jnp.full((bs,), past, jnp.int32)
    u, q, k, v, qi, ki, wi, gates = _mix_inputs(xs2, pos_s, 1, bs, norm_mix, w_groups, n_ssm, n_att)
    mats_s = ssm_matrices(*ssm_params, 1, LANES)
    y_ssm, s_re_s, s_im_s = s5_scan(u.reshape(bs, 1, n_ssm), state_ssm_re, state_ssm_im, mats_s, 1, True)
    ssm_out = ssm_glu(y_ssm.reshape(bs, n_ssm), u, ssm_D, w_glu_b, bs)
    scores = sample_scores(page_table, qi.astype(BF16).reshape(bs, H_IDX, D_IDX), wi.reshape(bs, H_IDX, 1),
                           ki.reshape(bs, 1, D_IDX), cache_idx_k)
    topk = min(TOPK_MAX, (past + 1) // 4)
    hl = sample_select(scores.reshape(bs, past + LANES), topk)
    idx = (hl[:, :, 0] * PAGE_SIZE + hl[:, :, 1]).astype(jnp.int32)
    att = sample_attend(idx, page_table, q.reshape(bs, h_att, HEAD_DIM), k.reshape(bs, h_att, HEAD_DIM),
                        v.reshape(bs, h_att, HEAD_DIM), cache_k, cache_v)
    merged = gate_merge(ssm_out, att.reshape(bs, n_att).astype(BF16), w_bs, w_ba, gates, bs, 512)
    x1 = out_residual(merged, w_out_b, xs2, bs, 512)
    y_s, conv_s = conv_ffn_final(x1.reshape(1, bs, d), state_ffn_conv, norm_ffn, norm_final, w_up_b, ffn_conv_w,
                                 ffn_conv_b, w_down_b, bs, 512, False)
    y_sample = y_s.reshape(bs, 1, d)
    k_s = k.reshape(bs, 1, h_att, HEAD_DIM)
    v_s = v.reshape(bs, 1, h_att, HEAD_DIM)
    ki_s = ki.reshape(bs, 1, D_IDX)

    return (y_prompt, y_sample, k_p, v_p, ki_p, k_s, v_s, ki_s, s_re_p, s_im_p, s_re_s, s_im_s, conv_p, conv_s)
```

```python
import functools

import numpy as np
import jax
import jax.numpy as jnp
from jax import lax
from jax.experimental import pallas as pl
from jax.experimental.pallas import tpu as pltpu

F32 = jnp.float32
BF16 = jnp.bfloat16

HEAD_DIM = 128
SSM_GROUP = 16
SSM_STATE = 64
H_IDX = 16
D_IDX = 64
TOPK_MAX = 256
ROPE_THETA = 500000.0
ROPE_FRACTION = 4
CONV_W = 3
RMS_EPS = 1e-6
NEG_INF = -1e30
PAGE_SIZE = 128

LANES = 128
SUBLANES = 8
MXU_WIDTH = 256
GROUPS_PER_TILE = LANES // SSM_GROUP
SSM_CHUNK = 8
VMEM_LIMIT = 56 * 1024 * 1024
INT_MIN = -2 ** 31
NT_DIMS = (((1,), (1,)), ((), ()))


def _cparams(sem):
    return pltpu.CompilerParams(dimension_semantics=sem, vmem_limit_bytes=VMEM_LIMIT)


def _rmsnorm_kernel(x_ref, g_ref, o_ref):
    x = x_ref[...]
    ms = jnp.mean(x * x, axis=-1, keepdims=True)
    o_ref[...] = (x * lax.rsqrt(ms + RMS_EPS) * g_ref[...]).astype(o_ref.dtype)


def rmsnorm_bf16(x2d, g, tm):
    m, d = x2d.shape
    return pl.pallas_call(
        _rmsnorm_kernel,
        grid=(m // tm,),
        in_specs=[pl.BlockSpec((tm, d), lambda i: (i, 0)), pl.BlockSpec((1, d), lambda i: (0, 0))],
        out_specs=pl.BlockSpec((tm, d), lambda i: (i, 0)),
        out_shape=jax.ShapeDtypeStruct((m, d), BF16),
        compiler_params=_cparams(("parallel",)),
        name="rmsnorm",
    )(x2d, g.reshape(1, d))


def _proj_kernel(h_ref, w_ref, *rest, mode, shift, outs, n_tab):
    tabs, o_refs = rest[:n_tab], rest[n_tab:]
    j = pl.program_id(1)
    z = jnp.dot(h_ref[...], w_ref[...], preferred_element_type=F32)
    if mode == "sigmoid":
        z = jax.nn.sigmoid(z)
    elif mode == "rope":
        c_ref, s1_ref, s2_ref = tabs
        tn = z.shape[-1]
        z = z * c_ref[...] + pltpu.roll(z, tn - shift, 1) * s1_ref[...] + pltpu.roll(z, shift, 1) * s2_ref[...]
    for (start, cnt, _, kind), o_ref in zip(outs, o_refs):

        def write(o_ref=o_ref, kind=kind):
            if kind == "heads":
                for hh in range(z.shape[1] // D_IDX):
                    o_ref[0, hh] = z[:, hh * D_IDX:(hh + 1) * D_IDX].astype(o_ref.dtype)
            else:
                o_ref[...] = z.astype(o_ref.dtype)

        pl.when(jnp.logical_and(j >= start, j < start + cnt))(write)


def project(h, w, tm, tn, outs, mode="plain", shift=0, tables=None, n_pos_tiles=1):
    m, k = h.shape
    n = w.shape[1]
    in_specs = [pl.BlockSpec((tm, k), lambda i, j: (i, 0)), pl.BlockSpec((k, tn), lambda i, j: (0, j))]
    args = [h, w]
    tables = tables or ()
    for t in tables:
        in_specs.append(pl.BlockSpec((tm, tn), lambda i, j: (i % n_pos_tiles, 0)))
        args.append(t)
    out_specs, out_shape = [], []
    for start, cnt, dtype, kind in outs:
        if kind == "heads":
            hpt = tn // D_IDX
            out_specs.append(pl.BlockSpec(
                (1, hpt, tm, D_IDX),
                lambda i, j, s=start, c=cnt: (i // n_pos_tiles, jnp.clip(j - s, 0, c - 1), i % n_pos_tiles, 0)))
            out_shape.append(jax.ShapeDtypeStruct((m // (n_pos_tiles * tm), cnt * hpt, n_pos_tiles * tm, D_IDX), dtype))
        else:
            out_specs.append(pl.BlockSpec((tm, tn), lambda i, j, s=start, c=cnt: (i, jnp.clip(j - s, 0, c - 1))))
            out_shape.append(jax.ShapeDtypeStruct((m, cnt * tn), dtype))
    return pl.pallas_call(
        functools.partial(_proj_kernel, mode=mode, shift=shift, outs=tuple(outs), n_tab=len(tables)),
        grid=(m // tm, n // tn),
        in_specs=in_specs,
        out_specs=out_specs,
        out_shape=out_shape,
        compiler_params=_cparams(("parallel", "arbitrary")),
        name="proj_" + mode,
    )(*args)


def rope_tables(pos, head, tn, extra=None):
    r = head // ROPE_FRACTION
    half = r // 2
    inv = ROPE_THETA ** (-jnp.arange(half, dtype=F32) * 2.0 / r)
    ang = pos.astype(F32)[:, None] * inv[None, :]
    cos, sin = jnp.cos(ang), jnp.sin(ang)
    t = pos.shape[0]
    zeros = jnp.zeros((t, head - r), F32)
    zh = jnp.zeros((t, half), F32)
    c = jnp.concatenate([cos, cos, jnp.ones((t, head - r), F32)], axis=1)
    s1 = jnp.concatenate([-sin, zh, zeros], axis=1)
    s2 = jnp.concatenate([zh, sin, zeros], axis=1)
    if extra is None:
        reps = tn // head
        return tuple(jnp.tile(a, (1, reps)) for a in (c, s1, s2))
    pad = jnp.zeros((t, tn - head), F32)
    return (jnp.concatenate([c, jnp.broadcast_to(extra[None, :], (t, tn - head))], axis=1),
            jnp.concatenate([s1, pad], axis=1), jnp.concatenate([s2, pad], axis=1))


def ssm_tile_matrices(a_re, a_im, log_dt, b_re, b_im, c_re, c_im, chunk):
    hp = lax.Precision.HIGHEST
    g, p = a_re.shape
    gt = GROUPS_PER_TILE
    nq = g // gt
    a = lax.complex(a_re.astype(F32), a_im.astype(F32))
    dt = jnp.exp(log_dt.astype(F32))[:, None]
    adt = a * dt
    a_bar = jnp.exp(adt)
    b_bar = ((a_bar - 1.0) / a)[..., None] * lax.complex(b_re.astype(F32), b_im.astype(F32))
    cc = lax.complex(c_re.astype(F32), c_im.astype(F32))
    steps = jnp.arange(chunk + 1, dtype=F32)
    pw = jnp.exp(adt[:, None, :] * steps[None, :, None].astype(jnp.complex64))
    kd = jnp.real(jnp.einsum("gcp,gdp,gpe->gdce", cc, pw[:, :chunk], b_bar, precision=hp))
    jj = jnp.arange(chunk)[:, None]
    tt = jnp.arange(chunk)[None, :]
    dd = tt - jj
    ktg = jnp.where((dd >= 0)[None, :, :, None, None], kd[:, jnp.maximum(dd, 0)], 0.0)
    w1g = pw[:, chunk - 1 - jnp.arange(chunk)][:, :, None, :] * b_bar.transpose(0, 2, 1)[:, None]
    m2g = cc.transpose(0, 2, 1)[:, :, None, :] * pw[:, 1:chunk + 1].transpose(0, 2, 1)[:, :, :, None]
    eye = jnp.eye(gt, dtype=F32)
    cw = chunk * LANES

    def tile_w1(x):
        return jnp.einsum("qgjcp,gh->qjgchp", x.reshape(nq, gt, chunk, SSM_GROUP, p), eye).reshape(nq, cw, gt * p)

    def tile_w2(x):
        return jnp.einsum("qgptc,gh->qgpthc", x.reshape(nq, gt, p, chunk, SSM_GROUP), eye).reshape(nq, gt * p, cw)

    kt = jnp.einsum("qgjtoi,gh->qjgitho", ktg.reshape(nq, gt, chunk, chunk, SSM_GROUP, SSM_GROUP), eye)
    kt = kt.reshape(nq, cw, cw)
    al = pw[:, chunk].reshape(nq, 1, gt * p)
    return dict(w1r=tile_w1(jnp.real(w1g)), w1i=tile_w1(jnp.imag(w1g)), kt=kt,
                w2r=tile_w2(jnp.real(m2g)), w2i=tile_w2(-jnp.imag(m2g)), ar=jnp.real(al), ai=jnp.imag(al))


def _s5_seq_kernel(u_ref, w1r_ref, w1i_ref, kt_ref, w2r_ref, w2i_ref, ar_ref, ai_ref, d_ref, s0r_ref, s0i_ref,
                   o_ref, sfr_ref, sfi_ref, ucat, slr, sli, spr, spi, yc, yb, st_r, st_i, *, chunk, n_rows, nb):
    ts = pl.program_id(1)
    ns = st_r.shape[-1]

    @pl.when(ts == 0)
    def _():
        st_r[...] = s0r_ref[...]
        st_i[...] = s0i_ref[...]

    for b in range(nb):
        for tl in range(chunk):
            ucat[:, b, tl * LANES:(tl + 1) * LANES] = u_ref[b, pl.ds(tl, n_rows, stride=chunk), :]
    x = ucat[...].reshape(n_rows * nb, chunk * LANES).astype(BF16)
    slr[...] = jnp.dot(x, w1r_ref[0], preferred_element_type=F32).reshape(n_rows, nb, ns)
    sli[...] = jnp.dot(x, w1i_ref[0], preferred_element_type=F32).reshape(n_rows, nb, ns)
    ar = jnp.broadcast_to(ar_ref[0], (nb, ns))
    ai = jnp.broadcast_to(ai_ref[0], (nb, ns))

    def step(n, carry):
        sr, si = carry
        spr[n] = sr
        spi[n] = si
        return ar * sr - ai * si + slr[n], ar * si + ai * sr + sli[n]

    sr, si = lax.fori_loop(0, n_rows, step, (st_r[...], st_i[...]))
    st_r[...] = sr
    st_i[...] = si
    sfr_ref[...] = sr
    sfi_ref[...] = si
    y = (jnp.dot(x, kt_ref[0], preferred_element_type=F32)
         + jnp.dot(spr[...].reshape(n_rows * nb, ns).astype(BF16), w2r_ref[0], preferred_element_type=F32)
         + jnp.dot(spi[...].reshape(n_rows * nb, ns).astype(BF16), w2i_ref[0], preferred_element_type=F32))
    yc[...] = y.reshape(n_rows, nb, chunk * LANES)
    for b in range(nb):
        for tl in range(chunk):
            yb[b, pl.ds(tl, n_rows, stride=chunk), :] = yc[:, b, tl * LANES:(tl + 1) * LANES]
    o_ref[...] = jax.nn.gelu(yb[...] + d_ref[...] * u_ref[...]).astype(o_ref.dtype)


def s5_sequence(u, s0_re, s0_im, d_skip, mats, chunk, t_seg):
    b, t, n = u.shape
    g, p = s0_re.shape[1:]
    nq = n // LANES
    ns = GROUPS_PER_TILE * p
    cw = chunk * LANES
    n_rows = t_seg // chunk
    wspec = lambda r, c: pl.BlockSpec((1, r, c), lambda q, s: (q, 0, 0))
    state_spec = pl.BlockSpec((b, ns), lambda q, s: (0, q))
    y, sr, si = pl.pallas_call(
        functools.partial(_s5_seq_kernel, chunk=chunk, n_rows=n_rows, nb=b),
        grid=(nq, t // t_seg),
        in_specs=[pl.BlockSpec((b, t_seg, LANES), lambda q, s: (0, s, q)),
                  wspec(cw, ns), wspec(cw, ns), wspec(cw, cw), wspec(ns, cw), wspec(ns, cw),
                  wspec(1, ns), wspec(1, ns), pl.BlockSpec((1, LANES), lambda q, s: (0, q)), state_spec, state_spec],
        out_specs=[pl.BlockSpec((b, t_seg, LANES), lambda q, s: (0, s, q)), state_spec, state_spec],
        out_shape=[jax.ShapeDtypeStruct((b, t, n), BF16), jax.ShapeDtypeStruct((b, g * p), F32),
                   jax.ShapeDtypeStruct((b, g * p), F32)],
        scratch_shapes=[pltpu.VMEM((n_rows, b, cw), F32),
                        pltpu.VMEM((n_rows, b, ns), F32), pltpu.VMEM((n_rows, b, ns), F32),
                        pltpu.VMEM((n_rows, b, ns), F32), pltpu.VMEM((n_rows, b, ns), F32),
                        pltpu.VMEM((n_rows, b, cw), F32), pltpu.VMEM((b, t_seg, LANES), F32),
                        pltpu.VMEM((b, ns), F32), pltpu.VMEM((b, ns), F32)],
        compiler_params=_cparams(("parallel", "arbitrary")),
        name="s5_sequence",
    )(u, mats["w1r"].astype(BF16), mats["w1i"].astype(BF16), mats["kt"].astype(BF16), mats["w2r"].astype(BF16),
      mats["w2i"].astype(BF16), mats["ar"], mats["ai"], d_skip.reshape(1, n).astype(F32),
      s0_re.astype(F32).reshape(b, g * p), s0_im.astype(F32).reshape(b, g * p))
    return y, sr.reshape(b, g, p), si.reshape(b, g, p)


def _s5_step_kernel(u_ref, w1r_ref, w1i_ref, kt_ref, w2r_ref, w2i_ref, ar_ref, ai_ref, d_ref, s0r_ref, s0i_ref,
                    o_ref, sfr_ref, sfi_ref):
    dot = functools.partial(jnp.dot, preferred_element_type=F32, precision=lax.Precision.HIGHEST)
    x = u_ref[...]
    sr, si = s0r_ref[...], s0i_ref[...]
    ar, ai = ar_ref[0], ai_ref[0]
    sfr_ref[...] = ar * sr - ai * si + dot(x, w1r_ref[0])
    sfi_ref[...] = ar * si + ai * sr + dot(x, w1i_ref[0])
    y = dot(x, kt_ref[0]) + dot(sr, w2r_ref[0]) + dot(si, w2i_ref[0])
    o_ref[...] = jax.nn.gelu(y + d_ref[...] * x).astype(o_ref.dtype)


def s5_step(u, s0_re, s0_im, d_skip, mats):
    b, n = u.shape
    g, p = s0_re.shape[1:]
    nq = n // LANES
    ns = GROUPS_PER_TILE * p
    wspec = lambda r, c: pl.BlockSpec((1, r, c), lambda q: (q, 0, 0))
    state_spec = pl.BlockSpec((b, ns), lambda q: (0, q))
    lane_spec = pl.BlockSpec((b, LANES), lambda q: (0, q))
    y, sr, si = pl.pallas_call(
        _s5_step_kernel,
        grid=(nq,),
        in_specs=[lane_spec, wspec(LANES, ns), wspec(LANES, ns), wspec(LANES, LANES), wspec(ns, LANES),
                  wspec(ns, LANES), wspec(1, ns), wspec(1, ns), pl.BlockSpec((1, LANES), lambda q: (0, q)),
                  state_spec, state_spec],
        out_specs=[lane_spec, state_spec, state_spec],
        out_shape=[jax.ShapeDtypeStruct((b, n), BF16), jax.ShapeDtypeStruct((b, g * p), F32),
                   jax.ShapeDtypeStruct((b, g * p), F32)],
        compiler_params=_cparams(("parallel",)),
        name="s5_step",
    )(u, mats["w1r"], mats["w1i"], mats["kt"], mats["w2r"], mats["w2i"], mats["ar"], mats["ai"],
      d_skip.reshape(1, n).astype(F32), s0_re.astype(F32).reshape(b, g * p), s0_im.astype(F32).reshape(b, g * p))
    return y, sr.reshape(b, g, p), si.reshape(b, g, p)


def _glu_kernel(y_ref, w_ref, o_ref):
    n = o_ref.shape[-1]
    z = jnp.dot(y_ref[...], w_ref[...], preferred_element_type=F32)
    o_ref[...] = (z[:, :n] * jax.nn.sigmoid(z[:, n:])).astype(o_ref.dtype)


def ssm_glu(y, w_glu, tm):
    m, n = y.shape
    return pl.pallas_call(
        _glu_kernel,
        grid=(m // tm,),
        in_specs=[pl.BlockSpec((tm, n), lambda i: (i, 0)), pl.BlockSpec((n, 2 * n), lambda i: (0, 0))],
        out_specs=pl.BlockSpec((tm, n), lambda i: (i, 0)),
        out_shape=jax.ShapeDtypeStruct((m, n), BF16),
        compiler_params=_cparams(("parallel",)),
        name="ssm_glu",
    )(y, w_glu)


def _sortable_key(score):
    bits = pltpu.bitcast(score, jnp.int32)
    return jnp.where(bits < 0, bits ^ jnp.int32(0x7FFFFFFF), bits)


def _kth_largest_key(key, k):
    rows = key.shape[0]

    def body(i, res):
        cand = res + jnp.left_shift(jnp.int32(1), 31 - i)
        cnt = jnp.sum((key >= cand).astype(jnp.int32), axis=-1, keepdims=True)
        return jnp.where(cnt >= k, cand, res)

    return lax.fori_loop(0, 32, body, jnp.full((rows, 1), INT_MIN, jnp.int32))


def _exclusive_cumsum_lanes(flags_bf16):
    rows, n = flags_bf16.shape
    r = lax.broadcasted_iota(jnp.int32, (LANES, LANES), 0)
    c = lax.broadcasted_iota(jnp.int32, (LANES, LANES), 1)
    tri = jnp.where(r < c, 1.0, 0.0).astype(BF16)
    off = jnp.zeros((rows, 1), F32)
    out = []
    for i in range(n // LANES):
        chunk = flags_bf16[:, i * LANES:(i + 1) * LANES]
        out.append(jnp.dot(chunk, tri, preferred_element_type=F32) + off)
        off = off + jnp.sum(chunk.astype(F32), axis=-1, keepdims=True)
    return jnp.concatenate(out, axis=-1)


def _attend_keys(qi_ref, ki_ref, q_ref, k_ref, v_ref, o_ref, score_scr, bias_scr, wb_scr, *, qb, tq, lk, topk):
    n_chunks = lk // MXU_WIDTH
    qi_all = qi_ref[0].reshape(H_IDX * tq, D_IDX)

    def chunk_body(c, carry):
        k0 = pl.multiple_of(c * MXU_WIDTH, MXU_WIDTH)
        s = lax.dot_general(qi_all, ki_ref[0, pl.ds(k0, MXU_WIDTH), :], NT_DIMS, preferred_element_type=F32)
        acc = jnp.zeros((tq, MXU_WIDTH), F32)
        for h in range(H_IDX):
            wb = wb_scr[h]
            acc = acc + jnp.maximum(s[h * tq:(h + 1) * tq], 0.0) * jnp.concatenate([wb, wb], axis=-1)
        score_scr[c] = acc
        return carry

    lax.fori_loop(0, n_chunks, chunk_body, 0)
    score = jnp.concatenate([score_scr[c] for c in range(n_chunks)], axis=-1)
    row = qb * tq + lax.broadcasted_iota(jnp.int32, (tq, lk), 0)
    col = lax.broadcasted_iota(jnp.int32, (tq, lk), 1)
    causal = col <= row
    key = _sortable_key(jnp.where(causal, score, -jnp.inf))
    thr = _kth_largest_key(key, topk)
    gt = key > thr
    eq = jnp.logical_and(key == thr, causal)
    need = topk - jnp.sum(gt.astype(jnp.int32), axis=-1, keepdims=True)
    n_eq = jnp.sum(eq.astype(jnp.int32), axis=-1, keepdims=True)
    bias_scr[:, :lk] = jnp.where(jnp.logical_and(causal, key >= thr), 0.0, NEG_INF)

    @pl.when(jnp.max(n_eq - need) > 0)
    def _():
        rank = _exclusive_cumsum_lanes(jnp.where(eq, 1.0, 0.0).astype(BF16))
        keep = jnp.logical_or(gt, jnp.logical_and(eq, rank < need.astype(F32)))
        bias_scr[:, :lk] = jnp.where(jnp.logical_and(causal, keep), 0.0, NEG_INF)

    bias = bias_scr[:, :lk]
    scale = HEAD_DIM ** -0.5
    for h in range(q_ref.shape[-1] // HEAD_DIM):
        sl = slice(h * HEAD_DIM, (h + 1) * HEAD_DIM)
        logits = lax.dot_general(q_ref[0, :, sl], k_ref[0, :lk, sl], NT_DIMS, preferred_element_type=F32) * scale + bias
        m = jnp.max(logits, axis=-1, keepdims=True)
        p = jnp.exp(logits - m)
        den = jnp.sum(p, axis=-1, keepdims=True)
        o = jnp.dot(p.astype(BF16), v_ref[0, :lk, sl], preferred_element_type=F32) / den
        o_ref[0, :, sl] = o.astype(o_ref.dtype)


def _prompt_attn_kernel(qi_ref, ki_ref, w_ref, q_ref, k_ref, v_ref, o_ref, score_scr, bias_scr, wb_scr,
                        *, tq, seq, topk, n_buckets):
    qb = pl.program_id(1)
    w = w_ref[0] * (D_IDX ** -0.5)
    for h in range(H_IDX):
        wb_scr[h] = jnp.broadcast_to(w[:, h:h + 1], (tq, LANES))
    per = (seq // tq) // n_buckets
    for bkt in range(n_buckets):
        body = functools.partial(_attend_keys, qi_ref, ki_ref, q_ref, k_ref, v_ref, o_ref, score_scr, bias_scr, wb_scr,
                                 qb=qb, tq=tq, lk=(bkt + 1) * per * tq, topk=topk)
        pl.when(qb // per == bkt)(body)


def prompt_attention(qi, ki, wi, q, k, v, tq, n_buckets):
    b, t, n = q.shape
    topk = min(TOPK_MAX, t // 4)
    return pl.pallas_call(
        functools.partial(_prompt_attn_kernel, tq=tq, seq=t, topk=topk, n_buckets=n_buckets),
        grid=(b, t // tq),
        in_specs=[pl.BlockSpec((1, H_IDX, tq, D_IDX), lambda i, j: (i, 0, j, 0)),
                  pl.BlockSpec((1, t, D_IDX), lambda i, j: (i, 0, 0)),
                  pl.BlockSpec((1, tq, H_IDX), lambda i, j: (i, j, 0)),
                  pl.BlockSpec((1, tq, n), lambda i, j: (i, j, 0)),
                  pl.BlockSpec((1, t, n), lambda i, j: (i, 0, 0)),
                  pl.BlockSpec((1, t, n), lambda i, j: (i, 0, 0))],
        out_specs=pl.BlockSpec((1, tq, n), lambda i, j: (i, j, 0)),
        out_shape=jax.ShapeDtypeStruct((b, t, n), BF16),
        scratch_shapes=[pltpu.VMEM((t // MXU_WIDTH, tq, MXU_WIDTH), F32), pltpu.VMEM((tq, t), F32),
                        pltpu.VMEM((H_IDX, tq, LANES), F32)],
        compiler_params=_cparams(("parallel", "parallel")),
        name="prompt_attn",
    )(qi, ki, wi, q, k, v)


def _sample_score_kernel(pt_ref, qi_ref, w_ref, kin_ref, cache_ref, o_ref, kbuf, sem, *, n_pages, n_batch):
    b = pl.program_id(0)
    past = n_pages * PAGE_SIZE

    def page_copy(bb, slot, pg):
        return pltpu.make_async_copy(cache_ref.at[pt_ref[bb, pg]],
                                     kbuf.at[slot, pl.ds(pl.multiple_of(pg * PAGE_SIZE, PAGE_SIZE), PAGE_SIZE)],
                                     sem.at[slot])

    def start_all(bb, slot):
        lax.fori_loop(0, n_pages, lambda pg, c: (page_copy(bb, slot, pg).start(), c)[1], 0)

    @pl.when(b == 0)
    def _():
        start_all(0, 0)

    @pl.when(b + 1 < n_batch)
    def _():
        start_all(b + 1, (b + 1) % 2)

    slot = b % 2
    lax.fori_loop(0, n_pages, lambda pg, c: (page_copy(b, slot, pg).wait(), c)[1], 0)

    qi = qi_ref[0]
    w = w_ref[0] * (D_IDX ** -0.5)
    step = 2048
    for c in range(past // step):
        kc = kbuf[slot, c * step:(c + 1) * step, :].astype(BF16)
        s = lax.dot_general(qi, kc, NT_DIMS, preferred_element_type=F32)
        o_ref[0, :, c * step:(c + 1) * step] = jnp.sum(jnp.maximum(s, 0.0) * w, axis=0, keepdims=True)
    s_new = jnp.sum(qi.astype(F32) * kin_ref[0].astype(BF16).astype(F32), axis=-1, keepdims=True)
    s_new = jnp.sum(jnp.maximum(s_new, 0.0) * w, axis=0, keepdims=True)
    lane = lax.broadcasted_iota(jnp.int32, (1, LANES), 1)
    o_ref[0, :, past:] = jnp.where(lane == 0, s_new, -jnp.inf)


def sample_scores(page_table, qi, wi, ki_new, cache_idx_k):
    bsz, n_pages = page_table.shape
    past = n_pages * PAGE_SIZE
    grid_spec = pltpu.PrefetchScalarGridSpec(
        num_scalar_prefetch=1,
        grid=(bsz,),
        in_specs=[pl.BlockSpec((1, H_IDX, D_IDX), lambda i, pt: (i, 0, 0)),
                  pl.BlockSpec((1, H_IDX, 1), lambda i, pt: (i, 0, 0)),
                  pl.BlockSpec((1, 1, D_IDX), lambda i, pt: (i, 0, 0)),
                  pl.BlockSpec(memory_space=pl.ANY)],
        out_specs=pl.BlockSpec((1, 1, past + LANES), lambda i, pt: (i, 0, 0)),
        scratch_shapes=[pltpu.VMEM((2, past, D_IDX), F32), pltpu.SemaphoreType.DMA((2,))],
    )
    return pl.pallas_call(
        functools.partial(_sample_score_kernel, n_pages=n_pages, n_batch=bsz),
        grid_spec=grid_spec,
        out_shape=jax.ShapeDtypeStruct((bsz, 1, past + LANES), F32),
        compiler_params=_cparams(("arbitrary",)),
        name="sample_scores",
    )(page_table, qi, wi, ki_new, cache_idx_k)


def _sample_select_kernel(score_ref, pos_ref, o_ref, rank_scr, *, topk):
    score = score_ref[...]
    bsz, n = score.shape
    key = _sortable_key(score)
    thr = _kth_largest_key(key, topk)
    gt = key > thr
    eq = key == thr
    need = topk - jnp.sum(gt.astype(jnp.int32), axis=-1, keepdims=True)
    rank_eq = _exclusive_cumsum_lanes(jnp.where(eq, 1.0, 0.0).astype(BF16))
    sel = jnp.logical_or(gt, jnp.logical_and(eq, rank_eq < need.astype(F32)))
    self32 = jnp.where(sel, 1.0, 0.0)
    rank = _exclusive_cumsum_lanes(self32.astype(BF16))
    rank_scr[...] = jnp.where(sel, rank, -1.0)
    slot_id = lax.broadcasted_iota(jnp.int32, (topk, 1), 0).astype(F32)
    n_split = 3
    step = n // n_split

    def body(bb, c):
        acc = jnp.zeros((topk, LANES), F32)
        for s in range(n_split):
            onehot = jnp.where(rank_scr[pl.ds(bb, 1), s * step:(s + 1) * step] == slot_id, 1.0, 0.0).astype(BF16)
            acc = acc + jnp.dot(onehot, pos_ref[s * step:(s + 1) * step, :], preferred_element_type=F32)
        o_ref[bb] = acc
        return c

    lax.fori_loop(0, bsz, body, 0)


def sample_select(scores, topk):
    bsz, n = scores.shape
    s = jnp.arange(n, dtype=jnp.int32)
    lane = jnp.arange(LANES, dtype=jnp.int32)[None, :]
    pos = jnp.where(lane == 0, (s // PAGE_SIZE)[:, None], jnp.where(lane == 1, (s % PAGE_SIZE)[:, None], 0)).astype(BF16)
    return pl.pallas_call(
        functools.partial(_sample_select_kernel, topk=topk),
        out_shape=jax.ShapeDtypeStruct((bsz, topk, LANES), F32),
        scratch_shapes=[pltpu.VMEM((bsz, n), F32)],
        compiler_params=pltpu.CompilerParams(vmem_limit_bytes=VMEM_LIMIT),
        name="sample_select",
    )(scores, pos)


def _sample_attend_kernel(idx_ref, pt_ref, q_ref, knew_ref, vnew_ref, ck_ref, cv_ref, o_ref, kbuf, vbuf, ksem, vsem,
                          *, topk, past, n_batch):
    b = pl.program_id(0)

    def row_copies(bb, slot, r, in_past):
        i = jnp.minimum(idx_ref[bb, r], past - 1)
        phys = pt_ref[bb, i // PAGE_SIZE]
        off = i % PAGE_SIZE
        ksrc = ck_ref.at[phys, off] if in_past else knew_ref.at[bb]
        vsrc = cv_ref.at[phys, off] if in_past else vnew_ref.at[bb]
        return (pltpu.make_async_copy(ksrc, kbuf.at[slot, r], ksem.at[slot]),
                pltpu.make_async_copy(vsrc, vbuf.at[slot, r], vsem.at[slot]))

    def start_all(bb, slot):
        def body(r, c):
            old = idx_ref[bb, r] < past

            @pl.when(old)
            def _():
                for cp in row_copies(bb, slot, r, True):
                    cp.start()

            @pl.when(jnp.logical_not(old))
            def _():
                for cp in row_copies(bb, slot, r, False):
                    cp.start()

            return c

        lax.fori_loop(0, topk, body, 0)

    @pl.when(b == 0)
    def _():
        start_all(0, 0)

    @pl.when(b + 1 < n_batch)
    def _():
        start_all(b + 1, (b + 1) % 2)

    slot = b % 2

    def wait_body(r, c):
        for cp in row_copies(b, slot, r, True):
            cp.wait()
        return c

    lax.fori_loop(0, topk, wait_body, 0)

    q = q_ref[0]
    k = kbuf[slot]
    logits = jnp.sum(k * q[None], axis=-1, keepdims=True) * (HEAD_DIM ** -0.5)
    m = jnp.max(logits, axis=0, keepdims=True)
    p = jnp.exp(logits - m)
    den = jnp.sum(p, axis=0)
    o_ref[0] = jnp.sum(p * vbuf[slot], axis=0) / den


def sample_attend(idx, page_table, q, k_new, v_new, cache_k, cache_v):
    bsz, h, dh = q.shape
    topk = idx.shape[1]
    past = page_table.shape[1] * PAGE_SIZE
    grid_spec = pltpu.PrefetchScalarGridSpec(
        num_scalar_prefetch=2,
        grid=(bsz,),
        in_specs=[pl.BlockSpec((1, h, dh), lambda i, a, c: (i, 0, 0)),
                  pl.BlockSpec(memory_space=pl.ANY), pl.BlockSpec(memory_space=pl.ANY),
                  pl.BlockSpec(memory_space=pl.ANY), pl.BlockSpec(memory_space=pl.ANY)],
        out_specs=pl.BlockSpec((1, h, dh), lambda i, a, c: (i, 0, 0)),
        scratch_shapes=[pltpu.VMEM((2, topk, h, dh), F32), pltpu.VMEM((2, topk, h, dh), F32),
                        pltpu.SemaphoreType.DMA((2,)), pltpu.SemaphoreType.DMA((2,))],
    )
    return pl.pallas_call(
        functools.partial(_sample_attend_kernel, topk=topk, past=past, n_batch=bsz),
        grid_spec=grid_spec,
        out_shape=jax.ShapeDtypeStruct((bsz, h, dh), F32),
        compiler_params=_cparams(("arbitrary",)),
        name="sample_attend",
    )(idx, page_table, q, k_new, v_new, cache_k, cache_v)


def _gate_merge_kernel(s_ref, a_ref, ws_ref, wa_ref, gs_ref, ga_ref, o_ref):
    ms = jnp.dot(s_ref[...], ws_ref[...], preferred_element_type=F32)
    ma = jnp.dot(a_ref[...], wa_ref[...], preferred_element_type=F32)
    o_ref[...] = (gs_ref[...] * ms + ga_ref[...] * ma).astype(o_ref.dtype)


def gate_merge(ssm_out, att, w_bs, w_ba, gates, tm, tn):
    m, kdim = ssm_out.shape
    d = w_bs.shape[1]
    nj = d // tn
    return pl.pallas_call(
        _gate_merge_kernel,
        grid=(m // tm, nj),
        in_specs=[pl.BlockSpec((tm, kdim), lambda i, j: (i, 0)), pl.BlockSpec((tm, kdim), lambda i, j: (i, 0)),
                  pl.BlockSpec((kdim, tn), lambda i, j: (0, j)), pl.BlockSpec((kdim, tn), lambda i, j: (0, j)),
                  pl.BlockSpec((tm, tn), lambda i, j: (i, j)), pl.BlockSpec((tm, tn), lambda i, j: (i, j + nj))],
        out_specs=pl.BlockSpec((tm, tn), lambda i, j: (i, j)),
        out_shape=jax.ShapeDtypeStruct((m, d), BF16),
        compiler_params=_cparams(("parallel", "parallel")),
        name="gate_merge",
    )(ssm_out, att, w_bs, w_ba, gates, gates)


def _out_residual_kernel(m_ref, w_ref, x_ref, o_ref):
    o_ref[...] = x_ref[...] + jnp.dot(m_ref[...], w_ref[...], preferred_element_type=F32)


def out_residual(merged, w_out, x2d, tm, tn):
    m, d = x2d.shape
    return pl.pallas_call(
        _out_residual_kernel,
        grid=(m // tm, d // tn),
        in_specs=[pl.BlockSpec((tm, d), lambda i, j: (i, 0)), pl.BlockSpec((d, tn), lambda i, j: (0, j)),
                  pl.BlockSpec((tm, tn), lambda i, j: (i, j))],
        out_specs=pl.BlockSpec((tm, tn), lambda i, j: (i, j)),
        out_shape=jax.ShapeDtypeStruct((m, d), F32),
        compiler_params=_cparams(("parallel", "parallel")),
        name="out_residual",
    )(merged, w_out, x2d)


def _ffn_kernel(*refs, seq_mode, tm, n_ff_tiles):
    if seq_mode:
        (x_ref, gf_ref, gfin_ref, wa_ref, wb_ref, cwa_ref, cwb_ref, cba_ref, cbb_ref, wd_ref, pa_ref, pb_ref,
         y_ref, ca_ref, cb_ref, h_scr, acc_scr, car_a, car_b) = refs
        i = pl.program_id(1)
        j = pl.program_id(2)
    else:
        (x_ref, gf_ref, gfin_ref, wa_ref, wb_ref, cwa_ref, cwb_ref, cba_ref, cbb_ref, wd_ref,
         p2a_ref, p1a_ref, p2b_ref, p1b_ref, y_ref, ca_ref, cb_ref, h_scr, acc_scr) = refs
        j = pl.program_id(1)

    @pl.when(j == 0)
    def _():
        x = x_ref[0]
        ms = jnp.mean(x * x, axis=-1, keepdims=True)
        h_scr[...] = (x * lax.rsqrt(ms + RMS_EPS) * gf_ref[...]).astype(BF16)
        acc_scr[...] = jnp.zeros_like(acc_scr)

    h = h_scr[...]

    def conv_half(w_ref, cw_ref, cbias_ref, out_ref, prev):
        up = jnp.dot(h, w_ref[...], preferred_element_type=F32)
        cw = cw_ref[...]
        if seq_mode:
            prev_ref, car = prev

            @pl.when(i == 0)
            def _():
                car[j] = prev_ref[0]

            c2 = car[j]
            row = lax.broadcasted_iota(jnp.int32, up.shape, 0)
            r6 = jnp.broadcast_to(c2[0:1, :], up.shape)
            r7 = jnp.broadcast_to(c2[1:2, :], up.shape)
            p1 = jnp.where(row == 0, r7, pltpu.roll(up, 1, 0))
            p2 = jnp.where(row == 0, r6, jnp.where(row == 1, r7, pltpu.roll(up, 2, 0)))
            car[j] = up[tm - 2:, :]
            out_ref[0, 0] = up[tm - 2:, :]
        else:
            p2_ref, p1_ref = prev
            p2, p1 = p2_ref[...], p1_ref[...]
            out_ref[...] = up
        return cbias_ref[...] + cw[0:1, :] * p2 + cw[1:2, :] * p1 + cw[2:3, :] * up

    if seq_mode:
        a = conv_half(wa_ref, cwa_ref, cba_ref, ca_ref, (pa_ref, car_a))
        g = conv_half(wb_ref, cwb_ref, cbb_ref, cb_ref, (pb_ref, car_b))
    else:
        a = conv_half(wa_ref, cwa_ref, cba_ref, ca_ref, (p2a_ref, p1a_ref))
        g = conv_half(wb_ref, cwb_ref, cbb_ref, cb_ref, (p2b_ref, p1b_ref))
    act = (jax.nn.silu(a) * g).astype(BF16)
    acc_scr[...] += jnp.dot(act, wd_ref[...], preferred_element_type=F32)

    @pl.when(j == n_ff_tiles - 1)
    def _():
        x2 = x_ref[0] + acc_scr[...]
        ms = jnp.mean(x2 * x2, axis=-1, keepdims=True)
        y_ref[0] = x2 * lax.rsqrt(ms + RMS_EPS) * gfin_ref[...]


def conv_ffn_final(x, conv_prev, g_ffn, g_final, w_up, conv_w, conv_b, w_down, tm, tf, seq_mode):
    b, t, d = x.shape
    f = w_down.shape[0]
    nf = f // tf
    gf = g_ffn.reshape(1, d).astype(F32)
    gfin = g_final.reshape(1, d).astype(F32)
    cb2 = conv_b.reshape(1, 2 * f).astype(F32)
    conv_w = conv_w.astype(F32)
    kern = functools.partial(_ffn_kernel, seq_mode=seq_mode, tm=tm, n_ff_tiles=nf)
    if seq_mode:
        nt = t // tm
        wspec = lambda blk, off: pl.BlockSpec(blk, lambda bi, i, j: (0, j + off))
        in_specs = [pl.BlockSpec((1, tm, d), lambda bi, i, j: (bi, i, 0)),
                    pl.BlockSpec((1, d), lambda bi, i, j: (0, 0)), pl.BlockSpec((1, d), lambda bi, i, j: (0, 0)),
                    wspec((d, tf), 0), wspec((d, tf), nf), wspec((CONV_W, tf), 0), wspec((CONV_W, tf), nf),
                    wspec((1, tf), 0), wspec((1, tf), nf),
                    pl.BlockSpec((tf, d), lambda bi, i, j: (j, 0)),
                    pl.BlockSpec((1, CONV_W - 1, tf), lambda bi, i, j: (bi, 0, j)),
                    pl.BlockSpec((1, CONV_W - 1, tf), lambda bi, i, j: (bi, 0, j + nf))]
        out_specs = [pl.BlockSpec((1, tm, d), lambda bi, i, j: (bi, i, 0)),
                     pl.BlockSpec((1, 1, CONV_W - 1, tf), lambda bi, i, j: (bi, i, 0, j)),
                     pl.BlockSpec((1, 1, CONV_W - 1, tf), lambda bi, i, j: (bi, i, 0, j))]
        out_shape = [jax.ShapeDtypeStruct((b, t, d), F32), jax.ShapeDtypeStruct((b, nt, CONV_W - 1, f), F32),
                     jax.ShapeDtypeStruct((b, nt, CONV_W - 1, f), F32)]
        scratch = [pltpu.VMEM((tm, d), BF16), pltpu.VMEM((tm, d), F32),
                   pltpu.VMEM((nf, CONV_W - 1, tf), F32), pltpu.VMEM((nf, CONV_W - 1, tf), F32)]
        y, ca, cb = pl.pallas_call(
            kern, grid=(b, nt, nf), in_specs=in_specs, out_specs=out_specs, out_shape=out_shape,
            scratch_shapes=scratch, compiler_params=_cparams(("arbitrary", "arbitrary", "arbitrary")),
            name="conv_ffn_seq",
        )(x, gf, gfin, w_up, w_up, conv_w, conv_w, cb2, cb2, w_down, conv_prev, conv_prev)
        return y, jnp.concatenate([ca[:, -1], cb[:, -1]], axis=-1)
    p2, p1 = conv_prev[:, 0, :], conv_prev[:, 1, :]
    wspec = lambda blk, off: pl.BlockSpec(blk, lambda i, j: (0, j + off))
    in_specs = [pl.BlockSpec((1, tm, d), lambda i, j: (0, i, 0)),
                pl.BlockSpec((1, d), lambda i, j: (0, 0)), pl.BlockSpec((1, d), lambda i, j: (0, 0)),
                wspec((d, tf), 0), wspec((d, tf), nf), wspec((CONV_W, tf), 0), wspec((CONV_W, tf), nf),
                wspec((1, tf), 0), wspec((1, tf), nf),
                pl.BlockSpec((tf, d), lambda i, j: (j, 0)),
                pl.BlockSpec((tm, tf), lambda i, j: (i, j)), pl.BlockSpec((tm, tf), lambda i, j: (i, j)),
                pl.BlockSpec((tm, tf), lambda i, j: (i, j + nf)), pl.BlockSpec((tm, tf), lambda i, j: (i, j + nf))]
    out_specs = [pl.BlockSpec((1, tm, d), lambda i, j: (0, i, 0)),
                 pl.BlockSpec((tm, tf), lambda i, j: (i, j)), pl.BlockSpec((tm, tf), lambda i, j: (i, j))]
    out_shape = [jax.ShapeDtypeStruct((1, t, d), F32), jax.ShapeDtypeStruct((t, f), F32),
                 jax.ShapeDtypeStruct((t, f), F32)]
    scratch = [pltpu.VMEM((tm, d), BF16), pltpu.VMEM((tm, d), F32)]
    y, ua, ub = pl.pallas_call(
        kern, grid=(t // tm, nf), in_specs=in_specs, out_specs=out_specs, out_shape=out_shape,
        scratch_shapes=scratch, compiler_params=_cparams(("arbitrary", "arbitrary")),
        name="conv_ffn_step",
    )(x, gf, gfin, w_up, w_up, conv_w, conv_w, cb2, cb2, w_down, p2, p1, p2, p1)
    return y, jnp.stack([p1, jnp.concatenate([ua, ub], axis=-1)], axis=1)


def _split_w_in(w_in, d_model, n_ssm, n_att):
    sizes = [n_ssm, n_att, n_att, n_att, H_IDX * D_IDX, D_IDX, H_IDX, d_model, d_model]
    offs = np.concatenate([[0], np.cumsum(sizes)]).tolist()
    col = lambda i: w_in[:, offs[i]:offs[i + 1]]
    w_uv = jnp.concatenate([col(0), col(3)], axis=1).astype(BF16)
    w_qk = jnp.concatenate([col(1), col(2)], axis=1).astype(BF16)
    w_qi = col(4).astype(BF16)
    pad = jnp.zeros((w_in.shape[0], LANES - D_IDX - H_IDX), w_in.dtype)
    w_kw = jnp.concatenate([col(5), col(6), pad], axis=1).astype(BF16)
    w_g = jnp.concatenate([col(7), col(8)], axis=1).astype(BF16)
    return w_uv, w_qk, w_qi, w_kw, w_g


def _mix_inputs(x2d, pos, n_pos_tiles, tm, norm_mix, w_groups, n_ssm, n_att, sequence):
    w_uv, w_qk, w_qi, w_kw, w_g = w_groups
    h = rmsnorm_bf16(x2d, norm_mix, tm)
    tn = 512
    nu, na = n_ssm // tn, n_att // tn
    half_qk = HEAD_DIM // ROPE_FRACTION // 2
    half_idx = D_IDX // ROPE_FRACTION // 2
    out = {}
    if sequence:
        out["u"], out["v"], out["v16"] = project(
            h, w_uv, tm, tn, [(0, nu, F32, "flat"), (nu, na, F32, "flat"), (nu, na, BF16, "flat")])
        out["q16"], out["k"], out["k16"] = project(
            h, w_qk, tm, tn, [(0, na, BF16, "flat"), (na, na, F32, "flat"), (na, na, BF16, "flat")],
            "rope", half_qk, rope_tables(pos, HEAD_DIM, tn), n_pos_tiles)
        out["qi16"], = project(h, w_qi, tm, tn, [(0, H_IDX * D_IDX // tn, BF16, "heads")],
                               "rope", half_idx, rope_tables(pos, D_IDX, tn), n_pos_tiles)
    else:
        out["u"], out["v"] = project(h, w_uv, tm, tn, [(0, nu, F32, "flat"), (nu, na, F32, "flat")])
        out["q"], out["k"] = project(h, w_qk, tm, tn, [(0, na, F32, "flat"), (na, na, F32, "flat")],
                                     "rope", half_qk, rope_tables(pos, HEAD_DIM, tn), n_pos_tiles)
        out["qi"], = project(h, w_qi, tm, tn, [(0, H_IDX * D_IDX // tn, F32, "flat")],
                             "rope", half_idx, rope_tables(pos, D_IDX, tn), n_pos_tiles)
    extra = jnp.concatenate([jnp.full((H_IDX,), H_IDX ** -0.5, F32), jnp.zeros((LANES - D_IDX - H_IDX,), F32)])
    kw, = project(h, w_kw, tm, LANES, [(0, 1, F32, "flat")], "rope", half_idx,
                  rope_tables(pos, D_IDX, LANES, extra), n_pos_tiles)
    out["ki"], out["wi"] = kw[:, :D_IDX], kw[:, D_IDX:D_IDX + H_IDX]
    out["gates"], = project(h, w_g, tm, tn, [(0, w_g.shape[1] // tn, F32, "flat")], "sigmoid")
    return out


def kernel(x_prompt, x_sample, cache_k, cache_v, cache_idx_k, page_table, state_ssm_re, state_ssm_im,
           state_ffn_conv, norm_mix, w_in, ssm_A_re, ssm_A_im, ssm_log_dt, ssm_B_re, ssm_B_im, ssm_C_re,
           ssm_C_im, ssm_D, w_glu, w_branch_ssm, w_branch_att, w_out, norm_ffn, w_up, ffn_conv_w,
           ffn_conv_b, w_down, norm_final):
    bp, t, d = x_prompt.shape
    bs, ts, _ = x_sample.shape
    n_ssm = ssm_D.shape[0]
    n_att = w_branch_att.shape[0]
    g = n_ssm // SSM_GROUP
    h_att = n_att // HEAD_DIM
    past = page_table.shape[1] * PAGE_SIZE
    assert ts == 1, "sample group is a single decode step"
    assert bp == SUBLANES, "the S5 sequence kernel keeps one batch row per sublane"

    w_groups = _split_w_in(w_in, d, n_ssm, n_att)
    w_glu_b, w_bs, w_ba, w_out_b = (w.astype(BF16) for w in (w_glu, w_branch_ssm, w_branch_att, w_out))
    w_up_b, w_down_b = w_up.astype(BF16), w_down.astype(BF16)
    ssm_params = (ssm_A_re, ssm_A_im, ssm_log_dt, ssm_B_re, ssm_B_im, ssm_C_re, ssm_C_im)

    mp = bp * t
    tm = 1024
    xp2 = x_prompt.reshape(mp, d)
    pos_p = jnp.arange(t, dtype=jnp.int32)
    pr = _mix_inputs(xp2, pos_p, t // tm, tm, norm_mix, w_groups, n_ssm, n_att, True)
    zeros_s = jnp.zeros((bp, g, SSM_STATE), F32)
    y_act, s_re_p, s_im_p = s5_sequence(pr["u"].reshape(bp, t, n_ssm), zeros_s, zeros_s, ssm_D,
                                        ssm_tile_matrices(*ssm_params, SSM_CHUNK), SSM_CHUNK, 512)
    ssm_out = ssm_glu(y_act.reshape(mp, n_ssm), w_glu_b, 512)
    att = prompt_attention(pr["qi16"], pr["ki"].astype(BF16).reshape(bp, t, D_IDX), pr["wi"].reshape(bp, t, H_IDX),
                           pr["q16"].reshape(bp, t, n_att), pr["k16"].reshape(bp, t, n_att),
                           pr["v16"].reshape(bp, t, n_att), 256, 4)
    merged = gate_merge(ssm_out, att.reshape(mp, n_att), w_bs, w_ba, pr["gates"], 512, 512)
    x1 = out_residual(merged, w_out_b, xp2, 512, 512)
    zeros_c = jnp.zeros((bp, CONV_W - 1, w_up.shape[1]), F32)
    y_prompt, conv_p = conv_ffn_final(x1.reshape(bp, t, d), zeros_c, norm_ffn, norm_final, w_up_b, ffn_conv_w,
                                      ffn_conv_b, w_down_b, 512, 512, True)
    k_p = pr["k"].reshape(bp, t, h_att, HEAD_DIM)
    v_p = pr["v"].reshape(bp, t, h_att, HEAD_DIM)
    ki_p = pr["ki"].reshape(bp, t, D_IDX)

    xs2 = x_sample.reshape(bs, d)
    pos_s = jnp.full((bs,), past, jnp.int32)
    sm = _mix_inputs(xs2, pos_s, 1, bs, norm_mix, w_groups, n_ssm, n_att, False)
    y_act, s_re_s, s_im_s = s5_step(sm["u"], state_ssm_re, state_ssm_im, ssm_D, ssm_tile_matrices(*ssm_params, 1))
    ssm_out = ssm_glu(y_act, w_glu_b, bs)
    scores = sample_scores(page_table, sm["qi"].astype(BF16).reshape(bs, H_IDX, D_IDX), sm["wi"].reshape(bs, H_IDX, 1),
                           sm["ki"].reshape(bs, 1, D_IDX), cache_idx_k)
    topk = min(TOPK_MAX, (past + 1) // 4)
    hl = sample_select(scores.reshape(bs, past + LANES), topk)
    idx = (hl[:, :, 0] * PAGE_SIZE + hl[:, :, 1]).astype(jnp.int32)
    att = sample_attend(idx, page_table, sm["q"].reshape(bs, h_att, HEAD_DIM), sm["k"].reshape(bs, h_att, HEAD_DIM),
                        sm["v"].reshape(bs, h_att, HEAD_DIM), cache_k, cache_v)
    merged = gate_merge(ssm_out, att.reshape(bs, n_att).astype(BF16), w_bs, w_ba, sm["gates"], bs, 512)
    x1 = out_residual(merged, w_out_b, xs2, bs, 512)
    y_s, conv_s = conv_ffn_final(x1.reshape(1, bs, d), state_ffn_conv, norm_ffn, norm_final, w_up_b, ffn_conv_w,
                                 ffn_conv_b, w_down_b, bs, 512, False)
    y_sample = y_s.reshape(bs, 1, d)
    k_s = sm["k"].reshape(bs, 1, h_att, HEAD_DIM)
    v_s = sm["v"].reshape(bs, 1, h_att, HEAD_DIM)
    ki_s = sm["ki"].reshape(bs, 1, D_IDX)

    return (y_prompt, y_sample, k_p, v_p, ki_p, k_s, v_s, ki_s, s_re_p, s_im_p, s_re_s, s_im_s, conv_p, conv_s)
```

```python
import functools

import numpy as np
import jax
import jax.numpy as jnp
from jax import lax
from jax.experimental import pallas as pl
from jax.experimental.pallas import tpu as pltpu

F32 = jnp.float32
BF16 = jnp.bfloat16

HEAD_DIM = 128
SSM_GROUP = 16
SSM_STATE = 64
H_IDX = 16
D_IDX = 64
TOPK_MAX = 256
ROPE_THETA = 500000.0
ROPE_FRACTION = 4
CONV_W = 3
RMS_EPS = 1e-6
NEG_INF = -1e30
PAGE_SIZE = 128

LANES = 128
SUBLANES = 8
MXU_WIDTH = 256
GROUPS_PER_TILE = LANES // SSM_GROUP
SSM_CHUNK = 8
PROJ_SUB_ROWS = 256
FFN_SUB_ROWS = 256
VMEM_LIMIT = 56 * 1024 * 1024
INT_MIN = -2 ** 31
NT_DIMS = (((1,), (1,)), ((), ()))


def _cparams(sem):
    return pltpu.CompilerParams(dimension_semantics=sem, vmem_limit_bytes=VMEM_LIMIT)


def _rmsnorm_kernel(x_ref, g_ref, o_ref):
    x = x_ref[...]
    ms = jnp.mean(x * x, axis=-1, keepdims=True)
    o_ref[...] = (x * lax.rsqrt(ms + RMS_EPS) * g_ref[...]).astype(o_ref.dtype)


def rmsnorm_bf16(x2d, g, tm):
    m, d = x2d.shape
    return pl.pallas_call(
        _rmsnorm_kernel,
        grid=(m // tm,),
        in_specs=[pl.BlockSpec((tm, d), lambda i: (i, 0)), pl.BlockSpec((1, d), lambda i: (0, 0))],
        out_specs=pl.BlockSpec((tm, d), lambda i: (i, 0)),
        out_shape=jax.ShapeDtypeStruct((m, d), BF16),
        compiler_params=_cparams(("parallel",)),
        name="rmsnorm",
    )(x2d, g.reshape(1, d))


def _proj_kernel(h_ref, w_ref, *rest, mode, shift, outs, n_tab, n_col_tiles):
    tabs, o_refs = rest[:n_tab], rest[n_tab:]
    j = pl.program_id(1)
    tm, tn = h_ref.shape[0], w_ref.shape[1]
    sub = min(tm, PROJ_SUB_ROWS)
    for r0 in range(0, tm, sub):
        rows = slice(r0, r0 + sub)
        z = jnp.dot(h_ref[rows, :], w_ref[...], preferred_element_type=F32)
        if mode == "rope":
            c_ref, s1_ref, s2_ref = tabs
            z = (z * c_ref[rows, :] + pltpu.roll(z, tn - shift, 1) * s1_ref[rows, :]
                 + pltpu.roll(z, shift, 1) * s2_ref[rows, :])
        for (start, cnt, _, kind), o_ref in zip(outs, o_refs):
            if kind == "heads":
                assert start + cnt == n_col_tiles
                for hh in range(tn // D_IDX):
                    o_ref[0, hh, rows, :] = z[:, hh * D_IDX:(hh + 1) * D_IDX].astype(o_ref.dtype)
            elif start + cnt == n_col_tiles:
                o_ref[rows, :] = z.astype(o_ref.dtype)
            else:
                o_ref[rows, :] = jnp.where(j >= start + cnt, o_ref[rows, :], z.astype(o_ref.dtype))


def project(h, w, tm, tn, outs, mode="plain", shift=0, tables=None, n_pos_tiles=1):
    m, k = h.shape
    n = w.shape[1]
    in_specs = [pl.BlockSpec((tm, k), lambda i, j: (i, 0)), pl.BlockSpec((k, tn), lambda i, j: (0, j))]
    args = [h, w]
    tables = tables or ()
    for t in tables:
        in_specs.append(pl.BlockSpec((tm, tn), lambda i, j: (i % n_pos_tiles, 0)))
        args.append(t)
    out_specs, out_shape = [], []
    for start, cnt, dtype, kind in outs:
        if kind == "heads":
            hpt = tn // D_IDX
            out_specs.append(pl.BlockSpec(
                (1, hpt, tm, D_IDX),
                lambda i, j, s=start, c=cnt: (i // n_pos_tiles, jnp.clip(j - s, 0, c - 1), i % n_pos_tiles, 0)))
            out_shape.append(jax.ShapeDtypeStruct((m // (n_pos_tiles * tm), cnt * hpt, n_pos_tiles * tm, D_IDX), dtype))
        else:
            out_specs.append(pl.BlockSpec((tm, tn), lambda i, j, s=start, c=cnt: (i, jnp.clip(j - s, 0, c - 1))))
            out_shape.append(jax.ShapeDtypeStruct((m, cnt * tn), dtype))
    return pl.pallas_call(
        functools.partial(_proj_kernel, mode=mode, shift=shift, outs=tuple(outs), n_tab=len(tables),
                          n_col_tiles=n // tn),
        grid=(m // tm, n // tn),
        in_specs=in_specs,
        out_specs=out_specs,
        out_shape=out_shape,
        compiler_params=_cparams(("parallel", "arbitrary")),
        name="proj_" + mode,
    )(*args)


def rope_tables(pos, head, tn, extra=None):
    r = head // ROPE_FRACTION
    half = r // 2
    inv = ROPE_THETA ** (-jnp.arange(half, dtype=F32) * 2.0 / r)
    ang = pos.astype(F32)[:, None] * inv[None, :]
    cos, sin = jnp.cos(ang), jnp.sin(ang)
    t = pos.shape[0]
    zeros = jnp.zeros((t, head - r), F32)
    zh = jnp.zeros((t, half), F32)
    c = jnp.concatenate([cos, cos, jnp.ones((t, head - r), F32)], axis=1)
    s1 = jnp.concatenate([-sin, zh, zeros], axis=1)
    s2 = jnp.concatenate([zh, sin, zeros], axis=1)
    if extra is None:
        reps = tn // head
        return tuple(jnp.tile(a, (1, reps)) for a in (c, s1, s2))
    pad = jnp.zeros((t, tn - head), F32)
    return (jnp.concatenate([c, jnp.broadcast_to(extra[None, :], (t, tn - head))], axis=1),
            jnp.concatenate([s1, pad], axis=1), jnp.concatenate([s2, pad], axis=1))


def ssm_tile_matrices(a_re, a_im, log_dt, b_re, b_im, c_re, c_im, chunk, dtype):
    hp = lax.Precision.HIGHEST
    g, p = a_re.shape
    gt = GROUPS_PER_TILE
    nq = g // gt
    a = lax.complex(a_re.astype(F32), a_im.astype(F32))
    dt = jnp.exp(log_dt.astype(F32))[:, None]
    adt = a * dt
    a_bar = jnp.exp(adt)
    b_bar = ((a_bar - 1.0) / a)[..., None] * lax.complex(b_re.astype(F32), b_im.astype(F32))
    cc = lax.complex(c_re.astype(F32), c_im.astype(F32))
    steps = jnp.arange(chunk + 1, dtype=F32)
    pw = jnp.exp(adt[:, None, :] * steps[None, :, None].astype(jnp.complex64))
    kd = jnp.real(jnp.einsum("gcp,gdp,gpe->gdce", cc, pw[:, :chunk], b_bar, precision=hp))
    w1g = pw[:, chunk - 1 - jnp.arange(chunk)][:, :, None, :] * b_bar.transpose(0, 2, 1)[:, None]
    m2g = cc.transpose(0, 2, 1)[:, :, None, :] * pw[:, 1:chunk + 1].transpose(0, 2, 1)[:, :, :, None]
    kdc = kd.reshape(nq, gt, chunk, SSM_GROUP, SSM_GROUP).transpose(0, 2, 4, 1, 3).reshape(nq, chunk, SSM_GROUP, LANES)
    w1c = w1g.reshape(nq, gt, chunk, SSM_GROUP, p).transpose(0, 2, 3, 1, 4).reshape(nq, chunk, SSM_GROUP, gt * p)
    m2c = m2g.reshape(nq, gt, p, chunk, SSM_GROUP).transpose(0, 3, 2, 1, 4).reshape(nq, chunk, p, LANES)
    al = pw[:, chunk].reshape(nq, 1, gt * p)
    kt, w1r, w1i, w2r, w2i = _ssm_expand(kdc, jnp.real(w1c), jnp.imag(w1c), jnp.real(m2c), -jnp.imag(m2c), chunk, dtype)
    return dict(w1r=w1r, w1i=w1i, kt=kt, w2r=w2r, w2i=w2i, ar=jnp.real(al), ai=jnp.imag(al))


def _ssm_expand_kernel(kd_ref, w1r_ref, w1i_ref, w2r_ref, w2i_ref, kt_o, w1r_o, w1i_o, w2r_o, w2i_o, *, chunk):
    gt, p = GROUPS_PER_TILE, SSM_STATE

    def same_group(rows_per_group, cols_per_group):
        shape = (gt * rows_per_group, gt * cols_per_group)
        r = jnp.right_shift(lax.broadcasted_iota(jnp.int32, shape, 0), rows_per_group.bit_length() - 1)
        c = jnp.right_shift(lax.broadcasted_iota(jnp.int32, shape, 1), cols_per_group.bit_length() - 1)
        return r == c

    def block_diag(small, mask, dtype):
        return jnp.where(mask, jnp.tile(small, (gt, 1)), 0.0).astype(dtype)

    m_cc, m_cp, m_pc = same_group(SSM_GROUP, SSM_GROUP), same_group(SSM_GROUP, p), same_group(p, SSM_GROUP)
    for j in range(chunk):
        rows = slice(j * LANES, (j + 1) * LANES)
        w1r_o[0, rows, :] = block_diag(w1r_ref[0, j], m_cp, w1r_o.dtype)
        w1i_o[0, rows, :] = block_diag(w1i_ref[0, j], m_cp, w1i_o.dtype)
        w2r_o[0, :, rows] = block_diag(w2r_ref[0, j], m_pc, w2r_o.dtype)
        w2i_o[0, :, rows] = block_diag(w2i_ref[0, j], m_pc, w2i_o.dtype)
        for t in range(chunk):
            cols = slice(t * LANES, (t + 1) * LANES)
            if t >= j:
                kt_o[0, rows, cols] = block_diag(kd_ref[0, t - j], m_cc, kt_o.dtype)
            else:
                kt_o[0, rows, cols] = jnp.zeros((LANES, LANES), kt_o.dtype)


def _ssm_expand(kdc, w1r, w1i, w2r, w2i, chunk, dtype):
    nq = kdc.shape[0]
    ns = w1r.shape[-1]
    cw = chunk * LANES
    spec4 = lambda a: pl.BlockSpec((1,) + a.shape[1:], lambda q: (q, 0, 0, 0))
    spec3 = lambda r, c: pl.BlockSpec((1, r, c), lambda q: (q, 0, 0))
    shapes = [(cw, cw), (cw, ns), (cw, ns), (ns, cw), (ns, cw)]
    return pl.pallas_call(
        functools.partial(_ssm_expand_kernel, chunk=chunk),
        grid=(nq,),
        in_specs=[spec4(a) for a in (kdc, w1r, w1i, w2r, w2i)],
        out_specs=[spec3(r, c) for r, c in shapes],
        out_shape=[jax.ShapeDtypeStruct((nq, r, c), dtype) for r, c in shapes],
        compiler_params=_cparams(("parallel",)),
        name="ssm_expand",
    )(kdc, w1r, w1i, w2r, w2i)


def _s5_seq_kernel(u_ref, w1r_ref, w1i_ref, kt_ref, w2r_ref, w2i_ref, ar_ref, ai_ref, d_ref, s0r_ref, s0i_ref,
                   o_ref, sfr_ref, sfi_ref, ucat, slr, sli, spr, spi, yc, yb, st_r, st_i, *, chunk, n_rows, nb):
    ts = pl.program_id(1)
    ns = st_r.shape[-1]

    @pl.when(ts == 0)
    def _():
        st_r[...] = s0r_ref[...]
        st_i[...] = s0i_ref[...]

    for b in range(nb):
        for tl in range(chunk):
            ucat[:, b, tl * LANES:(tl + 1) * LANES] = u_ref[b, pl.ds(tl, n_rows, stride=chunk), :]
    x = ucat[...].reshape(n_rows * nb, chunk * LANES).astype(BF16)
    slr[...] = jnp.dot(x, w1r_ref[0], preferred_element_type=F32).reshape(n_rows, nb, ns)
    sli[...] = jnp.dot(x, w1i_ref[0], preferred_element_type=F32).reshape(n_rows, nb, ns)
    ar = jnp.broadcast_to(ar_ref[0], (nb, ns))
    ai = jnp.broadcast_to(ai_ref[0], (nb, ns))

    def step(n, carry):
        sr, si = carry
        spr[n] = sr
        spi[n] = si
        return ar * sr - ai * si + slr[n], ar * si + ai * sr + sli[n]

    sr, si = lax.fori_loop(0, n_rows, step, (st_r[...], st_i[...]))
    st_r[...] = sr
    st_i[...] = si
    sfr_ref[...] = sr
    sfi_ref[...] = si
    y = (jnp.dot(x, kt_ref[0], preferred_element_type=F32)
         + jnp.dot(spr[...].reshape(n_rows * nb, ns).astype(BF16), w2r_ref[0], preferred_element_type=F32)
         + jnp.dot(spi[...].reshape(n_rows * nb, ns).astype(BF16), w2i_ref[0], preferred_element_type=F32))
    yc[...] = y.reshape(n_rows, nb, chunk * LANES)
    for b in range(nb):
        for tl in range(chunk):
            yb[b, pl.ds(tl, n_rows, stride=chunk), :] = yc[:, b, tl * LANES:(tl + 1) * LANES]
    o_ref[...] = jax.nn.gelu(yb[...] + d_ref[...] * u_ref[...]).astype(o_ref.dtype)


def s5_sequence(u, s0_re, s0_im, d_skip, mats, chunk, t_seg):
    b, t, n = u.shape
    g, p = s0_re.shape[1:]
    nq = n // LANES
    ns = GROUPS_PER_TILE * p
    cw = chunk * LANES
    n_rows = t_seg // chunk
    wspec = lambda r, c: pl.BlockSpec((1, r, c), lambda q, s: (q, 0, 0))
    state_spec = pl.BlockSpec((b, ns), lambda q, s: (0, q))
    y, sr, si = pl.pallas_call(
        functools.partial(_s5_seq_kernel, chunk=chunk, n_rows=n_rows, nb=b),
        grid=(nq, t // t_seg),
        in_specs=[pl.BlockSpec((b, t_seg, LANES), lambda q, s: (0, s, q)),
                  wspec(cw, ns), wspec(cw, ns), wspec(cw, cw), wspec(ns, cw), wspec(ns, cw),
                  wspec(1, ns), wspec(1, ns), pl.BlockSpec((1, LANES), lambda q, s: (0, q)), state_spec, state_spec],
        out_specs=[pl.BlockSpec((b, t_seg, LANES), lambda q, s: (0, s, q)), state_spec, state_spec],
        out_shape=[jax.ShapeDtypeStruct((b, t, n), BF16), jax.ShapeDtypeStruct((b, g * p), F32),
                   jax.ShapeDtypeStruct((b, g * p), F32)],
        scratch_shapes=[pltpu.VMEM((n_rows, b, cw), F32),
                        pltpu.VMEM((n_rows, b, ns), F32), pltpu.VMEM((n_rows, b, ns), F32),
                        pltpu.VMEM((n_rows, b, ns), F32), pltpu.VMEM((n_rows, b, ns), F32),
                        pltpu.VMEM((n_rows, b, cw), F32), pltpu.VMEM((b, t_seg, LANES), F32),
                        pltpu.VMEM((b, ns), F32), pltpu.VMEM((b, ns), F32)],
        compiler_params=_cparams(("parallel", "arbitrary")),
        name="s5_sequence",
    )(u, mats["w1r"], mats["w1i"], mats["kt"], mats["w2r"], mats["w2i"], mats["ar"], mats["ai"],
      d_skip.reshape(1, n).astype(F32),
      s0_re.astype(F32).reshape(b, g * p), s0_im.astype(F32).reshape(b, g * p))
    return y, sr.reshape(b, g, p), si.reshape(b, g, p)


def _s5_step_kernel(u_ref, w1r_ref, w1i_ref, kt_ref, w2r_ref, w2i_ref, ar_ref, ai_ref, d_ref, s0r_ref, s0i_ref,
                    o_ref, sfr_ref, sfi_ref):
    dot = functools.partial(jnp.dot, preferred_element_type=F32, precision=lax.Precision.HIGHEST)
    x = u_ref[...]
    sr, si = s0r_ref[...], s0i_ref[...]
    ar, ai = ar_ref[0], ai_ref[0]
    sfr_ref[...] = ar * sr - ai * si + dot(x, w1r_ref[0])
    sfi_ref[...] = ar * si + ai * sr + dot(x, w1i_ref[0])
    y = dot(x, kt_ref[0]) + dot(sr, w2r_ref[0]) + dot(si, w2i_ref[0])
    o_ref[...] = jax.nn.gelu(y + d_ref[...] * x).astype(o_ref.dtype)


def s5_step(u, s0_re, s0_im, d_skip, mats):
    b, n = u.shape
    g, p = s0_re.shape[1:]
    nq = n // LANES
    ns = GROUPS_PER_TILE * p
    wspec = lambda r, c: pl.BlockSpec((1, r, c), lambda q: (q, 0, 0))
    state_spec = pl.BlockSpec((b, ns), lambda q: (0, q))
    lane_spec = pl.BlockSpec((b, LANES), lambda q: (0, q))
    y, sr, si = pl.pallas_call(
        _s5_step_kernel,
        grid=(nq,),
        in_specs=[lane_spec, wspec(LANES, ns), wspec(LANES, ns), wspec(LANES, LANES), wspec(ns, LANES),
                  wspec(ns, LANES), wspec(1, ns), wspec(1, ns), pl.BlockSpec((1, LANES), lambda q: (0, q)),
                  state_spec, state_spec],
        out_specs=[lane_spec, state_spec, state_spec],
        out_shape=[jax.ShapeDtypeStruct((b, n), BF16), jax.ShapeDtypeStruct((b, g * p), F32),
                   jax.ShapeDtypeStruct((b, g * p), F32)],
        compiler_params=_cparams(("parallel",)),
        name="s5_step",
    )(u, mats["w1r"], mats["w1i"], mats["kt"], mats["w2r"], mats["w2i"], mats["ar"], mats["ai"],
      d_skip.reshape(1, n).astype(F32), s0_re.astype(F32).reshape(b, g * p), s0_im.astype(F32).reshape(b, g * p))
    return y, sr.reshape(b, g, p), si.reshape(b, g, p)


def _glu_kernel(y_ref, w_ref, o_ref):
    n = o_ref.shape[-1]
    z = jnp.dot(y_ref[...], w_ref[...], preferred_element_type=F32)
    o_ref[...] = (z[:, :n] * jax.nn.sigmoid(z[:, n:])).astype(o_ref.dtype)


def ssm_glu(y, w_glu, tm):
    m, n = y.shape
    return pl.pallas_call(
        _glu_kernel,
        grid=(m // tm,),
        in_specs=[pl.BlockSpec((tm, n), lambda i: (i, 0)), pl.BlockSpec((n, 2 * n), lambda i: (0, 0))],
        out_specs=pl.BlockSpec((tm, n), lambda i: (i, 0)),
        out_shape=jax.ShapeDtypeStruct((m, n), BF16),
        compiler_params=_cparams(("parallel",)),
        name="ssm_glu",
    )(y, w_glu)


def _sortable_key(score):
    bits = pltpu.bitcast(score, jnp.int32)
    return jnp.where(bits < 0, bits ^ jnp.int32(0x7FFFFFFF), bits)


def _kth_largest_key(key, k):
    rows = key.shape[0]

    def body(i, res):
        cand = res + jnp.left_shift(jnp.int32(1), 31 - i)
        cnt = jnp.sum((key >= cand).astype(jnp.int32), axis=-1, keepdims=True)
        return jnp.where(cnt >= k, cand, res)

    return lax.fori_loop(0, 32, body, jnp.full((rows, 1), INT_MIN, jnp.int32))


def _exclusive_cumsum_lanes(flags_bf16):
    rows, n = flags_bf16.shape
    r = lax.broadcasted_iota(jnp.int32, (LANES, LANES), 0)
    c = lax.broadcasted_iota(jnp.int32, (LANES, LANES), 1)
    tri = jnp.where(r < c, 1.0, 0.0).astype(BF16)
    off = jnp.zeros((rows, 1), F32)
    out = []
    for i in range(n // LANES):
        chunk = flags_bf16[:, i * LANES:(i + 1) * LANES]
        out.append(jnp.dot(chunk, tri, preferred_element_type=F32) + off)
        off = off + jnp.sum(chunk.astype(F32), axis=-1, keepdims=True)
    return jnp.concatenate(out, axis=-1)


def _attend_keys(qi_ref, ki_ref, q_ref, k_ref, v_ref, o_ref, score_scr, bias_scr, wb_scr, *, qb, tq, lk, topk):
    n_chunks = lk // MXU_WIDTH
    qi_all = qi_ref[0].reshape(H_IDX * tq, D_IDX)

    def chunk_body(c, carry):
        k0 = pl.multiple_of(c * MXU_WIDTH, MXU_WIDTH)
        s = lax.dot_general(qi_all, ki_ref[0, pl.ds(k0, MXU_WIDTH), :], NT_DIMS, preferred_element_type=F32)
        acc = jnp.zeros((tq, MXU_WIDTH), F32)
        for h in range(H_IDX):
            wb = wb_scr[h]
            acc = acc + jnp.maximum(s[h * tq:(h + 1) * tq], 0.0) * jnp.concatenate([wb, wb], axis=-1)
        score_scr[c] = acc
        return carry

    lax.fori_loop(0, n_chunks, chunk_body, 0)
    score = jnp.concatenate([score_scr[c] for c in range(n_chunks)], axis=-1)
    row = qb * tq + lax.broadcasted_iota(jnp.int32, (tq, lk), 0)
    col = lax.broadcasted_iota(jnp.int32, (tq, lk), 1)
    causal = col <= row
    key = _sortable_key(jnp.where(causal, score, -jnp.inf))
    thr = _kth_largest_key(key, topk)
    gt = key > thr
    eq = jnp.logical_and(key == thr, causal)
    need = topk - jnp.sum(gt.astype(jnp.int32), axis=-1, keepdims=True)
    n_eq = jnp.sum(eq.astype(jnp.int32), axis=-1, keepdims=True)
    bias_scr[:, :lk] = jnp.where(jnp.logical_and(causal, key >= thr), 0.0, NEG_INF)

    @pl.when(jnp.max(n_eq - need) > 0)
    def _():
        rank = _exclusive_cumsum_lanes(jnp.where(eq, 1.0, 0.0).astype(BF16))
        keep = jnp.logical_or(gt, jnp.logical_and(eq, rank < need.astype(F32)))
        bias_scr[:, :lk] = jnp.where(jnp.logical_and(causal, keep), 0.0, NEG_INF)

    bias = bias_scr[:, :lk]
    scale = HEAD_DIM ** -0.5
    for h in range(q_ref.shape[-1] // HEAD_DIM):
        sl = slice(h * HEAD_DIM, (h + 1) * HEAD_DIM)
        logits = lax.dot_general(q_ref[0, :, sl], k_ref[0, :lk, sl], NT_DIMS, preferred_element_type=F32) * scale + bias
        m = jnp.max(logits, axis=-1, keepdims=True)
        p = jnp.exp(logits - m)
        den = jnp.sum(p, axis=-1, keepdims=True)
        o = jnp.dot(p.astype(BF16), v_ref[0, :lk, sl], preferred_element_type=F32) / den
        o_ref[0, :, sl] = o.astype(o_ref.dtype)


def _prompt_attn_kernel(qi_ref, ki_ref, w_ref, q_ref, k_ref, v_ref, o_ref, score_scr, bias_scr, wb_scr,
                        *, tq, seq, topk, n_buckets):
    qb = pl.program_id(1)
    w = w_ref[0] * (D_IDX ** -0.5)
    for h in range(H_IDX):
        wb_scr[h] = jnp.broadcast_to(w[:, h:h + 1], (tq, LANES))
    per = (seq // tq) // n_buckets
    for bkt in range(n_buckets):
        body = functools.partial(_attend_keys, qi_ref, ki_ref, q_ref, k_ref, v_ref, o_ref, score_scr, bias_scr, wb_scr,
                                 qb=qb, tq=tq, lk=(bkt + 1) * per * tq, topk=topk)
        pl.when(qb // per == bkt)(body)


def prompt_attention(qi, ki, wi, q, k, v, tq, n_buckets):
    b, t, n = q.shape
    topk = min(TOPK_MAX, t // 4)
    return pl.pallas_call(
        functools.partial(_prompt_attn_kernel, tq=tq, seq=t, topk=topk, n_buckets=n_buckets),
        grid=(b, t // tq),
        in_specs=[pl.BlockSpec((1, H_IDX, tq, D_IDX), lambda i, j: (i, 0, j, 0)),
                  pl.BlockSpec((1, t, D_IDX), lambda i, j: (i, 0, 0)),
                  pl.BlockSpec((1, tq, H_IDX), lambda i, j: (i, j, 0)),
                  pl.BlockSpec((1, tq, n), lambda i, j: (i, j, 0)),
                  pl.BlockSpec((1, t, n), lambda i, j: (i, 0, 0)),
                  pl.BlockSpec((1, t, n), lambda i, j: (i, 0, 0))],
        out_specs=pl.BlockSpec((1, tq, n), lambda i, j: (i, j, 0)),
        out_shape=jax.ShapeDtypeStruct((b, t, n), BF16),
        scratch_shapes=[pltpu.VMEM((t // MXU_WIDTH, tq, MXU_WIDTH), F32), pltpu.VMEM((tq, t), F32),
                        pltpu.VMEM((H_IDX, tq, LANES), F32)],
        compiler_params=_cparams(("parallel", "parallel")),
        name="prompt_attn",
    )(qi, ki, wi, q, k, v)


def _sample_score_kernel(pt_ref, qi_ref, w_ref, kin_ref, cache_ref, o_ref, kbuf, sem, *, n_pages, n_batch):
    b = pl.program_id(0)
    past = n_pages * PAGE_SIZE

    def page_copy(bb, slot, pg):
        return pltpu.make_async_copy(cache_ref.at[pt_ref[bb, pg]],
                                     kbuf.at[slot, :, pl.ds(pl.multiple_of(pg * PAGE_SIZE, PAGE_SIZE), PAGE_SIZE)],
                                     sem.at[slot])

    def start_all(bb, slot):
        lax.fori_loop(0, n_pages, lambda pg, c: (page_copy(bb, slot, pg).start(), c)[1], 0)

    @pl.when(b == 0)
    def _():
        start_all(0, 0)

    @pl.when(b + 1 < n_batch)
    def _():
        start_all(b + 1, (b + 1) % 2)

    slot = b % 2
    lax.fori_loop(0, n_pages, lambda pg, c: (page_copy(b, slot, pg).wait(), c)[1], 0)

    qi = qi_ref[0]
    w = w_ref[0] * (D_IDX ** -0.5)
    step = 2048
    for c in range(past // step):
        kc = kbuf[slot, :, c * step:(c + 1) * step].astype(BF16)
        s = jnp.dot(qi, kc, preferred_element_type=F32)
        o_ref[0, :, c * step:(c + 1) * step] = jnp.sum(jnp.maximum(s, 0.0) * w, axis=0, keepdims=True)
    s_new = jnp.sum(qi.astype(F32) * kin_ref[0].astype(BF16).astype(F32), axis=-1, keepdims=True)
    s_new = jnp.sum(jnp.maximum(s_new, 0.0) * w, axis=0, keepdims=True)
    lane = lax.broadcasted_iota(jnp.int32, (1, LANES), 1)
    o_ref[0, :, past:] = jnp.where(lane == 0, s_new, -jnp.inf)


def sample_scores(page_table, qi, wi, ki_new, cache_idx_k):
    bsz, n_pages = page_table.shape
    past = n_pages * PAGE_SIZE
    grid_spec = pltpu.PrefetchScalarGridSpec(
        num_scalar_prefetch=1,
        grid=(bsz,),
        in_specs=[pl.BlockSpec((1, H_IDX, D_IDX), lambda i, pt: (i, 0, 0)),
                  pl.BlockSpec((1, H_IDX, 1), lambda i, pt: (i, 0, 0)),
                  pl.BlockSpec((1, 1, D_IDX), lambda i, pt: (i, 0, 0)),
                  pl.BlockSpec(memory_space=pl.ANY)],
        out_specs=pl.BlockSpec((1, 1, past + LANES), lambda i, pt: (i, 0, 0)),
        scratch_shapes=[pltpu.VMEM((2, D_IDX, past), F32), pltpu.SemaphoreType.DMA((2,))],
    )
    return pl.pallas_call(
        functools.partial(_sample_score_kernel, n_pages=n_pages, n_batch=bsz),
        grid_spec=grid_spec,
        out_shape=jax.ShapeDtypeStruct((bsz, 1, past + LANES), F32),
        compiler_params=_cparams(("arbitrary",)),
        name="sample_scores",
    )(page_table, qi, wi, ki_new, jnp.swapaxes(cache_idx_k, 1, 2))


def _sample_select_kernel(score_ref, pos_ref, o_ref, rank_scr, *, topk):
    score = score_ref[...]
    bsz, n = score.shape
    key = _sortable_key(score)
    thr = _kth_largest_key(key, topk)
    gt = key > thr
    eq = key == thr
    need = topk - jnp.sum(gt.astype(jnp.int32), axis=-1, keepdims=True)
    rank_eq = _exclusive_cumsum_lanes(jnp.where(eq, 1.0, 0.0).astype(BF16))
    sel = jnp.logical_or(gt, jnp.logical_and(eq, rank_eq < need.astype(F32)))
    self32 = jnp.where(sel, 1.0, 0.0)
    rank = _exclusive_cumsum_lanes(self32.astype(BF16))
    rank_scr[...] = jnp.where(sel, rank, -1.0)
    slot_id = lax.broadcasted_iota(jnp.int32, (topk, 1), 0).astype(F32)
    n_split = 3
    step = n // n_split

    def body(bb, c):
        acc = jnp.zeros((topk, LANES), F32)
        for s in range(n_split):
            onehot = jnp.where(rank_scr[pl.ds(bb, 1), s * step:(s + 1) * step] == slot_id, 1.0, 0.0).astype(BF16)
            acc = acc + jnp.dot(onehot, pos_ref[s * step:(s + 1) * step, :], preferred_element_type=F32)
        o_ref[bb] = acc
        return c

    lax.fori_loop(0, bsz, body, 0)


def sample_select(scores, topk):
    bsz, n = scores.shape
    s = jnp.arange(n, dtype=jnp.int32)
    lane = jnp.arange(LANES, dtype=jnp.int32)[None, :]
    pos = jnp.where(lane == 0, (s // PAGE_SIZE)[:, None], jnp.where(lane == 1, (s % PAGE_SIZE)[:, None], 0)).astype(BF16)
    return pl.pallas_call(
        functools.partial(_sample_select_kernel, topk=topk),
        out_shape=jax.ShapeDtypeStruct((bsz, topk, LANES), F32),
        scratch_shapes=[pltpu.VMEM((bsz, n), F32)],
        compiler_params=pltpu.CompilerParams(vmem_limit_bytes=VMEM_LIMIT),
        name="sample_select",
    )(scores, pos)


def _sample_attend_kernel(idx_ref, pt_ref, q_ref, knew_ref, vnew_ref, ck_ref, cv_ref, o_ref, kbuf, vbuf, ksem, vsem,
                          *, topk, past, n_batch):
    b = pl.program_id(0)

    def row_copies(bb, slot, r, in_past):
        i = jnp.minimum(idx_ref[bb, r], past - 1)
        phys = pt_ref[bb, i // PAGE_SIZE]
        off = i % PAGE_SIZE
        ksrc = ck_ref.at[phys, off] if in_past else knew_ref.at[bb]
        vsrc = cv_ref.at[phys, off] if in_past else vnew_ref.at[bb]
        return (pltpu.make_async_copy(ksrc, kbuf.at[slot, r], ksem.at[slot]),
                pltpu.make_async_copy(vsrc, vbuf.at[slot, r], vsem.at[slot]))

    def start_all(bb, slot):
        def body(r, c):
            old = idx_ref[bb, r] < past

            @pl.when(old)
            def _():
                for cp in row_copies(bb, slot, r, True):
                    cp.start()

            @pl.when(jnp.logical_not(old))
            def _():
                for cp in row_copies(bb, slot, r, False):
                    cp.start()

            return c

        lax.fori_loop(0, topk, body, 0)

    @pl.when(b == 0)
    def _():
        start_all(0, 0)

    @pl.when(b + 1 < n_batch)
    def _():
        start_all(b + 1, (b + 1) % 2)

    slot = b % 2

    def wait_body(r, c):
        for cp in row_copies(b, slot, r, True):
            cp.wait()
        return c

    lax.fori_loop(0, topk, wait_body, 0)

    q = q_ref[0]
    k = kbuf[slot]
    logits = jnp.sum(k * q[None], axis=-1, keepdims=True) * (HEAD_DIM ** -0.5)
    m = jnp.max(logits, axis=0, keepdims=True)
    p = jnp.exp(logits - m)
    den = jnp.sum(p, axis=0)
    o_ref[0] = jnp.sum(p * vbuf[slot], axis=0) / den


def sample_attend(idx, page_table, q, k_new, v_new, cache_k, cache_v):
    bsz, h, dh = q.shape
    topk = idx.shape[1]
    past = page_table.shape[1] * PAGE_SIZE
    grid_spec = pltpu.PrefetchScalarGridSpec(
        num_scalar_prefetch=2,
        grid=(bsz,),
        in_specs=[pl.BlockSpec((1, h, dh), lambda i, a, c: (i, 0, 0)),
                  pl.BlockSpec(memory_space=pl.ANY), pl.BlockSpec(memory_space=pl.ANY),
                  pl.BlockSpec(memory_space=pl.ANY), pl.BlockSpec(memory_space=pl.ANY)],
        out_specs=pl.BlockSpec((1, h, dh), lambda i, a, c: (i, 0, 0)),
        scratch_shapes=[pltpu.VMEM((2, topk, h, dh), F32), pltpu.VMEM((2, topk, h, dh), F32),
                        pltpu.SemaphoreType.DMA((2,)), pltpu.SemaphoreType.DMA((2,))],
    )
    return pl.pallas_call(
        functools.partial(_sample_attend_kernel, topk=topk, past=past, n_batch=bsz),
        grid_spec=grid_spec,
        out_shape=jax.ShapeDtypeStruct((bsz, h, dh), F32),
        compiler_params=_cparams(("arbitrary",)),
        name="sample_attend",
    )(idx, page_table, q, k_new, v_new, cache_k, cache_v)


def _merge_kernel(h_ref, s_ref, a_ref, x_ref, wgs_ref, wga_ref, wbs_ref, wba_ref, wo_ref, gn_ref, o_ref, hn_ref,
                  *, n_tiles):
    j = pl.program_id(1)
    dot = functools.partial(jnp.dot, preferred_element_type=F32)
    h = h_ref[...]
    merged = (jax.nn.sigmoid(dot(h, wgs_ref[...])) * dot(s_ref[...], wbs_ref[...])
              + jax.nn.sigmoid(dot(h, wga_ref[...])) * dot(a_ref[...], wba_ref[...]))
    part = dot(merged.astype(BF16), wo_ref[...])

    @pl.when(j == 0)
    def _():
        o_ref[...] = x_ref[...] + part

    @pl.when(j > 0)
    def _():
        o_ref[...] += part

    @pl.when(j == n_tiles - 1)
    def _():
        x1 = o_ref[...]
        ms = jnp.mean(x1 * x1, axis=-1, keepdims=True)
        hn_ref[...] = (x1 * lax.rsqrt(ms + RMS_EPS) * gn_ref[...]).astype(hn_ref.dtype)


def merge_branches(h, ssm_out, att, x2d, w_gates, w_bs, w_ba, w_out, g_next, tm, tn):
    m, d = x2d.shape
    kdim = ssm_out.shape[1]
    nj = d // tn
    row = lambda c: pl.BlockSpec((tm, c), lambda i, j: (i, 0))
    return pl.pallas_call(
        functools.partial(_merge_kernel, n_tiles=nj),
        grid=(m // tm, nj),
        in_specs=[row(d), row(kdim), row(kdim), row(d),
                  pl.BlockSpec((d, tn), lambda i, j: (0, j)), pl.BlockSpec((d, tn), lambda i, j: (0, j + nj)),
                  pl.BlockSpec((kdim, tn), lambda i, j: (0, j)), pl.BlockSpec((kdim, tn), lambda i, j: (0, j)),
                  pl.BlockSpec((tn, d), lambda i, j: (j, 0)), pl.BlockSpec((1, d), lambda i, j: (0, 0))],
        out_specs=[row(d), row(d)],
        out_shape=[jax.ShapeDtypeStruct((m, d), F32), jax.ShapeDtypeStruct((m, d), BF16)],
        compiler_params=_cparams(("parallel", "arbitrary")),
        name="merge_branches",
    )(h, ssm_out, att, x2d, w_gates, w_gates, w_bs, w_ba, w_out, g_next.reshape(1, d).astype(F32))


def _ffn_step_kernel(x_ref, gf_ref, gfin_ref, wa_ref, wb_ref, cwa_ref, cwb_ref, cba_ref, cbb_ref, wd_ref,
                     p2a_ref, p1a_ref, p2b_ref, p1b_ref, y_ref, ca_ref, cb_ref, h_scr, acc_scr, *, n_ff_tiles):
    j = pl.program_id(1)

    @pl.when(j == 0)
    def _():
        x = x_ref[0]
        ms = jnp.mean(x * x, axis=-1, keepdims=True)
        h_scr[...] = (x * lax.rsqrt(ms + RMS_EPS) * gf_ref[...]).astype(BF16)
        acc_scr[...] = jnp.zeros_like(acc_scr)

    h = h_scr[...]

    def conv_half(w_ref, cw_ref, cbias_ref, out_ref, p2_ref, p1_ref):
        up = jnp.dot(h, w_ref[...], preferred_element_type=F32)
        cw = cw_ref[...]
        out_ref[...] = up
        return cbias_ref[...] + cw[0:1, :] * p2_ref[...] + cw[1:2, :] * p1_ref[...] + cw[2:3, :] * up

    a = conv_half(wa_ref, cwa_ref, cba_ref, ca_ref, p2a_ref, p1a_ref)
    g = conv_half(wb_ref, cwb_ref, cbb_ref, cb_ref, p2b_ref, p1b_ref)
    act = (jax.nn.silu(a) * g).astype(BF16)
    acc_scr[...] += jnp.dot(act, wd_ref[...], preferred_element_type=F32)

    @pl.when(j == n_ff_tiles - 1)
    def _():
        x2 = x_ref[0] + acc_scr[...]
        ms = jnp.mean(x2 * x2, axis=-1, keepdims=True)
        y_ref[0] = x2 * lax.rsqrt(ms + RMS_EPS) * gfin_ref[...]


def conv_ffn_step(x, conv_prev, g_ffn, g_final, w_up, conv_w, conv_b, w_down, tm, tf):
    b, t, d = x.shape
    f = w_down.shape[0]
    nf = f // tf
    gf = g_ffn.reshape(1, d).astype(F32)
    gfin = g_final.reshape(1, d).astype(F32)
    cb2 = conv_b.reshape(1, 2 * f).astype(F32)
    conv_w = conv_w.astype(F32)
    kern = functools.partial(_ffn_step_kernel, n_ff_tiles=nf)
    p2, p1 = conv_prev[:, 0, :], conv_prev[:, 1, :]
    wspec = lambda blk, off: pl.BlockSpec(blk, lambda i, j: (0, j + off))
    in_specs = [pl.BlockSpec((1, tm, d), lambda i, j: (0, i, 0)),
                pl.BlockSpec((1, d), lambda i, j: (0, 0)), pl.BlockSpec((1, d), lambda i, j: (0, 0)),
                wspec((d, tf), 0), wspec((d, tf), nf), wspec((CONV_W, tf), 0), wspec((CONV_W, tf), nf),
                wspec((1, tf), 0), wspec((1, tf), nf),
                pl.BlockSpec((tf, d), lambda i, j: (j, 0)),
                pl.BlockSpec((tm, tf), lambda i, j: (i, j)), pl.BlockSpec((tm, tf), lambda i, j: (i, j)),
                pl.BlockSpec((tm, tf), lambda i, j: (i, j + nf)), pl.BlockSpec((tm, tf), lambda i, j: (i, j + nf))]
    out_specs = [pl.BlockSpec((1, tm, d), lambda i, j: (0, i, 0)),
                 pl.BlockSpec((tm, tf), lambda i, j: (i, j)), pl.BlockSpec((tm, tf), lambda i, j: (i, j))]
    out_shape = [jax.ShapeDtypeStruct((1, t, d), F32), jax.ShapeDtypeStruct((t, f), F32),
                 jax.ShapeDtypeStruct((t, f), F32)]
    scratch = [pltpu.VMEM((tm, d), BF16), pltpu.VMEM((tm, d), F32)]
    y, ua, ub = pl.pallas_call(
        kern, grid=(t // tm, nf), in_specs=in_specs, out_specs=out_specs, out_shape=out_shape,
        scratch_shapes=scratch, compiler_params=_cparams(("arbitrary", "arbitrary")),
        name="conv_ffn_step",
    )(x, gf, gfin, w_up, w_up, conv_w, conv_w, cb2, cb2, w_down, p2, p1, p2, p1)
    return y, jnp.stack([p1, jnp.concatenate([ua, ub], axis=-1)], axis=1)


def _ffn_seq_kernel(h_ref, xt_ref, gfin_ref, wa_ref, wb_ref, cwa_ref, cwb_ref, cba_ref, cbb_ref, wd_ref, pa_ref, pb_ref,
                    y_ref, ca_ref, cb_ref, act_scr, x2_scr, car_a, car_b, *, tm, tf, nf, tn, nd):
    i = pl.program_id(1)
    j = pl.program_id(2)

    @pl.when(j < nf)
    def _():
        sub = min(tm, FFN_SUB_ROWS)
        row = lax.broadcasted_iota(jnp.int32, (sub, tf), 0)

        def conv_half(w_ref, cw_ref, cbias_ref, out_ref, prev_ref, car):
            cw = cw_ref[...]
            cbias = cbias_ref[...]

            @pl.when(i == 0)
            def _():
                car[j] = prev_ref[0]

            c2 = car[j]
            rm2, rm1 = c2[0:1, :], c2[1:2, :]
            outs = []
            for r0 in range(0, tm, sub):
                up = jnp.dot(h_ref[0, r0:r0 + sub, :], w_ref[...], preferred_element_type=F32)
                p1 = jnp.where(row == 0, rm1, pltpu.roll(up, 1, 0))
                p2 = jnp.where(row == 0, rm2, jnp.where(row == 1, rm1, pltpu.roll(up, 2, 0)))
                outs.append(cbias + cw[0:1, :] * p2 + cw[1:2, :] * p1 + cw[2:3, :] * up)
                rm2, rm1 = up[sub - 2:sub - 1, :], up[sub - 1:sub, :]
            last2 = jnp.concatenate([rm2, rm1], axis=0)
            car[j] = last2
            out_ref[0, 0] = last2
            return outs

        a = conv_half(wa_ref, cwa_ref, cba_ref, ca_ref, pa_ref, car_a)
        g = conv_half(wb_ref, cwb_ref, cbb_ref, cb_ref, pb_ref, car_b)
        act = [(jax.nn.silu(ar) * gr).astype(BF16) for ar, gr in zip(a, g)]
        for k in range(nf):

            @pl.when(j == k)
            def _(k=k):
                for r, act_r in enumerate(act):
                    act_scr[r * sub:(r + 1) * sub, k * tf:(k + 1) * tf] = act_r

    @pl.when(j >= nf)
    def _():
        part = xt_ref[0] + jnp.dot(act_scr[...], wd_ref[...], preferred_element_type=F32)
        for n in range(nd):

            @pl.when(j == nf + n)
            def _(n=n):
                x2_scr[:, n * tn:(n + 1) * tn] = part

    @pl.when(j == nf + nd - 1)
    def _():
        x2 = x2_scr[...]
        ms = jnp.mean(x2 * x2, axis=-1, keepdims=True)
        y_ref[0] = x2 * lax.rsqrt(ms + RMS_EPS) * gfin_ref[...]


def conv_ffn_sequence(h, x, conv_prev, g_final, w_up, conv_w, conv_b, w_down, tm, tf, tn):
    b, t, d = x.shape
    f = w_down.shape[0]
    nf, nd, nt = f // tf, d // tn, t // tm
    gfin = g_final.reshape(1, d).astype(F32)
    cb2 = conv_b.reshape(1, 2 * f).astype(F32)
    conv_w = conv_w.astype(F32)
    up_tile = lambda j: jnp.minimum(j, nf - 1)
    down_tile = lambda j: jnp.clip(j - nf, 0, nd - 1)
    wspec = lambda blk, off: pl.BlockSpec(blk, lambda bi, i, j: (0, up_tile(j) + off))
    prev_spec = lambda off: pl.BlockSpec((1, CONV_W - 1, tf), lambda bi, i, j: (bi, 0, up_tile(j) + off))
    state_spec = pl.BlockSpec((1, 1, CONV_W - 1, tf), lambda bi, i, j: (bi, i, 0, up_tile(j)))
    y, ca, cb = pl.pallas_call(
        functools.partial(_ffn_seq_kernel, tm=tm, tf=tf, nf=nf, tn=tn, nd=nd),
        grid=(b, nt, nf + nd),
        in_specs=[pl.BlockSpec((1, tm, d), lambda bi, i, j: (bi, i, 0)),
                  pl.BlockSpec((1, tm, tn), lambda bi, i, j: (bi, i, down_tile(j))),
                  pl.BlockSpec((1, d), lambda bi, i, j: (0, 0)),
                  wspec((d, tf), 0), wspec((d, tf), nf), wspec((CONV_W, tf), 0), wspec((CONV_W, tf), nf),
                  wspec((1, tf), 0), wspec((1, tf), nf),
                  pl.BlockSpec((f, tn), lambda bi, i, j: (0, down_tile(j))),
                  prev_spec(0), prev_spec(nf)],
        out_specs=[pl.BlockSpec((1, tm, d), lambda bi, i, j: (bi, i, 0)), state_spec, state_spec],
        out_shape=[jax.ShapeDtypeStruct((b, t, d), F32), jax.ShapeDtypeStruct((b, nt, CONV_W - 1, f), F32),
                   jax.ShapeDtypeStruct((b, nt, CONV_W - 1, f), F32)],
        scratch_shapes=[pltpu.VMEM((tm, f), BF16), pltpu.VMEM((tm, d), F32),
                        pltpu.VMEM((nf, CONV_W - 1, tf), F32), pltpu.VMEM((nf, CONV_W - 1, tf), F32)],
        compiler_params=_cparams(("arbitrary", "arbitrary", "arbitrary")),
        name="conv_ffn_seq",
    )(h, x, gfin, w_up, w_up, conv_w, conv_w, cb2, cb2, w_down, conv_prev, conv_prev)
    return y, jnp.concatenate([ca[:, -1], cb[:, -1]], axis=-1)


def _split_w_in(w_in, d_model, n_ssm, n_att):
    sizes = [n_ssm, n_att, n_att, n_att, H_IDX * D_IDX, D_IDX, H_IDX, d_model, d_model]
    offs = np.concatenate([[0], np.cumsum(sizes)]).tolist()
    col = lambda i: w_in[:, offs[i]:offs[i + 1]]
    w_uv = jnp.concatenate([col(0), col(3)], axis=1).astype(BF16)
    w_qk = jnp.concatenate([col(1), col(2)], axis=1).astype(BF16)
    w_qi = col(4).astype(BF16)
    pad = jnp.zeros((w_in.shape[0], LANES - D_IDX - H_IDX), w_in.dtype)
    w_kw = jnp.concatenate([col(5), col(6), pad], axis=1).astype(BF16)
    w_g = jnp.concatenate([col(7), col(8)], axis=1).astype(BF16)
    return w_uv, w_qk, w_qi, w_kw, w_g


def _mix_inputs(x2d, pos, n_pos_tiles, tm, norm_mix, w_groups, n_ssm, n_att, sequence):
    w_uv, w_qk, w_qi, w_kw, w_g = w_groups
    h = rmsnorm_bf16(x2d, norm_mix, tm)
    tn = 512
    nu, na = n_ssm // tn, n_att // tn
    half_qk = HEAD_DIM // ROPE_FRACTION // 2
    half_idx = D_IDX // ROPE_FRACTION // 2
    out = {}
    if sequence:
        out["u"], out["v"], out["v16"] = project(
            h, w_uv, tm, tn, [(0, nu, F32, "flat"), (nu, na, F32, "flat"), (nu, na, BF16, "flat")])
        out["q16"], out["k"], out["k16"] = project(
            h, w_qk, tm, tn, [(0, na, BF16, "flat"), (na, na, F32, "flat"), (na, na, BF16, "flat")],
            "rope", half_qk, rope_tables(pos, HEAD_DIM, tn), n_pos_tiles)
        out["qi16"], = project(h, w_qi, tm, tn, [(0, H_IDX * D_IDX // tn, BF16, "heads")],
                               "rope", half_idx, rope_tables(pos, D_IDX, tn), n_pos_tiles)
    else:
        out["u"], out["v"] = project(h, w_uv, tm, tn, [(0, nu, F32, "flat"), (nu, na, F32, "flat")])
        out["q"], out["k"] = project(h, w_qk, tm, tn, [(0, na, F32, "flat"), (na, na, F32, "flat")],
                                     "rope", half_qk, rope_tables(pos, HEAD_DIM, tn), n_pos_tiles)
        out["qi"], = project(h, w_qi, tm, tn, [(0, H_IDX * D_IDX // tn, F32, "flat")],
                             "rope", half_idx, rope_tables(pos, D_IDX, tn), n_pos_tiles)
    extra = jnp.concatenate([jnp.full((H_IDX,), H_IDX ** -0.5, F32), jnp.zeros((LANES - D_IDX - H_IDX,), F32)])
    kw, = project(h, w_kw, tm, LANES, [(0, 1, F32, "flat")], "rope", half_idx,
                  rope_tables(pos, D_IDX, LANES, extra), n_pos_tiles)
    out["ki"], out["wi"] = kw[:, :D_IDX], kw[:, D_IDX:D_IDX + H_IDX]
    out["h"] = h
    return out


def kernel(x_prompt, x_sample, cache_k, cache_v, cache_idx_k, page_table, state_ssm_re, state_ssm_im,
           state_ffn_conv, norm_mix, w_in, ssm_A_re, ssm_A_im, ssm_log_dt, ssm_B_re, ssm_B_im, ssm_C_re,
           ssm_C_im, ssm_D, w_glu, w_branch_ssm, w_branch_att, w_out, norm_ffn, w_up, ffn_conv_w,
           ffn_conv_b, w_down, norm_final):
    bp, t, d = x_prompt.shape
    bs, ts, _ = x_sample.shape
    n_ssm = ssm_D.shape[0]
    n_att = w_branch_att.shape[0]
    g = n_ssm // SSM_GROUP
    h_att = n_att // HEAD_DIM
    past = page_table.shape[1] * PAGE_SIZE
    assert ts == 1, "sample group is a single decode step"
    assert bp == SUBLANES, "the S5 sequence kernel keeps one batch row per sublane"

    w_groups = _split_w_in(w_in, d, n_ssm, n_att)
    w_glu_b, w_bs, w_ba, w_out_b = (w.astype(BF16) for w in (w_glu, w_branch_ssm, w_branch_att, w_out))
    w_up_b, w_down_b = w_up.astype(BF16), w_down.astype(BF16)
    ssm_params = (ssm_A_re, ssm_A_im, ssm_log_dt, ssm_B_re, ssm_B_im, ssm_C_re, ssm_C_im)

    mp = bp * t
    tm = 1024
    xp2 = x_prompt.reshape(mp, d)
    pos_p = jnp.arange(t, dtype=jnp.int32)
    pr = _mix_inputs(xp2, pos_p, t // tm, tm, norm_mix, w_groups, n_ssm, n_att, True)
    zeros_s = jnp.zeros((bp, g, SSM_STATE), F32)
    y_act, s_re_p, s_im_p = s5_sequence(pr["u"].reshape(bp, t, n_ssm), zeros_s, zeros_s, ssm_D,
                                        ssm_tile_matrices(*ssm_params, SSM_CHUNK, BF16), SSM_CHUNK, 512)
    ssm_out = ssm_glu(y_act.reshape(mp, n_ssm), w_glu_b, 512)
    att = prompt_attention(pr["qi16"], pr["ki"].astype(BF16).reshape(bp, t, D_IDX), pr["wi"].reshape(bp, t, H_IDX),
                           pr["q16"].reshape(bp, t, n_att), pr["k16"].reshape(bp, t, n_att),
                           pr["v16"].reshape(bp, t, n_att), 256, 4)
    x1, h_ffn = merge_branches(pr["h"], ssm_out, att.reshape(mp, n_att), xp2, w_groups[4], w_bs, w_ba, w_out_b,
                               norm_ffn, 512, 512)
    zeros_c = jnp.zeros((bp, CONV_W - 1, w_up.shape[1]), F32)
    y_prompt, conv_p = conv_ffn_sequence(h_ffn.reshape(bp, t, d), x1.reshape(bp, t, d), zeros_c, norm_final, w_up_b,
                                         ffn_conv_w, ffn_conv_b, w_down_b, 512, 512, 512)
    k_p = pr["k"].reshape(bp, t, h_att, HEAD_DIM)
    v_p = pr["v"].reshape(bp, t, h_att, HEAD_DIM)
    ki_p = pr["ki"].reshape(bp, t, D_IDX)

    xs2 = x_sample.reshape(bs, d)
    pos_s = jnp.full((bs,), past, jnp.int32)
    sm = _mix_inputs(xs2, pos_s, 1, bs, norm_mix, w_groups, n_ssm, n_att, False)
    y_act, s_re_s, s_im_s = s5_step(sm["u"], state_ssm_re, state_ssm_im, ssm_D, ssm_tile_matrices(*ssm_params, 1, F32))
    ssm_out = ssm_glu(y_act, w_glu_b, bs)
    scores = sample_scores(page_table, sm["qi"].astype(BF16).reshape(bs, H_IDX, D_IDX), sm["wi"].reshape(bs, H_IDX, 1),
                           sm["ki"].reshape(bs, 1, D_IDX), cache_idx_k)
    topk = min(TOPK_MAX, (past + 1) // 4)
    hl = sample_select(scores.reshape(bs, past + LANES), topk)
    idx = (hl[:, :, 0] * PAGE_SIZE + hl[:, :, 1]).astype(jnp.int32)
    att = sample_attend(idx, page_table, sm["q"].reshape(bs, h_att, HEAD_DIM), sm["k"].reshape(bs, h_att, HEAD_DIM),
                        sm["v"].reshape(bs, h_att, HEAD_DIM), cache_k, cache_v)
    x1, _ = merge_branches(sm["h"], ssm_out, att.reshape(bs, n_att).astype(BF16), xs2, w_groups[4], w_bs, w_ba, w_out_b,
                           norm_ffn, bs, 512)
    y_s, conv_s = conv_ffn_step(x1.reshape(1, bs, d), state_ffn_conv, norm_ffn, norm_final, w_up_b, ffn_conv_w,
                                ffn_conv_b, w_down_b, bs, 512)
    y_sample = y_s.reshape(bs, 1, d)
    k_s = sm["k"].reshape(bs, 1, h_att, HEAD_DIM)
    v_s = sm["v"].reshape(bs, 1, h_att, HEAD_DIM)
    ki_s = sm["ki"].reshape(bs, 1, D_IDX)

    return (y_prompt, y_sample, k_p, v_p, ki_p, k_s, v_s, ki_s, s_re_p, s_im_p, s_re_s, s_im_s, conv_p, conv_s)
```

```python
import functools

import numpy as np
import jax
import jax.numpy as jnp
from jax import lax
from jax.experimental import pallas as pl
from jax.experimental.pallas import tpu as pltpu

F32 = jnp.float32
BF16 = jnp.bfloat16

HEAD_DIM = 128
SSM_GROUP = 16
SSM_STATE = 64
H_IDX = 16
D_IDX = 64
TOPK_MAX = 256
ROPE_THETA = 500000.0
ROPE_FRACTION = 4
CONV_W = 3
RMS_EPS = 1e-6
NEG_INF = -1e30
PAGE_SIZE = 128

LANES = 128
SUBLANES = 8
MXU_WIDTH = 256
GROUPS_PER_TILE = LANES // SSM_GROUP
SSM_CHUNK = 8
PROJ_SUB_ROWS = 256
FFN_SUB_ROWS = 256
COL_TILE = 512
VMEM_LIMIT = 56 * 1024 * 1024
INT_MIN = -2 ** 31
NT_DIMS = (((1,), (1,)), ((), ()))


def _cparams(sem):
    return pltpu.CompilerParams(dimension_semantics=sem, vmem_limit_bytes=VMEM_LIMIT)


def _rmsnorm_kernel(x_ref, g_ref, o_ref):
    x = x_ref[...]
    ms = jnp.mean(x * x, axis=-1, keepdims=True)
    o_ref[...] = (x * lax.rsqrt(ms + RMS_EPS) * g_ref[...]).astype(o_ref.dtype)


def rmsnorm_bf16(x2d, g, tm):
    m, d = x2d.shape
    return pl.pallas_call(
        _rmsnorm_kernel,
        grid=(m // tm,),
        in_specs=[pl.BlockSpec((tm, d), lambda i: (i, 0)), pl.BlockSpec((1, d), lambda i: (0, 0))],
        out_specs=pl.BlockSpec((tm, d), lambda i: (i, 0)),
        out_shape=jax.ShapeDtypeStruct((m, d), BF16),
        compiler_params=_cparams(("parallel",)),
        name="rmsnorm",
    )(x2d, g.reshape(1, d))


def _proj_kernel(h_ref, w_ref, *rest, mode, shift, outs, n_tab, n_col_tiles):
    tabs, o_refs = rest[:n_tab], rest[n_tab:]
    j = pl.program_id(1)
    tm, tn = h_ref.shape[0], w_ref.shape[2]
    sub = min(tm, PROJ_SUB_ROWS)
    for r0 in range(0, tm, sub):
        rows = slice(r0, r0 + sub)
        z = jnp.dot(h_ref[rows, :], w_ref[0], preferred_element_type=F32)
        if mode == "rope":
            c_ref, s1_ref, s2_ref = tabs
            z = (z * c_ref[rows, :] + pltpu.roll(z, tn - shift, 1) * s1_ref[rows, :]
                 + pltpu.roll(z, shift, 1) * s2_ref[rows, :])
        for (start, cnt, _, kind), o_ref in zip(outs, o_refs):
            if kind == "heads":
                assert start + cnt == n_col_tiles
                for hh in range(tn // D_IDX):
                    o_ref[0, hh, rows, :] = z[:, hh * D_IDX:(hh + 1) * D_IDX].astype(o_ref.dtype)
            elif start + cnt == n_col_tiles:
                o_ref[rows, :] = z.astype(o_ref.dtype)
            else:
                o_ref[rows, :] = jnp.where(j >= start + cnt, o_ref[rows, :], z.astype(o_ref.dtype))


def project(h, w, tm, tn, outs, mode="plain", shift=0, tables=None, n_pos_tiles=1):
    m, k = h.shape
    n = w.shape[0] * tn
    in_specs = [pl.BlockSpec((tm, k), lambda i, j: (i, 0)), pl.BlockSpec((1, k, tn), lambda i, j: (j, 0, 0))]
    args = [h, w]
    tables = tables or ()
    for t in tables:
        in_specs.append(pl.BlockSpec((tm, tn), lambda i, j: (i % n_pos_tiles, 0)))
        args.append(t)
    out_specs, out_shape = [], []
    for start, cnt, dtype, kind in outs:
        if kind == "heads":
            hpt = tn // D_IDX
            out_specs.append(pl.BlockSpec(
                (1, hpt, tm, D_IDX),
                lambda i, j, s=start, c=cnt: (i // n_pos_tiles, jnp.clip(j - s, 0, c - 1), i % n_pos_tiles, 0)))
            out_shape.append(jax.ShapeDtypeStruct((m // (n_pos_tiles * tm), cnt * hpt, n_pos_tiles * tm, D_IDX), dtype))
        else:
            out_specs.append(pl.BlockSpec((tm, tn), lambda i, j, s=start, c=cnt: (i, jnp.clip(j - s, 0, c - 1))))
            out_shape.append(jax.ShapeDtypeStruct((m, cnt * tn), dtype))
    return pl.pallas_call(
        functools.partial(_proj_kernel, mode=mode, shift=shift, outs=tuple(outs), n_tab=len(tables),
                          n_col_tiles=n // tn),
        grid=(m // tm, n // tn),
        in_specs=in_specs,
        out_specs=out_specs,
        out_shape=out_shape,
        compiler_params=_cparams(("parallel", "arbitrary")),
        name="proj_" + mode,
    )(*args)


def rope_tables(pos, head, tn, extra=None):
    r = head // ROPE_FRACTION
    half = r // 2
    inv = ROPE_THETA ** (-jnp.arange(half, dtype=F32) * 2.0 / r)
    ang = pos.astype(F32)[:, None] * inv[None, :]
    cos, sin = jnp.cos(ang), jnp.sin(ang)
    t = pos.shape[0]
    zeros = jnp.zeros((t, head - r), F32)
    zh = jnp.zeros((t, half), F32)
    c = jnp.concatenate([cos, cos, jnp.ones((t, head - r), F32)], axis=1)
    s1 = jnp.concatenate([-sin, zh, zeros], axis=1)
    s2 = jnp.concatenate([zh, sin, zeros], axis=1)
    if extra is None:
        reps = tn // head
        return tuple(jnp.tile(a, (1, reps)) for a in (c, s1, s2))
    pad = jnp.zeros((t, tn - head), F32)
    return (jnp.concatenate([c, jnp.broadcast_to(extra[None, :], (t, tn - head))], axis=1),
            jnp.concatenate([s1, pad], axis=1), jnp.concatenate([s2, pad], axis=1))


def ssm_tile_matrices(a_re, a_im, log_dt, b_re, b_im, c_re, c_im, chunk, dtype):
    hp = lax.Precision.HIGHEST
    g, p = a_re.shape
    gt = GROUPS_PER_TILE
    nq = g // gt
    a = lax.complex(a_re.astype(F32), a_im.astype(F32))
    dt = jnp.exp(log_dt.astype(F32))[:, None]
    adt = a * dt
    a_bar = jnp.exp(adt)
    b_bar = ((a_bar - 1.0) / a)[..., None] * lax.complex(b_re.astype(F32), b_im.astype(F32))
    cc = lax.complex(c_re.astype(F32), c_im.astype(F32))
    steps = jnp.arange(chunk + 1, dtype=F32)
    pw = jnp.exp(adt[:, None, :] * steps[None, :, None].astype(jnp.complex64))
    kd = jnp.real(jnp.einsum("gcp,gdp,gpe->gdce", cc, pw[:, :chunk], b_bar, precision=hp))
    w1g = pw[:, chunk - 1 - jnp.arange(chunk)][:, :, None, :] * b_bar.transpose(0, 2, 1)[:, None]
    m2g = cc.transpose(0, 2, 1)[:, :, None, :] * pw[:, 1:chunk + 1].transpose(0, 2, 1)[:, :, :, None]
    kdc = kd.reshape(nq, gt, chunk, SSM_GROUP, SSM_GROUP).transpose(0, 2, 4, 1, 3).reshape(nq, chunk, SSM_GROUP, LANES)
    w1c = w1g.reshape(nq, gt, chunk, SSM_GROUP, p).transpose(0, 2, 3, 1, 4).reshape(nq, chunk, SSM_GROUP, gt * p)
    m2c = m2g.reshape(nq, gt, p, chunk, SSM_GROUP).transpose(0, 3, 2, 1, 4).reshape(nq, chunk, p, LANES)
    al = pw[:, chunk].reshape(nq, 1, gt * p)
    kt, w1r, w1i, w2r, w2i = _ssm_expand(kdc, jnp.real(w1c), jnp.imag(w1c), jnp.real(m2c), -jnp.imag(m2c), chunk, dtype)
    return dict(w1r=w1r, w1i=w1i, kt=kt, w2r=w2r, w2i=w2i, ar=jnp.real(al), ai=jnp.imag(al))


def _ssm_expand_kernel(kd_ref, w1r_ref, w1i_ref, w2r_ref, w2i_ref, kt_o, w1r_o, w1i_o, w2r_o, w2i_o, *, chunk):
    gt, p = GROUPS_PER_TILE, SSM_STATE

    def same_group(rows_per_group, cols_per_group):
        shape = (gt * rows_per_group, gt * cols_per_group)
        r = jnp.right_shift(lax.broadcasted_iota(jnp.int32, shape, 0), rows_per_group.bit_length() - 1)
        c = jnp.right_shift(lax.broadcasted_iota(jnp.int32, shape, 1), cols_per_group.bit_length() - 1)
        return r == c

    def block_diag(small, mask, dtype):
        return jnp.where(mask, jnp.tile(small, (gt, 1)), 0.0).astype(dtype)

    m_cc, m_cp, m_pc = same_group(SSM_GROUP, SSM_GROUP), same_group(SSM_GROUP, p), same_group(p, SSM_GROUP)
    for j in range(chunk):
        rows = slice(j * LANES, (j + 1) * LANES)
        w1r_o[0, rows, :] = block_diag(w1r_ref[0, j], m_cp, w1r_o.dtype)
        w1i_o[0, rows, :] = block_diag(w1i_ref[0, j], m_cp, w1i_o.dtype)
        w2r_o[0, :, rows] = block_diag(w2r_ref[0, j], m_pc, w2r_o.dtype)
        w2i_o[0, :, rows] = block_diag(w2i_ref[0, j], m_pc, w2i_o.dtype)
        for t in range(chunk):
            cols = slice(t * LANES, (t + 1) * LANES)
            if t >= j:
                kt_o[0, rows, cols] = block_diag(kd_ref[0, t - j], m_cc, kt_o.dtype)
            else:
                kt_o[0, rows, cols] = jnp.zeros((LANES, LANES), kt_o.dtype)


def _ssm_expand(kdc, w1r, w1i, w2r, w2i, chunk, dtype):
    nq = kdc.shape[0]
    ns = w1r.shape[-1]
    cw = chunk * LANES
    spec4 = lambda a: pl.BlockSpec((1,) + a.shape[1:], lambda q: (q, 0, 0, 0))
    spec3 = lambda r, c: pl.BlockSpec((1, r, c), lambda q: (q, 0, 0))
    shapes = [(cw, cw), (cw, ns), (cw, ns), (ns, cw), (ns, cw)]
    return pl.pallas_call(
        functools.partial(_ssm_expand_kernel, chunk=chunk),
        grid=(nq,),
        in_specs=[spec4(a) for a in (kdc, w1r, w1i, w2r, w2i)],
        out_specs=[spec3(r, c) for r, c in shapes],
        out_shape=[jax.ShapeDtypeStruct((nq, r, c), dtype) for r, c in shapes],
        compiler_params=_cparams(("parallel",)),
        name="ssm_expand",
    )(kdc, w1r, w1i, w2r, w2i)


def _s5_seq_kernel(u_ref, w1r_ref, w1i_ref, kt_ref, w2r_ref, w2i_ref, ar_ref, ai_ref, d_ref, s0r_ref, s0i_ref,
                   o_ref, sfr_ref, sfi_ref, ucat, slr, sli, spr, spi, yc, yb, st_r, st_i, *, chunk, n_rows, nb):
    ts = pl.program_id(1)
    ns = st_r.shape[-1]

    @pl.when(ts == 0)
    def _():
        st_r[...] = s0r_ref[...]
        st_i[...] = s0i_ref[...]

    for b in range(nb):
        for tl in range(chunk):
            ucat[:, b, tl * LANES:(tl + 1) * LANES] = u_ref[b, pl.ds(tl, n_rows, stride=chunk), :]
    x = ucat[...].reshape(n_rows * nb, chunk * LANES).astype(BF16)
    slr[...] = jnp.dot(x, w1r_ref[0], preferred_element_type=F32).reshape(n_rows, nb, ns)
    sli[...] = jnp.dot(x, w1i_ref[0], preferred_element_type=F32).reshape(n_rows, nb, ns)
    ar = jnp.broadcast_to(ar_ref[0], (nb, ns))
    ai = jnp.broadcast_to(ai_ref[0], (nb, ns))

    def step(n, carry):
        sr, si = carry
        spr[n] = sr
        spi[n] = si
        return ar * sr - ai * si + slr[n], ar * si + ai * sr + sli[n]

    sr, si = lax.fori_loop(0, n_rows, step, (st_r[...], st_i[...]))
    st_r[...] = sr
    st_i[...] = si
    sfr_ref[...] = sr
    sfi_ref[...] = si
    y = (jnp.dot(x, kt_ref[0], preferred_element_type=F32)
         + jnp.dot(spr[...].reshape(n_rows * nb, ns).astype(BF16), w2r_ref[0], preferred_element_type=F32)
         + jnp.dot(spi[...].reshape(n_rows * nb, ns).astype(BF16), w2i_ref[0], preferred_element_type=F32))
    yc[...] = y.reshape(n_rows, nb, chunk * LANES)
    for b in range(nb):
        for tl in range(chunk):
            yb[b, pl.ds(tl, n_rows, stride=chunk), :] = yc[:, b, tl * LANES:(tl + 1) * LANES]
    o_ref[...] = jax.nn.gelu(yb[...] + d_ref[...] * u_ref[...]).astype(o_ref.dtype)


def s5_sequence(u, s0_re, s0_im, d_skip, mats, chunk, t_seg):
    b, t, n = u.shape
    g, p = s0_re.shape[1:]
    nq = n // LANES
    ns = GROUPS_PER_TILE * p
    cw = chunk * LANES
    n_rows = t_seg // chunk
    wspec = lambda r, c: pl.BlockSpec((1, r, c), lambda q, s: (q, 0, 0))
    state_spec = pl.BlockSpec((b, ns), lambda q, s: (0, q))
    y, sr, si = pl.pallas_call(
        functools.partial(_s5_seq_kernel, chunk=chunk, n_rows=n_rows, nb=b),
        grid=(nq, t // t_seg),
        in_specs=[pl.BlockSpec((b, t_seg, LANES), lambda q, s: (0, s, q)),
                  wspec(cw, ns), wspec(cw, ns), wspec(cw, cw), wspec(ns, cw), wspec(ns, cw),
                  wspec(1, ns), wspec(1, ns), pl.BlockSpec((1, LANES), lambda q, s: (0, q)), state_spec, state_spec],
        out_specs=[pl.BlockSpec((b, t_seg, LANES), lambda q, s: (0, s, q)), state_spec, state_spec],
        out_shape=[jax.ShapeDtypeStruct((b, t, n), BF16), jax.ShapeDtypeStruct((b, g * p), F32),
                   jax.ShapeDtypeStruct((b, g * p), F32)],
        scratch_shapes=[pltpu.VMEM((n_rows, b, cw), F32),
                        pltpu.VMEM((n_rows, b, ns), F32), pltpu.VMEM((n_rows, b, ns), F32),
                        pltpu.VMEM((n_rows, b, ns), F32), pltpu.VMEM((n_rows, b, ns), F32),
                        pltpu.VMEM((n_rows, b, cw), F32), pltpu.VMEM((b, t_seg, LANES), F32),
                        pltpu.VMEM((b, ns), F32), pltpu.VMEM((b, ns), F32)],
        compiler_params=_cparams(("parallel", "arbitrary")),
        name="s5_sequence",
    )(u, mats["w1r"], mats["w1i"], mats["kt"], mats["w2r"], mats["w2i"], mats["ar"], mats["ai"],
      d_skip.reshape(1, n).astype(F32),
      s0_re.astype(F32).reshape(b, g * p), s0_im.astype(F32).reshape(b, g * p))
    return y, sr.reshape(b, g, p), si.reshape(b, g, p)


def _s5_step_kernel(u_ref, w1r_ref, w1i_ref, kt_ref, w2r_ref, w2i_ref, ar_ref, ai_ref, d_ref, s0r_ref, s0i_ref,
                    o_ref, sfr_ref, sfi_ref):
    dot = functools.partial(jnp.dot, preferred_element_type=F32, precision=lax.Precision.HIGHEST)
    x = u_ref[...]
    sr, si = s0r_ref[...], s0i_ref[...]
    ar, ai = ar_ref[0], ai_ref[0]
    sfr_ref[...] = ar * sr - ai * si + dot(x, w1r_ref[0])
    sfi_ref[...] = ar * si + ai * sr + dot(x, w1i_ref[0])
    y = dot(x, kt_ref[0]) + dot(sr, w2r_ref[0]) + dot(si, w2i_ref[0])
    o_ref[...] = jax.nn.gelu(y + d_ref[...] * x).astype(o_ref.dtype)


def s5_step(u, s0_re, s0_im, d_skip, mats):
    b, n = u.shape
    g, p = s0_re.shape[1:]
    nq = n // LANES
    ns = GROUPS_PER_TILE * p
    wspec = lambda r, c: pl.BlockSpec((1, r, c), lambda q: (q, 0, 0))
    state_spec = pl.BlockSpec((b, ns), lambda q: (0, q))
    lane_spec = pl.BlockSpec((b, LANES), lambda q: (0, q))
    y, sr, si = pl.pallas_call(
        _s5_step_kernel,
        grid=(nq,),
        in_specs=[lane_spec, wspec(LANES, ns), wspec(LANES, ns), wspec(LANES, LANES), wspec(ns, LANES),
                  wspec(ns, LANES), wspec(1, ns), wspec(1, ns), pl.BlockSpec((1, LANES), lambda q: (0, q)),
                  state_spec, state_spec],
        out_specs=[lane_spec, state_spec, state_spec],
        out_shape=[jax.ShapeDtypeStruct((b, n), BF16), jax.ShapeDtypeStruct((b, g * p), F32),
                   jax.ShapeDtypeStruct((b, g * p), F32)],
        compiler_params=_cparams(("parallel",)),
        name="s5_step",
    )(u, mats["w1r"], mats["w1i"], mats["kt"], mats["w2r"], mats["w2i"], mats["ar"], mats["ai"],
      d_skip.reshape(1, n).astype(F32), s0_re.astype(F32).reshape(b, g * p), s0_im.astype(F32).reshape(b, g * p))
    return y, sr.reshape(b, g, p), si.reshape(b, g, p)


def _glu_kernel(y_ref, w_ref, o_ref):
    n = o_ref.shape[-1]
    z = jnp.dot(y_ref[...], w_ref[...], preferred_element_type=F32)
    o_ref[...] = (z[:, :n] * jax.nn.sigmoid(z[:, n:])).astype(o_ref.dtype)


def ssm_glu(y, w_glu, tm):
    m, n = y.shape
    return pl.pallas_call(
        _glu_kernel,
        grid=(m // tm,),
        in_specs=[pl.BlockSpec((tm, n), lambda i: (i, 0)), pl.BlockSpec((n, 2 * n), lambda i: (0, 0))],
        out_specs=pl.BlockSpec((tm, n), lambda i: (i, 0)),
        out_shape=jax.ShapeDtypeStruct((m, n), BF16),
        compiler_params=_cparams(("parallel",)),
        name="ssm_glu",
    )(y, w_glu)


def _sortable_key(score):
    bits = pltpu.bitcast(score, jnp.int32)
    return jnp.where(bits < 0, bits ^ jnp.int32(0x7FFFFFFF), bits)


def _kth_largest_key(key, k):
    rows = key.shape[0]

    def body(i, res):
        cand = res + jnp.left_shift(jnp.int32(1), 31 - i)
        cnt = jnp.sum(jnp.where(key >= cand, 1.0, 0.0), axis=-1, keepdims=True)
        return jnp.where(cnt >= float(k), cand, res)

    return lax.fori_loop(0, 32, body, jnp.full((rows, 1), INT_MIN, jnp.int32))


def _exclusive_cumsum_lanes(flags_bf16):
    rows, n = flags_bf16.shape
    r = lax.broadcasted_iota(jnp.int32, (LANES, LANES), 0)
    c = lax.broadcasted_iota(jnp.int32, (LANES, LANES), 1)
    tri = jnp.where(r < c, 1.0, 0.0).astype(BF16)
    off = jnp.zeros((rows, 1), F32)
    out = []
    for i in range(n // LANES):
        chunk = flags_bf16[:, i * LANES:(i + 1) * LANES]
        out.append(jnp.dot(chunk, tri, preferred_element_type=F32) + off)
        off = off + jnp.sum(chunk.astype(F32), axis=-1, keepdims=True)
    return jnp.concatenate(out, axis=-1)


def _attend_keys(qi_ref, ki_ref, q_ref, k_ref, v_ref, o_ref, score_scr, bias_scr, wb_scr, *, qb, tq, lk, topk):
    n_chunks = lk // MXU_WIDTH
    qi_all = qi_ref[0].reshape(H_IDX * tq, D_IDX)

    def chunk_body(c, carry):
        k0 = pl.multiple_of(c * MXU_WIDTH, MXU_WIDTH)
        s = lax.dot_general(qi_all, ki_ref[0, pl.ds(k0, MXU_WIDTH), :], NT_DIMS, preferred_element_type=F32)
        acc = jnp.zeros((tq, MXU_WIDTH), F32)
        for h in range(H_IDX):
            wb = wb_scr[h]
            acc = acc + jnp.maximum(s[h * tq:(h + 1) * tq], 0.0) * jnp.concatenate([wb, wb], axis=-1)
        score_scr[c] = acc
        return carry

    lax.fori_loop(0, n_chunks, chunk_body, 0)
    score = jnp.concatenate([score_scr[c] for c in range(n_chunks)], axis=-1)
    row = qb * tq + lax.broadcasted_iota(jnp.int32, (tq, lk), 0)
    col = lax.broadcasted_iota(jnp.int32, (tq, lk), 1)
    causal = col <= row
    key = _sortable_key(jnp.where(causal, score, -jnp.inf))
    thr = _kth_largest_key(key, topk)
    gt = key > thr
    eq = jnp.logical_and(key == thr, causal)
    need = topk - jnp.sum(gt.astype(jnp.int32), axis=-1, keepdims=True)
    n_eq = jnp.sum(eq.astype(jnp.int32), axis=-1, keepdims=True)
    bias_scr[:, :lk] = jnp.where(jnp.logical_and(causal, key >= thr), 0.0, NEG_INF)

    @pl.when(jnp.max(n_eq - need) > 0)
    def _():
        rank = _exclusive_cumsum_lanes(jnp.where(eq, 1.0, 0.0).astype(BF16))
        keep = jnp.logical_or(gt, jnp.logical_and(eq, rank < need.astype(F32)))
        bias_scr[:, :lk] = jnp.where(jnp.logical_and(causal, keep), 0.0, NEG_INF)

    bias = bias_scr[:, :lk]
    scale = HEAD_DIM ** -0.5
    for h in range(q_ref.shape[-1] // HEAD_DIM):
        sl = slice(h * HEAD_DIM, (h + 1) * HEAD_DIM)
        logits = lax.dot_general(q_ref[0, :, sl], k_ref[0, :lk, sl], NT_DIMS, preferred_element_type=F32) * scale + bias
        m = jnp.max(logits, axis=-1, keepdims=True)
        p = jnp.exp(logits - m)
        den = jnp.sum(p, axis=-1, keepdims=True)
        o = jnp.dot(p.astype(BF16), v_ref[0, :lk, sl], preferred_element_type=F32) / den
        o_ref[0, :, sl] = o.astype(o_ref.dtype)


def _prompt_attn_kernel(qi_ref, ki_ref, w_ref, q_ref, k_ref, v_ref, o_ref, score_scr, bias_scr, wb_scr,
                        *, tq, seq, topk, n_buckets):
    qb = pl.program_id(1)
    w = w_ref[0] * (D_IDX ** -0.5)
    for h in range(H_IDX):
        wb_scr[h] = jnp.broadcast_to(w[:, h:h + 1], (tq, LANES))
    per = (seq // tq) // n_buckets
    for bkt in range(n_buckets):
        body = functools.partial(_attend_keys, qi_ref, ki_ref, q_ref, k_ref, v_ref, o_ref, score_scr, bias_scr, wb_scr,
                                 qb=qb, tq=tq, lk=(bkt + 1) * per * tq, topk=topk)
        pl.when(qb // per == bkt)(body)


def prompt_attention(qi, ki, wi, q, k, v, tq, n_buckets):
    b, t, n = q.shape
    topk = min(TOPK_MAX, t // 4)
    return pl.pallas_call(
        functools.partial(_prompt_attn_kernel, tq=tq, seq=t, topk=topk, n_buckets=n_buckets),
        grid=(b, t // tq),
        in_specs=[pl.BlockSpec((1, H_IDX, tq, D_IDX), lambda i, j: (i, 0, j, 0)),
                  pl.BlockSpec((1, t, D_IDX), lambda i, j: (i, 0, 0)),
                  pl.BlockSpec((1, tq, H_IDX), lambda i, j: (i, j, 0)),
                  pl.BlockSpec((1, tq, n), lambda i, j: (i, j, 0)),
                  pl.BlockSpec((1, t, n), lambda i, j: (i, 0, 0)),
                  pl.BlockSpec((1, t, n), lambda i, j: (i, 0, 0))],
        out_specs=pl.BlockSpec((1, tq, n), lambda i, j: (i, j, 0)),
        out_shape=jax.ShapeDtypeStruct((b, t, n), BF16),
        scratch_shapes=[pltpu.VMEM((t // MXU_WIDTH, tq, MXU_WIDTH), F32), pltpu.VMEM((tq, t), F32),
                        pltpu.VMEM((H_IDX, tq, LANES), F32)],
        compiler_params=_cparams(("parallel", "parallel")),
        name="prompt_attn",
    )(qi, ki, wi, q, k, v)


def _sample_score_kernel(pt_ref, qi_ref, w_ref, kin_ref, cache_ref, o_ref, kbuf, sem, *, n_pages, n_batch):
    b = pl.program_id(0)
    past = n_pages * PAGE_SIZE

    def page_copy(bb, slot, pg):
        return pltpu.make_async_copy(cache_ref.at[pt_ref[bb, pg]],
                                     kbuf.at[slot, :, pl.ds(pl.multiple_of(pg * PAGE_SIZE, PAGE_SIZE), PAGE_SIZE)],
                                     sem.at[slot])

    def start_all(bb, slot):
        lax.fori_loop(0, n_pages, lambda pg, c: (page_copy(bb, slot, pg).start(), c)[1], 0)

    @pl.when(b == 0)
    def _():
        start_all(0, 0)

    @pl.when(b + 1 < n_batch)
    def _():
        start_all(b + 1, (b + 1) % 2)

    slot = b % 2
    lax.fori_loop(0, n_pages, lambda pg, c: (page_copy(b, slot, pg).wait(), c)[1], 0)

    qi = qi_ref[0]
    w = w_ref[0] * (D_IDX ** -0.5)
    step = 2048
    for c in range(past // step):
        kc = kbuf[slot, :, c * step:(c + 1) * step].astype(BF16)
        s = jnp.dot(qi, kc, preferred_element_type=F32)
        o_ref[0, :, c * step:(c + 1) * step] = jnp.sum(jnp.maximum(s, 0.0) * w, axis=0, keepdims=True)
    s_new = jnp.sum(qi.astype(F32) * kin_ref[0].astype(BF16).astype(F32), axis=-1, keepdims=True)
    s_new = jnp.sum(jnp.maximum(s_new, 0.0) * w, axis=0, keepdims=True)
    lane = lax.broadcasted_iota(jnp.int32, (1, LANES), 1)
    o_ref[0, :, past:] = jnp.where(lane == 0, s_new, -jnp.inf)


def sample_scores(page_table, qi, wi, ki_new, cache_idx_k):
    bsz, n_pages = page_table.shape
    past = n_pages * PAGE_SIZE
    grid_spec = pltpu.PrefetchScalarGridSpec(
        num_scalar_prefetch=1,
        grid=(bsz,),
        in_specs=[pl.BlockSpec((1, H_IDX, D_IDX), lambda i, pt: (i, 0, 0)),
                  pl.BlockSpec((1, H_IDX, 1), lambda i, pt: (i, 0, 0)),
                  pl.BlockSpec((1, 1, D_IDX), lambda i, pt: (i, 0, 0)),
                  pl.BlockSpec(memory_space=pl.ANY)],
        out_specs=pl.BlockSpec((1, 1, past + LANES), lambda i, pt: (i, 0, 0)),
        scratch_shapes=[pltpu.VMEM((2, D_IDX, past), F32), pltpu.SemaphoreType.DMA((2,))],
    )
    return pl.pallas_call(
        functools.partial(_sample_score_kernel, n_pages=n_pages, n_batch=bsz),
        grid_spec=grid_spec,
        out_shape=jax.ShapeDtypeStruct((bsz, 1, past + LANES), F32),
        compiler_params=_cparams(("arbitrary",)),
        name="sample_scores",
    )(page_table, qi, wi, ki_new, jnp.swapaxes(cache_idx_k, 1, 2))


def _sample_select_kernel(score_ref, pos_ref, o_ref, rank_scr, *, topk):
    score = score_ref[...]
    bsz, n = score.shape
    key = _sortable_key(score)
    thr = _kth_largest_key(key, topk)
    gt = key > thr
    eq = key == thr
    need = topk - jnp.sum(gt.astype(jnp.int32), axis=-1, keepdims=True)
    rank_eq = _exclusive_cumsum_lanes(jnp.where(eq, 1.0, 0.0).astype(BF16))
    sel = jnp.logical_or(gt, jnp.logical_and(eq, rank_eq < need.astype(F32)))
    self32 = jnp.where(sel, 1.0, 0.0)
    rank = _exclusive_cumsum_lanes(self32.astype(BF16))
    rank_scr[...] = jnp.where(sel, rank, -1.0)
    slot_id = lax.broadcasted_iota(jnp.int32, (topk, 1), 0).astype(F32)
    n_split = 3
    step = n // n_split

    def body(bb, c):
        acc = jnp.zeros((topk, LANES), F32)
        for s in range(n_split):
            onehot = jnp.where(rank_scr[pl.ds(bb, 1), s * step:(s + 1) * step] == slot_id, 1.0, 0.0).astype(BF16)
            acc = acc + jnp.dot(onehot, pos_ref[s * step:(s + 1) * step, :], preferred_element_type=F32)
        o_ref[bb] = acc
        return c

    lax.fori_loop(0, bsz, body, 0)


def sample_select(scores, topk):
    bsz, n = scores.shape
    s = jnp.arange(n, dtype=jnp.int32)
    lane = jnp.arange(LANES, dtype=jnp.int32)[None, :]
    pos = jnp.where(lane == 0, (s // PAGE_SIZE)[:, None], jnp.where(lane == 1, (s % PAGE_SIZE)[:, None], 0)).astype(BF16)
    return pl.pallas_call(
        functools.partial(_sample_select_kernel, topk=topk),
        out_shape=jax.ShapeDtypeStruct((bsz, topk, LANES), F32),
        scratch_shapes=[pltpu.VMEM((bsz, n), F32)],
        compiler_params=pltpu.CompilerParams(vmem_limit_bytes=VMEM_LIMIT),
        name="sample_select",
    )(scores, pos)


def _sample_attend_kernel(idx_ref, pt_ref, q_ref, knew_ref, vnew_ref, ck_ref, cv_ref, o_ref, kbuf, vbuf, ksem, vsem,
                          *, topk, past, n_batch):
    b = pl.program_id(0)

    def row_copies(bb, slot, r):
        i = jnp.minimum(idx_ref[bb, r], past - 1)
        phys = pt_ref[bb, i // PAGE_SIZE]
        off = i % PAGE_SIZE
        return (pltpu.make_async_copy(ck_ref.at[phys, off], kbuf.at[slot, r], ksem.at[slot]),
                pltpu.make_async_copy(cv_ref.at[phys, off], vbuf.at[slot, r], vsem.at[slot]))

    def start_all(bb, slot):
        def body(r, c):
            for cp in row_copies(bb, slot, r):
                cp.start()
            return c

        lax.fori_loop(0, topk, body, 0, unroll=8)

    @pl.when(b == 0)
    def _():
        start_all(0, 0)

    @pl.when(b + 1 < n_batch)
    def _():
        start_all(b + 1, (b + 1) % 2)

    slot = b % 2

    def wait_body(r, c):
        for cp in row_copies(b, slot, r):
            cp.wait()
        return c

    lax.fori_loop(0, topk, wait_body, 0, unroll=8)

    @pl.when(idx_ref[b, topk - 1] == past)
    def _():
        kbuf[slot, topk - 1] = knew_ref[0]
        vbuf[slot, topk - 1] = vnew_ref[0]

    q = q_ref[0]
    k = kbuf[slot]
    logits = jnp.sum(k * q[None], axis=-1, keepdims=True) * (HEAD_DIM ** -0.5)
    m = jnp.max(logits, axis=0, keepdims=True)
    p = jnp.exp(logits - m)
    den = jnp.sum(p, axis=0)
    o_ref[0] = jnp.sum(p * vbuf[slot], axis=0) / den


def sample_attend(idx, page_table, q, k_new, v_new, cache_k, cache_v):
    bsz, h, dh = q.shape
    topk = idx.shape[1]
    past = page_table.shape[1] * PAGE_SIZE
    grid_spec = pltpu.PrefetchScalarGridSpec(
        num_scalar_prefetch=2,
        grid=(bsz,),
        in_specs=[pl.BlockSpec((1, h, dh), lambda i, a, c: (i, 0, 0)), pl.BlockSpec((1, h, dh), lambda i, a, c: (i, 0, 0)),
                  pl.BlockSpec((1, h, dh), lambda i, a, c: (i, 0, 0)),
                  pl.BlockSpec(memory_space=pl.ANY), pl.BlockSpec(memory_space=pl.ANY)],
        out_specs=pl.BlockSpec((1, h, dh), lambda i, a, c: (i, 0, 0)),
        scratch_shapes=[pltpu.VMEM((2, topk, h, dh), F32), pltpu.VMEM((2, topk, h, dh), F32),
                        pltpu.SemaphoreType.DMA((2,)), pltpu.SemaphoreType.DMA((2,))],
    )
    return pl.pallas_call(
        functools.partial(_sample_attend_kernel, topk=topk, past=past, n_batch=bsz),
        grid_spec=grid_spec,
        out_shape=jax.ShapeDtypeStruct((bsz, h, dh), F32),
        compiler_params=_cparams(("arbitrary",)),
        name="sample_attend",
    )(idx, page_table, q, k_new, v_new, cache_k, cache_v)


def _merge_kernel(h_ref, s_ref, a_ref, x_ref, wgs_ref, wga_ref, wbs_ref, wba_ref, wo_ref, gn_ref, o_ref, hn_ref,
                  *, n_tiles):
    j = pl.program_id(1)
    dot = functools.partial(jnp.dot, preferred_element_type=F32)
    h = h_ref[...]
    merged = (jax.nn.sigmoid(dot(h, wgs_ref[0])) * dot(s_ref[...], wbs_ref[0])
              + jax.nn.sigmoid(dot(h, wga_ref[0])) * dot(a_ref[...], wba_ref[0]))
    part = dot(merged.astype(BF16), wo_ref[...])

    @pl.when(j == 0)
    def _():
        o_ref[...] = x_ref[...] + part

    @pl.when(j > 0)
    def _():
        o_ref[...] += part

    @pl.when(j == n_tiles - 1)
    def _():
        x1 = o_ref[...]
        ms = jnp.mean(x1 * x1, axis=-1, keepdims=True)
        hn_ref[...] = (x1 * lax.rsqrt(ms + RMS_EPS) * gn_ref[...]).astype(hn_ref.dtype)


def merge_branches(h, ssm_out, att, x2d, w_gates, w_bs, w_ba, w_out, g_next, tm, tn):
    m, d = x2d.shape
    kdim = ssm_out.shape[1]
    nj = d // tn
    row = lambda c: pl.BlockSpec((tm, c), lambda i, j: (i, 0))
    return pl.pallas_call(
        functools.partial(_merge_kernel, n_tiles=nj),
        grid=(m // tm, nj),
        in_specs=[row(d), row(kdim), row(kdim), row(d),
                  pl.BlockSpec((1, d, tn), lambda i, j: (j, 0, 0)), pl.BlockSpec((1, d, tn), lambda i, j: (j + nj, 0, 0)),
                  pl.BlockSpec((1, kdim, tn), lambda i, j: (j, 0, 0)), pl.BlockSpec((1, kdim, tn), lambda i, j: (j, 0, 0)),
                  pl.BlockSpec((tn, d), lambda i, j: (j, 0)), pl.BlockSpec((1, d), lambda i, j: (0, 0))],
        out_specs=[row(d), row(d)],
        out_shape=[jax.ShapeDtypeStruct((m, d), F32), jax.ShapeDtypeStruct((m, d), BF16)],
        compiler_params=_cparams(("parallel", "arbitrary")),
        name="merge_branches",
    )(h, ssm_out, att, x2d, w_gates, w_gates, w_bs, w_ba, w_out, g_next.reshape(1, d).astype(F32))


def _ffn_step_kernel(x_ref, gf_ref, gfin_ref, wa_ref, wb_ref, cwa_ref, cwb_ref, cba_ref, cbb_ref, wd_ref,
                     p2a_ref, p1a_ref, p2b_ref, p1b_ref, y_ref, ca_ref, cb_ref, h_scr, acc_scr, *, n_ff_tiles):
    j = pl.program_id(1)

    @pl.when(j == 0)
    def _():
        x = x_ref[0]
        ms = jnp.mean(x * x, axis=-1, keepdims=True)
        h_scr[...] = (x * lax.rsqrt(ms + RMS_EPS) * gf_ref[...]).astype(BF16)
        acc_scr[...] = jnp.zeros_like(acc_scr)

    h = h_scr[...]

    def conv_half(w_ref, cw_ref, cbias_ref, out_ref, p2_ref, p1_ref):
        up = jnp.dot(h, w_ref[0], preferred_element_type=F32)
        cw = cw_ref[...]
        out_ref[...] = up
        return cbias_ref[...] + cw[0:1, :] * p2_ref[...] + cw[1:2, :] * p1_ref[...] + cw[2:3, :] * up

    a = conv_half(wa_ref, cwa_ref, cba_ref, ca_ref, p2a_ref, p1a_ref)
    g = conv_half(wb_ref, cwb_ref, cbb_ref, cb_ref, p2b_ref, p1b_ref)
    act = (jax.nn.silu(a) * g).astype(BF16)
    acc_scr[...] += jnp.dot(act, wd_ref[...], preferred_element_type=F32)

    @pl.when(j == n_ff_tiles - 1)
    def _():
        x2 = x_ref[0] + acc_scr[...]
        ms = jnp.mean(x2 * x2, axis=-1, keepdims=True)
        y_ref[0] = x2 * lax.rsqrt(ms + RMS_EPS) * gfin_ref[...]


def conv_ffn_step(x, conv_prev, g_ffn, g_final, w_up, conv_w, conv_b, w_down, tm):
    b, t, d = x.shape
    tf = w_up.shape[2]
    f = w_down.shape[0]
    nf = f // tf
    gf = g_ffn.reshape(1, d).astype(F32)
    gfin = g_final.reshape(1, d).astype(F32)
    cb2 = conv_b.reshape(1, 2 * f).astype(F32)
    conv_w = conv_w.astype(F32)
    kern = functools.partial(_ffn_step_kernel, n_ff_tiles=nf)
    p2, p1 = conv_prev[:, 0, :], conv_prev[:, 1, :]
    wspec = lambda blk, off: pl.BlockSpec(blk, lambda i, j: (0, j + off))
    upspec = lambda off: pl.BlockSpec((1, d, tf), lambda i, j: (j + off, 0, 0))
    in_specs = [pl.BlockSpec((1, tm, d), lambda i, j: (0, i, 0)),
                pl.BlockSpec((1, d), lambda i, j: (0, 0)), pl.BlockSpec((1, d), lambda i, j: (0, 0)),
                upspec(0), upspec(nf), wspec((CONV_W, tf), 0), wspec((CONV_W, tf), nf),
                wspec((1, tf), 0), wspec((1, tf), nf),
                pl.BlockSpec((tf, d), lambda i, j: (j, 0)),
                pl.BlockSpec((tm, tf), lambda i, j: (i, j)), pl.BlockSpec((tm, tf), lambda i, j: (i, j)),
                pl.BlockSpec((tm, tf), lambda i, j: (i, j + nf)), pl.BlockSpec((tm, tf), lambda i, j: (i, j + nf))]
    out_specs = [pl.BlockSpec((1, tm, d), lambda i, j: (0, i, 0)),
                 pl.BlockSpec((tm, tf), lambda i, j: (i, j)), pl.BlockSpec((tm, tf), lambda i, j: (i, j))]
    out_shape = [jax.ShapeDtypeStruct((1, t, d), F32), jax.ShapeDtypeStruct((t, f), F32),
                 jax.ShapeDtypeStruct((t, f), F32)]
    scratch = [pltpu.VMEM((tm, d), BF16), pltpu.VMEM((tm, d), F32)]
    y, ua, ub = pl.pallas_call(
        kern, grid=(t // tm, nf), in_specs=in_specs, out_specs=out_specs, out_shape=out_shape,
        scratch_shapes=scratch, compiler_params=_cparams(("arbitrary", "arbitrary")),
        name="conv_ffn_step",
    )(x, gf, gfin, w_up, w_up, conv_w, conv_w, cb2, cb2, w_down, p2, p1, p2, p1)
    return y, jnp.stack([p1, jnp.concatenate([ua, ub], axis=-1)], axis=1)


def _ffn_seq_kernel(h_ref, xt_ref, gfin_ref, wa_ref, wb_ref, cwa_ref, cwb_ref, cba_ref, cbb_ref, wd_ref, pa_ref, pb_ref,
                    y_ref, ca_ref, cb_ref, act_scr, x2_scr, car_a, car_b, *, tm, tf, nf, tn, nd):
    i = pl.program_id(1)
    j = pl.program_id(2)

    @pl.when(j < nf)
    def _():
        sub = min(tm, FFN_SUB_ROWS)
        row = lax.broadcasted_iota(jnp.int32, (sub, tf), 0)

        def conv_half(w_ref, cw_ref, cbias_ref, out_ref, prev_ref, car):
            cw = cw_ref[...]
            cbias = cbias_ref[...]

            @pl.when(i == 0)
            def _():
                car[j] = prev_ref[0]

            c2 = car[j]
            rm2, rm1 = c2[0:1, :], c2[1:2, :]
            outs = []
            for r0 in range(0, tm, sub):
                up = jnp.dot(h_ref[0, r0:r0 + sub, :], w_ref[0], preferred_element_type=F32)
                p1 = jnp.where(row == 0, rm1, pltpu.roll(up, 1, 0))
                p2 = jnp.where(row == 0, rm2, jnp.where(row == 1, rm1, pltpu.roll(up, 2, 0)))
                outs.append(cbias + cw[0:1, :] * p2 + cw[1:2, :] * p1 + cw[2:3, :] * up)
                rm2, rm1 = up[sub - 2:sub - 1, :], up[sub - 1:sub, :]
            last2 = jnp.concatenate([rm2, rm1], axis=0)
            car[j] = last2
            out_ref[0, 0] = last2
            return outs

        a = conv_half(wa_ref, cwa_ref, cba_ref, ca_ref, pa_ref, car_a)
        g = conv_half(wb_ref, cwb_ref, cbb_ref, cb_ref, pb_ref, car_b)
        act = [(jax.nn.silu(ar) * gr).astype(BF16) for ar, gr in zip(a, g)]
        for k in range(nf):

            @pl.when(j == k)
            def _(k=k):
                for r, act_r in enumerate(act):
                    act_scr[r * sub:(r + 1) * sub, k * tf:(k + 1) * tf] = act_r

    @pl.when(j >= nf)
    def _():
        part = xt_ref[0] + jnp.dot(act_scr[...], wd_ref[0], preferred_element_type=F32)
        for n in range(nd):

            @pl.when(j == nf + n)
            def _(n=n):
                x2_scr[:, n * tn:(n + 1) * tn] = part

    @pl.when(j == nf + nd - 1)
    def _():
        x2 = x2_scr[...]
        ms = jnp.mean(x2 * x2, axis=-1, keepdims=True)
        y_ref[0] = x2 * lax.rsqrt(ms + RMS_EPS) * gfin_ref[...]


def conv_ffn_sequence(h, x, conv_prev, g_final, w_up, conv_w, conv_b, w_down, tm):
    b, t, d = x.shape
    tf = w_up.shape[2]
    nd, f, tn = w_down.shape
    nf, nt = f // tf, t // tm
    gfin = g_final.reshape(1, d).astype(F32)
    cb2 = conv_b.reshape(1, 2 * f).astype(F32)
    conv_w = conv_w.astype(F32)
    up_tile = lambda j: jnp.minimum(j, nf - 1)
    down_tile = lambda j: jnp.clip(j - nf, 0, nd - 1)
    wspec = lambda blk, off: pl.BlockSpec(blk, lambda bi, i, j: (0, up_tile(j) + off))
    upspec = lambda off: pl.BlockSpec((1, d, tf), lambda bi, i, j: (up_tile(j) + off, 0, 0))
    prev_spec = lambda off: pl.BlockSpec((1, CONV_W - 1, tf), lambda bi, i, j: (bi, 0, up_tile(j) + off))
    state_spec = pl.BlockSpec((1, 1, CONV_W - 1, tf), lambda bi, i, j: (bi, i, 0, up_tile(j)))
    y, ca, cb = pl.pallas_call(
        functools.partial(_ffn_seq_kernel, tm=tm, tf=tf, nf=nf, tn=tn, nd=nd),
        grid=(b, nt, nf + nd),
        in_specs=[pl.BlockSpec((1, tm, d), lambda bi, i, j: (bi, i, 0)),
                  pl.BlockSpec((1, tm, tn), lambda bi, i, j: (bi, i, down_tile(j))),
                  pl.BlockSpec((1, d), lambda bi, i, j: (0, 0)),
                  upspec(0), upspec(nf), wspec((CONV_W, tf), 0), wspec((CONV_W, tf), nf),
                  wspec((1, tf), 0), wspec((1, tf), nf),
                  pl.BlockSpec((1, f, tn), lambda bi, i, j: (down_tile(j), 0, 0)),
                  prev_spec(0), prev_spec(nf)],
        out_specs=[pl.BlockSpec((1, tm, d), lambda bi, i, j: (bi, i, 0)), state_spec, state_spec],
        out_shape=[jax.ShapeDtypeStruct((b, t, d), F32), jax.ShapeDtypeStruct((b, nt, CONV_W - 1, f), F32),
                   jax.ShapeDtypeStruct((b, nt, CONV_W - 1, f), F32)],
        scratch_shapes=[pltpu.VMEM((tm, f), BF16), pltpu.VMEM((tm, d), F32),
                        pltpu.VMEM((nf, CONV_W - 1, tf), F32), pltpu.VMEM((nf, CONV_W - 1, tf), F32)],
        compiler_params=_cparams(("arbitrary", "arbitrary", "arbitrary")),
        name="conv_ffn_seq",
    )(h, x, gfin, w_up, w_up, conv_w, conv_w, cb2, cb2, w_down, conv_prev, conv_prev)
    return y, jnp.concatenate([ca[:, -1], cb[:, -1]], axis=-1)


def column_tiles(w, tn):
    k, n = w.shape
    return w.astype(BF16).reshape(k, n // tn, tn).transpose(1, 0, 2)


def _split_w_in(w_in, d_model, n_ssm, n_att):
    sizes = [n_ssm, n_att, n_att, n_att, H_IDX * D_IDX, D_IDX, H_IDX, d_model, d_model]
    offs = np.concatenate([[0], np.cumsum(sizes)]).tolist()
    col = lambda i: w_in[:, offs[i]:offs[i + 1]]
    w_uv = column_tiles(jnp.concatenate([col(0), col(3)], axis=1), COL_TILE)
    w_qk = column_tiles(jnp.concatenate([col(1), col(2)], axis=1), COL_TILE)
    w_qi = column_tiles(col(4), COL_TILE)
    pad = jnp.zeros((w_in.shape[0], LANES - D_IDX - H_IDX), w_in.dtype)
    w_kw = column_tiles(jnp.concatenate([col(5), col(6), pad], axis=1), LANES)
    w_g = column_tiles(jnp.concatenate([col(7), col(8)], axis=1), COL_TILE)
    return w_uv, w_qk, w_qi, w_kw, w_g


def _mix_inputs(x2d, pos, n_pos_tiles, tm, norm_mix, w_groups, n_ssm, n_att, sequence):
    w_uv, w_qk, w_qi, w_kw, w_g = w_groups
    h = rmsnorm_bf16(x2d, norm_mix, tm)
    tn = COL_TILE
    nu, na = n_ssm // tn, n_att // tn
    half_qk = HEAD_DIM // ROPE_FRACTION // 2
    half_idx = D_IDX // ROPE_FRACTION // 2
    out = {}
    if sequence:
        out["u"], out["v"], out["v16"] = project(
            h, w_uv, tm, tn, [(0, nu, F32, "flat"), (nu, na, F32, "flat"), (nu, na, BF16, "flat")])
        out["q16"], out["k"], out["k16"] = project(
            h, w_qk, tm, tn, [(0, na, BF16, "flat"), (na, na, F32, "flat"), (na, na, BF16, "flat")],
            "rope", half_qk, rope_tables(pos, HEAD_DIM, tn), n_pos_tiles)
        out["qi16"], = project(h, w_qi, tm, tn, [(0, H_IDX * D_IDX // tn, BF16, "heads")],
                               "rope", half_idx, rope_tables(pos, D_IDX, tn), n_pos_tiles)
    else:
        out["u"], out["v"] = project(h, w_uv, tm, tn, [(0, nu, F32, "flat"), (nu, na, F32, "flat")])
        out["q"], out["k"] = project(h, w_qk, tm, tn, [(0, na, F32, "flat"), (na, na, F32, "flat")],
                                     "rope", half_qk, rope_tables(pos, HEAD_DIM, tn), n_pos_tiles)
        out["qi"], = project(h, w_qi, tm, tn, [(0, H_IDX * D_IDX // tn, F32, "flat")],
                             "rope", half_idx, rope_tables(pos, D_IDX, tn), n_pos_tiles)
    extra = jnp.concatenate([jnp.full((H_IDX,), H_IDX ** -0.5, F32), jnp.zeros((LANES - D_IDX - H_IDX,), F32)])
    kw, = project(h, w_kw, tm, LANES, [(0, 1, F32, "flat")], "rope", half_idx,
                  rope_tables(pos, D_IDX, LANES, extra), n_pos_tiles)
    out["ki"], out["wi"] = kw[:, :D_IDX], kw[:, D_IDX:D_IDX + H_IDX]
    out["h"] = h
    return out


def kernel(x_prompt, x_sample, cache_k, cache_v, cache_idx_k, page_table, state_ssm_re, state_ssm_im,
           state_ffn_conv, norm_mix, w_in, ssm_A_re, ssm_A_im, ssm_log_dt, ssm_B_re, ssm_B_im, ssm_C_re,
           ssm_C_im, ssm_D, w_glu, w_branch_ssm, w_branch_att, w_out, norm_ffn, w_up, ffn_conv_w,
           ffn_conv_b, w_down, norm_final):
    bp, t, d = x_prompt.shape
    bs, ts, _ = x_sample.shape
    n_ssm = ssm_D.shape[0]
    n_att = w_branch_att.shape[0]
    g = n_ssm // SSM_GROUP
    h_att = n_att // HEAD_DIM
    past = page_table.shape[1] * PAGE_SIZE
    assert ts == 1, "sample group is a single decode step"
    assert bp == SUBLANES, "the S5 sequence kernel keeps one batch row per sublane"

    w_groups = _split_w_in(w_in, d, n_ssm, n_att)
    w_glu_b, w_out_b, w_down_b = (w.astype(BF16) for w in (w_glu, w_out, w_down))
    w_bs, w_ba = column_tiles(w_branch_ssm, COL_TILE), column_tiles(w_branch_att, COL_TILE)
    w_up_t, w_down_t = column_tiles(w_up, COL_TILE), column_tiles(w_down, COL_TILE)
    ssm_params = (ssm_A_re, ssm_A_im, ssm_log_dt, ssm_B_re, ssm_B_im, ssm_C_re, ssm_C_im)

    mp = bp * t
    tm = 1024
    xp2 = x_prompt.reshape(mp, d)
    pos_p = jnp.arange(t, dtype=jnp.int32)
    pr = _mix_inputs(xp2, pos_p, t // tm, tm, norm_mix, w_groups, n_ssm, n_att, True)
    zeros_s = jnp.zeros((bp, g, SSM_STATE), F32)
    y_act, s_re_p, s_im_p = s5_sequence(pr["u"].reshape(bp, t, n_ssm), zeros_s, zeros_s, ssm_D,
                                        ssm_tile_matrices(*ssm_params, SSM_CHUNK, BF16), SSM_CHUNK, 512)
    ssm_out = ssm_glu(y_act.reshape(mp, n_ssm), w_glu_b, 512)
    att = prompt_attention(pr["qi16"], pr["ki"].astype(BF16).reshape(bp, t, D_IDX), pr["wi"].reshape(bp, t, H_IDX),
                           pr["q16"].reshape(bp, t, n_att), pr["k16"].reshape(bp, t, n_att),
                           pr["v16"].reshape(bp, t, n_att), 256, 4)
    x1, h_ffn = merge_branches(pr["h"], ssm_out, att.reshape(mp, n_att), xp2, w_groups[4], w_bs, w_ba, w_out_b,
                               norm_ffn, 512, 512)
    zeros_c = jnp.zeros((bp, CONV_W - 1, w_up.shape[1]), F32)
    y_prompt, conv_p = conv_ffn_sequence(h_ffn.reshape(bp, t, d), x1.reshape(bp, t, d), zeros_c, norm_final, w_up_t,
                                         ffn_conv_w, ffn_conv_b, w_down_t, 512)
    k_p = pr["k"].reshape(bp, t, h_att, HEAD_DIM)
    v_p = pr["v"].reshape(bp, t, h_att, HEAD_DIM)
    ki_p = pr["ki"].reshape(bp, t, D_IDX)

    xs2 = x_sample.reshape(bs, d)
    pos_s = jnp.full((bs,), past, jnp.int32)
    sm = _mix_inputs(xs2, pos_s, 1, bs, norm_mix, w_groups, n_ssm, n_att, False)
    y_act, s_re_s, s_im_s = s5_step(sm["u"], state_ssm_re, state_ssm_im, ssm_D, ssm_tile_matrices(*ssm_params, 1, F32))
    ssm_out = ssm_glu(y_act, w_glu_b, bs)
    scores = sample_scores(page_table, sm["qi"].astype(BF16).reshape(bs, H_IDX, D_IDX), sm["wi"].reshape(bs, H_IDX, 1),
                           sm["ki"].reshape(bs, 1, D_IDX), cache_idx_k)
    topk = min(TOPK_MAX, (past + 1) // 4)
    hl = sample_select(scores.reshape(bs, past + LANES), topk)
    idx = (hl[:, :, 0] * PAGE_SIZE + hl[:, :, 1]).astype(jnp.int32)
    att = sample_attend(idx, page_table, sm["q"].reshape(bs, h_att, HEAD_DIM), sm["k"].reshape(bs, h_att, HEAD_DIM),
                        sm["v"].reshape(bs, h_att, HEAD_DIM), cache_k, cache_v)
    x1, _ = merge_branches(sm["h"], ssm_out, att.reshape(bs, n_att).astype(BF16), xs2, w_groups[4], w_bs, w_ba, w_out_b,
                           norm_ffn, bs, 512)
    y_s, conv_s = conv_ffn_step(x1.reshape(1, bs, d), state_ffn_conv, norm_ffn, norm_final, w_up_t, ffn_conv_w,
                                ffn_conv_b, w_down_b, bs)
    y_sample = y_s.reshape(bs, 1, d)
    k_s = sm["k"].reshape(bs, 1, h_att, HEAD_DIM)
    v_s = sm["v"].reshape(bs, 1, h_att, HEAD_DIM)
    ki_s = sm["ki"].reshape(bs, 1, D_IDX)

    return (y_prompt, y_sample, k_p, v_p, ki_p, k_s, v_s, ki_s, s_re_p, s_im_p, s_re_s, s_im_s, conv_p, conv_s)
```

```python
import functools

import numpy as np
import jax
import jax.numpy as jnp
from jax import lax
from jax.experimental import pallas as pl
from jax.experimental.pallas import tpu as pltpu

F32 = jnp.float32
BF16 = jnp.bfloat16

HEAD_DIM = 128
SSM_GROUP = 16
SSM_STATE = 64
H_IDX = 16
D_IDX = 64
TOPK_MAX = 256
ROPE_THETA = 500000.0
ROPE_FRACTION = 4
CONV_W = 3
RMS_EPS = 1e-6
NEG_INF = -1e30
PAGE_SIZE = 128

LANES = 128
SUBLANES = 8
MXU_WIDTH = 256
GROUPS_PER_TILE = LANES // SSM_GROUP
SSM_CHUNK = 8
PROJ_SUB_ROWS = 256
FFN_SUB_ROWS = 256
COL_TILE = 512
MERGE_SUB_ROWS = 256
VMEM_LIMIT = 56 * 1024 * 1024
INT_MIN = -2 ** 31
NT_DIMS = (((1,), (1,)), ((), ()))


def _cparams(sem):
    return pltpu.CompilerParams(dimension_semantics=sem, vmem_limit_bytes=VMEM_LIMIT)


def _rmsnorm_kernel(x_ref, g_ref, o_ref):
    x = x_ref[...]
    ms = jnp.mean(x * x, axis=-1, keepdims=True)
    o_ref[...] = (x * lax.rsqrt(ms + RMS_EPS) * g_ref[...]).astype(o_ref.dtype)


def rmsnorm_bf16(x2d, g, tm):
    m, d = x2d.shape
    return pl.pallas_call(
        _rmsnorm_kernel,
        grid=(m // tm,),
        in_specs=[pl.BlockSpec((tm, d), lambda i: (i, 0)), pl.BlockSpec((1, d), lambda i: (0, 0))],
        out_specs=pl.BlockSpec((tm, d), lambda i: (i, 0)),
        out_shape=jax.ShapeDtypeStruct((m, d), BF16),
        compiler_params=_cparams(("parallel",)),
        name="rmsnorm",
    )(x2d, g.reshape(1, d))


def _proj_kernel(h_ref, w_ref, *rest, mode, shift, outs, n_tab, n_col_tiles):
    tabs, o_refs = rest[:n_tab], rest[n_tab:]
    j = pl.program_id(1)
    tm, tn = h_ref.shape[0], w_ref.shape[2]
    sub = min(tm, PROJ_SUB_ROWS)
    for r0 in range(0, tm, sub):
        rows = slice(r0, r0 + sub)
        z = jnp.dot(h_ref[rows, :], w_ref[0], preferred_element_type=F32)
        if mode == "rope":
            c_ref, s1_ref, s2_ref = tabs
            z = (z * c_ref[rows, :] + pltpu.roll(z, tn - shift, 1) * s1_ref[rows, :]
                 + pltpu.roll(z, shift, 1) * s2_ref[rows, :])
        for (start, cnt, _, kind), o_ref in zip(outs, o_refs):
            if kind == "heads":
                assert start + cnt == n_col_tiles
                for hh in range(tn // D_IDX):
                    o_ref[0, hh, rows, :] = z[:, hh * D_IDX:(hh + 1) * D_IDX].astype(o_ref.dtype)
            elif start + cnt == n_col_tiles:
                o_ref[rows, :] = z.astype(o_ref.dtype)
            else:
                o_ref[rows, :] = jnp.where(j >= start + cnt, o_ref[rows, :], z.astype(o_ref.dtype))


def project(h, w, tm, tn, outs, mode="plain", shift=0, tables=None, n_pos_tiles=1):
    m, k = h.shape
    n = w.shape[0] * tn
    in_specs = [pl.BlockSpec((tm, k), lambda i, j: (i, 0)), pl.BlockSpec((1, k, tn), lambda i, j: (j, 0, 0))]
    args = [h, w]
    tables = tables or ()
    for t in tables:
        in_specs.append(pl.BlockSpec((tm, tn), lambda i, j: (i % n_pos_tiles, 0)))
        args.append(t)
    out_specs, out_shape = [], []
    for start, cnt, dtype, kind in outs:
        if kind == "heads":
            hpt = tn // D_IDX
            out_specs.append(pl.BlockSpec(
                (1, hpt, tm, D_IDX),
                lambda i, j, s=start, c=cnt: (i // n_pos_tiles, jnp.clip(j - s, 0, c - 1), i % n_pos_tiles, 0)))
            out_shape.append(jax.ShapeDtypeStruct((m // (n_pos_tiles * tm), cnt * hpt, n_pos_tiles * tm, D_IDX), dtype))
        else:
            out_specs.append(pl.BlockSpec((tm, tn), lambda i, j, s=start, c=cnt: (i, jnp.clip(j - s, 0, c - 1))))
            out_shape.append(jax.ShapeDtypeStruct((m, cnt * tn), dtype))
    return pl.pallas_call(
        functools.partial(_proj_kernel, mode=mode, shift=shift, outs=tuple(outs), n_tab=len(tables),
                          n_col_tiles=n // tn),
        grid=(m // tm, n // tn),
        in_specs=in_specs,
        out_specs=out_specs,
        out_shape=out_shape,
        compiler_params=_cparams(("parallel", "arbitrary")),
        name="proj_" + mode,
    )(*args)


def rope_tables(pos, head, tn, extra=None):
    r = head // ROPE_FRACTION
    half = r // 2
    inv = ROPE_THETA ** (-jnp.arange(half, dtype=F32) * 2.0 / r)
    ang = pos.astype(F32)[:, None] * inv[None, :]
    cos, sin = jnp.cos(ang), jnp.sin(ang)
    t = pos.shape[0]
    zeros = jnp.zeros((t, head - r), F32)
    zh = jnp.zeros((t, half), F32)
    c = jnp.concatenate([cos, cos, jnp.ones((t, head - r), F32)], axis=1)
    s1 = jnp.concatenate([-sin, zh, zeros], axis=1)
    s2 = jnp.concatenate([zh, sin, zeros], axis=1)
    if extra is None:
        reps = tn // head
        return tuple(jnp.tile(a, (1, reps)) for a in (c, s1, s2))
    pad = jnp.zeros((t, tn - head), F32)
    return (jnp.concatenate([c, jnp.broadcast_to(extra[None, :], (t, tn - head))], axis=1),
            jnp.concatenate([s1, pad], axis=1), jnp.concatenate([s2, pad], axis=1))


def ssm_tile_matrices(a_re, a_im, log_dt, b_re, b_im, c_re, c_im, chunk, dtype):
    hp = lax.Precision.HIGHEST
    g, p = a_re.shape
    gt = GROUPS_PER_TILE
    nq = g // gt
    a = lax.complex(a_re.astype(F32), a_im.astype(F32))
    dt = jnp.exp(log_dt.astype(F32))[:, None]
    adt = a * dt
    a_bar = jnp.exp(adt)
    b_bar = ((a_bar - 1.0) / a)[..., None] * lax.complex(b_re.astype(F32), b_im.astype(F32))
    cc = lax.complex(c_re.astype(F32), c_im.astype(F32))
    steps = jnp.arange(chunk + 1, dtype=F32)
    pw = jnp.exp(adt[:, None, :] * steps[None, :, None].astype(jnp.complex64))
    kd = jnp.real(jnp.einsum("gcp,gdp,gpe->gdce", cc, pw[:, :chunk], b_bar, precision=hp))
    w1g = pw[:, chunk - 1 - jnp.arange(chunk)][:, :, None, :] * b_bar.transpose(0, 2, 1)[:, None]
    m2g = cc.transpose(0, 2, 1)[:, :, None, :] * pw[:, 1:chunk + 1].transpose(0, 2, 1)[:, :, :, None]
    kdc = kd.reshape(nq, gt, chunk, SSM_GROUP, SSM_GROUP).transpose(0, 2, 4, 1, 3).reshape(nq, chunk, SSM_GROUP, LANES)
    w1c = w1g.reshape(nq, gt, chunk, SSM_GROUP, p).transpose(0, 2, 3, 1, 4).reshape(nq, chunk, SSM_GROUP, gt * p)
    m2c = m2g.reshape(nq, gt, p, chunk, SSM_GROUP).transpose(0, 3, 2, 1, 4).reshape(nq, chunk, p, LANES)
    al = pw[:, chunk].reshape(nq, 1, gt * p)
    kt, w1r, w1i, w2r, w2i = _ssm_expand(kdc, jnp.real(w1c), jnp.imag(w1c), jnp.real(m2c), -jnp.imag(m2c), chunk, dtype)
    return dict(w1r=w1r, w1i=w1i, kt=kt, w2r=w2r, w2i=w2i, ar=jnp.real(al), ai=jnp.imag(al))


def _ssm_expand_kernel(kd_ref, w1r_ref, w1i_ref, w2r_ref, w2i_ref, kt_o, w1r_o, w1i_o, w2r_o, w2i_o, *, chunk):
    gt, p = GROUPS_PER_TILE, SSM_STATE

    def same_group(rows_per_group, cols_per_group):
        shape = (gt * rows_per_group, gt * cols_per_group)
        r = jnp.right_shift(lax.broadcasted_iota(jnp.int32, shape, 0), rows_per_group.bit_length() - 1)
        c = jnp.right_shift(lax.broadcasted_iota(jnp.int32, shape, 1), cols_per_group.bit_length() - 1)
        return r == c

    def block_diag(small, mask, dtype):
        return jnp.where(mask, jnp.tile(small, (gt, 1)), 0.0).astype(dtype)

    m_cc, m_cp, m_pc = same_group(SSM_GROUP, SSM_GROUP), same_group(SSM_GROUP, p), same_group(p, SSM_GROUP)
    for j in range(chunk):
        rows = slice(j * LANES, (j + 1) * LANES)
        w1r_o[0, rows, :] = block_diag(w1r_ref[0, j], m_cp, w1r_o.dtype)
        w1i_o[0, rows, :] = block_diag(w1i_ref[0, j], m_cp, w1i_o.dtype)
        w2r_o[0, :, rows] = block_diag(w2r_ref[0, j], m_pc, w2r_o.dtype)
        w2i_o[0, :, rows] = block_diag(w2i_ref[0, j], m_pc, w2i_o.dtype)
        for t in range(chunk):
            cols = slice(t * LANES, (t + 1) * LANES)
            if t >= j:
                kt_o[0, rows, cols] = block_diag(kd_ref[0, t - j], m_cc, kt_o.dtype)
            else:
                kt_o[0, rows, cols] = jnp.zeros((LANES, LANES), kt_o.dtype)


def _ssm_expand(kdc, w1r, w1i, w2r, w2i, chunk, dtype):
    nq = kdc.shape[0]
    ns = w1r.shape[-1]
    cw = chunk * LANES
    spec4 = lambda a: pl.BlockSpec((1,) + a.shape[1:], lambda q: (q, 0, 0, 0))
    spec3 = lambda r, c: pl.BlockSpec((1, r, c), lambda q: (q, 0, 0))
    shapes = [(cw, cw), (cw, ns), (cw, ns), (ns, cw), (ns, cw)]
    return pl.pallas_call(
        functools.partial(_ssm_expand_kernel, chunk=chunk),
        grid=(nq,),
        in_specs=[spec4(a) for a in (kdc, w1r, w1i, w2r, w2i)],
        out_specs=[spec3(r, c) for r, c in shapes],
        out_shape=[jax.ShapeDtypeStruct((nq, r, c), dtype) for r, c in shapes],
        compiler_params=_cparams(("parallel",)),
        name="ssm_expand",
    )(kdc, w1r, w1i, w2r, w2i)


def _s5_seq_kernel(u_ref, w1r_ref, w1i_ref, kt_ref, w2r_ref, w2i_ref, ar_ref, ai_ref, d_ref, s0r_ref, s0i_ref,
                   o_ref, sfr_ref, sfi_ref, ucat, slr, sli, spr, spi, yc, yb, st_r, st_i, *, chunk, n_rows, nb):
    ts = pl.program_id(1)
    ns = st_r.shape[-1]

    @pl.when(ts == 0)
    def _():
        st_r[...] = s0r_ref[...]
        st_i[...] = s0i_ref[...]

    for b in range(nb):
        for tl in range(chunk):
            ucat[:, b, tl * LANES:(tl + 1) * LANES] = u_ref[b, pl.ds(tl, n_rows, stride=chunk), :]
    x = ucat[...].reshape(n_rows * nb, chunk * LANES).astype(BF16)
    slr[...] = jnp.dot(x, w1r_ref[0], preferred_element_type=F32).reshape(n_rows, nb, ns)
    sli[...] = jnp.dot(x, w1i_ref[0], preferred_element_type=F32).reshape(n_rows, nb, ns)
    ar = jnp.broadcast_to(ar_ref[0], (nb, ns))
    ai = jnp.broadcast_to(ai_ref[0], (nb, ns))

    def step(n, carry):
        sr, si = carry
        spr[n] = sr
        spi[n] = si
        return ar * sr - ai * si + slr[n], ar * si + ai * sr + sli[n]

    sr, si = lax.fori_loop(0, n_rows, step, (st_r[...], st_i[...]))
    st_r[...] = sr
    st_i[...] = si
    sfr_ref[...] = sr
    sfi_ref[...] = si
    y = (jnp.dot(x, kt_ref[0], preferred_element_type=F32)
         + jnp.dot(spr[...].reshape(n_rows * nb, ns).astype(BF16), w2r_ref[0], preferred_element_type=F32)
         + jnp.dot(spi[...].reshape(n_rows * nb, ns).astype(BF16), w2i_ref[0], preferred_element_type=F32))
    yc[...] = y.reshape(n_rows, nb, chunk * LANES)
    for b in range(nb):
        for tl in range(chunk):
            yb[b, pl.ds(tl, n_rows, stride=chunk), :] = yc[:, b, tl * LANES:(tl + 1) * LANES]
    o_ref[...] = jax.nn.gelu(yb[...] + d_ref[...] * u_ref[...]).astype(o_ref.dtype)


def s5_sequence(u, s0_re, s0_im, d_skip, mats, chunk, t_seg):
    b, t, n = u.shape
    g, p = s0_re.shape[1:]
    nq = n // LANES
    ns = GROUPS_PER_TILE * p
    cw = chunk * LANES
    n_rows = t_seg // chunk
    wspec = lambda r, c: pl.BlockSpec((1, r, c), lambda q, s: (q, 0, 0))
    state_spec = pl.BlockSpec((b, ns), lambda q, s: (0, q))
    y, sr, si = pl.pallas_call(
        functools.partial(_s5_seq_kernel, chunk=chunk, n_rows=n_rows, nb=b),
        grid=(nq, t // t_seg),
        in_specs=[pl.BlockSpec((b, t_seg, LANES), lambda q, s: (0, s, q)),
                  wspec(cw, ns), wspec(cw, ns), wspec(cw, cw), wspec(ns, cw), wspec(ns, cw),
                  wspec(1, ns), wspec(1, ns), pl.BlockSpec((1, LANES), lambda q, s: (0, q)), state_spec, state_spec],
        out_specs=[pl.BlockSpec((b, t_seg, LANES), lambda q, s: (0, s, q)), state_spec, state_spec],
        out_shape=[jax.ShapeDtypeStruct((b, t, n), BF16), jax.ShapeDtypeStruct((b, g * p), F32),
                   jax.ShapeDtypeStruct((b, g * p), F32)],
        scratch_shapes=[pltpu.VMEM((n_rows, b, cw), F32),
                        pltpu.VMEM((n_rows, b, ns), F32), pltpu.VMEM((n_rows, b, ns), F32),
                        pltpu.VMEM((n_rows, b, ns), F32), pltpu.VMEM((n_rows, b, ns), F32),
                        pltpu.VMEM((n_rows, b, cw), F32), pltpu.VMEM((b, t_seg, LANES), F32),
                        pltpu.VMEM((b, ns), F32), pltpu.VMEM((b, ns), F32)],
        compiler_params=_cparams(("parallel", "arbitrary")),
        name="s5_sequence",
    )(u, mats["w1r"], mats["w1i"], mats["kt"], mats["w2r"], mats["w2i"], mats["ar"], mats["ai"],
      d_skip.reshape(1, n).astype(F32),
      s0_re.astype(F32).reshape(b, g * p), s0_im.astype(F32).reshape(b, g * p))
    return y, sr.reshape(b, g, p), si.reshape(b, g, p)


def _s5_step_kernel(u_ref, w1r_ref, w1i_ref, kt_ref, w2r_ref, w2i_ref, ar_ref, ai_ref, d_ref, s0r_ref, s0i_ref,
                    o_ref, sfr_ref, sfi_ref):
    dot = functools.partial(jnp.dot, preferred_element_type=F32, precision=lax.Precision.HIGHEST)
    x = u_ref[...]
    sr, si = s0r_ref[...], s0i_ref[...]
    ar, ai = ar_ref[0], ai_ref[0]
    sfr_ref[...] = ar * sr - ai * si + dot(x, w1r_ref[0])
    sfi_ref[...] = ar * si + ai * sr + dot(x, w1i_ref[0])
    y = dot(x, kt_ref[0]) + dot(sr, w2r_ref[0]) + dot(si, w2i_ref[0])
    o_ref[...] = jax.nn.gelu(y + d_ref[...] * x).astype(o_ref.dtype)


def s5_step(u, s0_re, s0_im, d_skip, mats):
    b, n = u.shape
    g, p = s0_re.shape[1:]
    nq = n // LANES
    ns = GROUPS_PER_TILE * p
    wspec = lambda r, c: pl.BlockSpec((1, r, c), lambda q: (q, 0, 0))
    state_spec = pl.BlockSpec((b, ns), lambda q: (0, q))
    lane_spec = pl.BlockSpec((b, LANES), lambda q: (0, q))
    y, sr, si = pl.pallas_call(
        _s5_step_kernel,
        grid=(nq,),
        in_specs=[lane_spec, wspec(LANES, ns), wspec(LANES, ns), wspec(LANES, LANES), wspec(ns, LANES),
                  wspec(ns, LANES), wspec(1, ns), wspec(1, ns), pl.BlockSpec((1, LANES), lambda q: (0, q)),
                  state_spec, state_spec],
        out_specs=[lane_spec, state_spec, state_spec],
        out_shape=[jax.ShapeDtypeStruct((b, n), BF16), jax.ShapeDtypeStruct((b, g * p), F32),
                   jax.ShapeDtypeStruct((b, g * p), F32)],
        compiler_params=_cparams(("parallel",)),
        name="s5_step",
    )(u, mats["w1r"], mats["w1i"], mats["kt"], mats["w2r"], mats["w2i"], mats["ar"], mats["ai"],
      d_skip.reshape(1, n).astype(F32), s0_re.astype(F32).reshape(b, g * p), s0_im.astype(F32).reshape(b, g * p))
    return y, sr.reshape(b, g, p), si.reshape(b, g, p)


def _glu_kernel(y_ref, w_ref, o_ref):
    n = o_ref.shape[-1]
    z = jnp.dot(y_ref[...], w_ref[...], preferred_element_type=F32)
    o_ref[...] = (z[:, :n] * jax.nn.sigmoid(z[:, n:])).astype(o_ref.dtype)


def ssm_glu(y, w_glu, tm):
    m, n = y.shape
    return pl.pallas_call(
        _glu_kernel,
        grid=(m // tm,),
        in_specs=[pl.BlockSpec((tm, n), lambda i: (i, 0)), pl.BlockSpec((n, 2 * n), lambda i: (0, 0))],
        out_specs=pl.BlockSpec((tm, n), lambda i: (i, 0)),
        out_shape=jax.ShapeDtypeStruct((m, n), BF16),
        compiler_params=_cparams(("parallel",)),
        name="ssm_glu",
    )(y, w_glu)


def _sortable_key(score):
    bits = pltpu.bitcast(score, jnp.int32)
    return jnp.where(bits < 0, bits ^ jnp.int32(0x7FFFFFFF), bits)


def _kth_largest_key(key, k):
    rows = key.shape[0]

    def body(i, res):
        cand = res + jnp.left_shift(jnp.int32(1), 31 - i)
        cnt = jnp.sum(jnp.where(key >= cand, 1.0, 0.0), axis=-1, keepdims=True)
        return jnp.where(cnt >= float(k), cand, res)

    return lax.fori_loop(0, 32, body, jnp.full((rows, 1), INT_MIN, jnp.int32))


def _exclusive_cumsum_lanes(flags_bf16):
    rows, n = flags_bf16.shape
    r = lax.broadcasted_iota(jnp.int32, (LANES, LANES), 0)
    c = lax.broadcasted_iota(jnp.int32, (LANES, LANES), 1)
    tri = jnp.where(r < c, 1.0, 0.0).astype(BF16)
    off = jnp.zeros((rows, 1), F32)
    out = []
    for i in range(n // LANES):
        chunk = flags_bf16[:, i * LANES:(i + 1) * LANES]
        out.append(jnp.dot(chunk, tri, preferred_element_type=F32) + off)
        off = off + jnp.sum(chunk.astype(F32), axis=-1, keepdims=True)
    return jnp.concatenate(out, axis=-1)


def _attend_keys(qi_ref, ki_ref, q_ref, k_ref, v_ref, o_ref, score_scr, bias_scr, wb_scr, *, qb, tq, lk, topk):
    n_chunks = lk // MXU_WIDTH
    qi_all = qi_ref[0].reshape(H_IDX * tq, D_IDX)

    def chunk_body(c, carry):
        k0 = pl.multiple_of(c * MXU_WIDTH, MXU_WIDTH)
        s = lax.dot_general(qi_all, ki_ref[0, pl.ds(k0, MXU_WIDTH), :], NT_DIMS, preferred_element_type=F32)
        acc = jnp.zeros((tq, MXU_WIDTH), F32)
        for h in range(H_IDX):
            wb = wb_scr[h]
            acc = acc + jnp.maximum(s[h * tq:(h + 1) * tq], 0.0) * jnp.concatenate([wb, wb], axis=-1)
        score_scr[c] = acc
        return carry

    lax.fori_loop(0, n_chunks, chunk_body, 0)
    score = jnp.concatenate([score_scr[c] for c in range(n_chunks)], axis=-1)
    row = qb * tq + lax.broadcasted_iota(jnp.int32, (tq, lk), 0)
    col = lax.broadcasted_iota(jnp.int32, (tq, lk), 1)
    causal = col <= row
    key = _sortable_key(jnp.where(causal, score, -jnp.inf))
    thr = _kth_largest_key(key, topk)
    gt = key > thr
    eq = jnp.logical_and(key == thr, causal)
    need = topk - jnp.sum(gt.astype(jnp.int32), axis=-1, keepdims=True)
    n_eq = jnp.sum(eq.astype(jnp.int32), axis=-1, keepdims=True)
    bias_scr[:, :lk] = jnp.where(jnp.logical_and(causal, key >= thr), 0.0, NEG_INF)

    @pl.when(jnp.max(n_eq - need) > 0)
    def _():
        rank = _exclusive_cumsum_lanes(jnp.where(eq, 1.0, 0.0).astype(BF16))
        keep = jnp.logical_or(gt, jnp.logical_and(eq, rank < need.astype(F32)))
        bias_scr[:, :lk] = jnp.where(jnp.logical_and(causal, keep), 0.0, NEG_INF)

    bias = bias_scr[:, :lk]
    scale = HEAD_DIM ** -0.5
    for h in range(q_ref.shape[-1] // HEAD_DIM):
        sl = slice(h * HEAD_DIM, (h + 1) * HEAD_DIM)
        logits = lax.dot_general(q_ref[0, :, sl], k_ref[0, :lk, sl], NT_DIMS, preferred_element_type=F32) * scale + bias
        m = jnp.max(logits, axis=-1, keepdims=True)
        p = jnp.exp(logits - m)
        den = jnp.sum(p, axis=-1, keepdims=True)
        o = jnp.dot(p.astype(BF16), v_ref[0, :lk, sl], preferred_element_type=F32) / den
        o_ref[0, :, sl] = o.astype(o_ref.dtype)


def _prompt_attn_kernel(qi_ref, ki_ref, w_ref, q_ref, k_ref, v_ref, o_ref, score_scr, bias_scr, wb_scr,
                        *, tq, seq, topk, n_buckets):
    qb = pl.program_id(1)
    w = w_ref[0] * (D_IDX ** -0.5)
    for h in range(H_IDX):
        wb_scr[h] = jnp.broadcast_to(w[:, h:h + 1], (tq, LANES))
    per = (seq // tq) // n_buckets
    for bkt in range(n_buckets):
        body = functools.partial(_attend_keys, qi_ref, ki_ref, q_ref, k_ref, v_ref, o_ref, score_scr, bias_scr, wb_scr,
                                 qb=qb, tq=tq, lk=(bkt + 1) * per * tq, topk=topk)
        pl.when(qb // per == bkt)(body)


def prompt_attention(qi, ki, wi, q, k, v, tq, n_buckets):
    b, t, n = q.shape
    topk = min(TOPK_MAX, t // 4)
    return pl.pallas_call(
        functools.partial(_prompt_attn_kernel, tq=tq, seq=t, topk=topk, n_buckets=n_buckets),
        grid=(b, t // tq),
        in_specs=[pl.BlockSpec((1, H_IDX, tq, D_IDX), lambda i, j: (i, 0, j, 0)),
                  pl.BlockSpec((1, t, D_IDX), lambda i, j: (i, 0, 0)),
                  pl.BlockSpec((1, tq, H_IDX), lambda i, j: (i, j, 0)),
                  pl.BlockSpec((1, tq, n), lambda i, j: (i, j, 0)),
                  pl.BlockSpec((1, t, n), lambda i, j: (i, 0, 0)),
                  pl.BlockSpec((1, t, n), lambda i, j: (i, 0, 0))],
        out_specs=pl.BlockSpec((1, tq, n), lambda i, j: (i, j, 0)),
        out_shape=jax.ShapeDtypeStruct((b, t, n), BF16),
        scratch_shapes=[pltpu.VMEM((t // MXU_WIDTH, tq, MXU_WIDTH), F32), pltpu.VMEM((tq, t), F32),
                        pltpu.VMEM((H_IDX, tq, LANES), F32)],
        compiler_params=_cparams(("parallel", "parallel")),
        name="prompt_attn",
    )(qi, ki, wi, q, k, v)


def _sample_score_kernel(pt_ref, qi_ref, w_ref, kin_ref, cache_ref, o_ref, kbuf, sem, *, n_pages, n_batch):
    b = pl.program_id(0)
    past = n_pages * PAGE_SIZE

    def page_copy(bb, slot, pg):
        return pltpu.make_async_copy(cache_ref.at[pt_ref[bb, pg]],
                                     kbuf.at[slot, :, pl.ds(pl.multiple_of(pg * PAGE_SIZE, PAGE_SIZE), PAGE_SIZE)],
                                     sem.at[slot])

    def start_all(bb, slot):
        lax.fori_loop(0, n_pages, lambda pg, c: (page_copy(bb, slot, pg).start(), c)[1], 0)

    @pl.when(b == 0)
    def _():
        start_all(0, 0)

    @pl.when(b + 1 < n_batch)
    def _():
        start_all(b + 1, (b + 1) % 2)

    slot = b % 2
    lax.fori_loop(0, n_pages, lambda pg, c: (page_copy(b, slot, pg).wait(), c)[1], 0)

    qi = qi_ref[0]
    w = w_ref[0] * (D_IDX ** -0.5)
    step = 2048
    for c in range(past // step):
        kc = kbuf[slot, :, c * step:(c + 1) * step].astype(BF16)
        s = jnp.dot(qi, kc, preferred_element_type=F32)
        o_ref[0, :, c * step:(c + 1) * step] = jnp.sum(jnp.maximum(s, 0.0) * w, axis=0, keepdims=True)
    s_new = jnp.sum(qi.astype(F32) * kin_ref[0].astype(BF16).astype(F32), axis=-1, keepdims=True)
    s_new = jnp.sum(jnp.maximum(s_new, 0.0) * w, axis=0, keepdims=True)
    lane = lax.broadcasted_iota(jnp.int32, (1, LANES), 1)
    o_ref[0, :, past:] = jnp.where(lane == 0, s_new, -jnp.inf)


def sample_scores(page_table, qi, wi, ki_new, cache_idx_k):
    bsz, n_pages = page_table.shape
    past = n_pages * PAGE_SIZE
    grid_spec = pltpu.PrefetchScalarGridSpec(
        num_scalar_prefetch=1,
        grid=(bsz,),
        in_specs=[pl.BlockSpec((1, H_IDX, D_IDX), lambda i, pt: (i, 0, 0)),
                  pl.BlockSpec((1, H_IDX, 1), lambda i, pt: (i, 0, 0)),
                  pl.BlockSpec((1, 1, D_IDX), lambda i, pt: (i, 0, 0)),
                  pl.BlockSpec(memory_space=pl.ANY)],
        out_specs=pl.BlockSpec((1, 1, past + LANES), lambda i, pt: (i, 0, 0)),
        scratch_shapes=[pltpu.VMEM((2, D_IDX, past), F32), pltpu.SemaphoreType.DMA((2,))],
    )
    return pl.pallas_call(
        functools.partial(_sample_score_kernel, n_pages=n_pages, n_batch=bsz),
        grid_spec=grid_spec,
        out_shape=jax.ShapeDtypeStruct((bsz, 1, past + LANES), F32),
        compiler_params=_cparams(("arbitrary",)),
        name="sample_scores",
    )(page_table, qi, wi, ki_new, jnp.swapaxes(cache_idx_k, 1, 2))


def _sample_select_kernel(score_ref, pos_ref, o_ref, rank_scr, *, topk):
    score = score_ref[...]
    bsz, n = score.shape
    key = _sortable_key(score)
    thr = _kth_largest_key(key, topk)
    gt = key > thr
    eq = key == thr
    need = topk - jnp.sum(gt.astype(jnp.int32), axis=-1, keepdims=True)
    rank_eq = _exclusive_cumsum_lanes(jnp.where(eq, 1.0, 0.0).astype(BF16))
    sel = jnp.logical_or(gt, jnp.logical_and(eq, rank_eq < need.astype(F32)))
    self32 = jnp.where(sel, 1.0, 0.0)
    rank = _exclusive_cumsum_lanes(self32.astype(BF16))
    rank_scr[...] = jnp.where(sel, rank, -1.0)
    slot_id = lax.broadcasted_iota(jnp.int32, (topk, 1), 0).astype(F32)
    n_split = 3
    step = n // n_split

    def body(bb, c):
        acc = jnp.zeros((topk, LANES), F32)
        for s in range(n_split):
            onehot = jnp.where(rank_scr[pl.ds(bb, 1), s * step:(s + 1) * step] == slot_id, 1.0, 0.0).astype(BF16)
            acc = acc + jnp.dot(onehot, pos_ref[s * step:(s + 1) * step, :], preferred_element_type=F32)
        o_ref[bb] = acc
        return c

    lax.fori_loop(0, bsz, body, 0)


def sample_select(scores, topk):
    bsz, n = scores.shape
    s = jnp.arange(n, dtype=jnp.int32)
    lane = jnp.arange(LANES, dtype=jnp.int32)[None, :]
    pos = jnp.where(lane == 0, (s // PAGE_SIZE)[:, None], jnp.where(lane == 1, (s % PAGE_SIZE)[:, None], 0)).astype(BF16)
    return pl.pallas_call(
        functools.partial(_sample_select_kernel, topk=topk),
        out_shape=jax.ShapeDtypeStruct((bsz, topk, LANES), F32),
        scratch_shapes=[pltpu.VMEM((bsz, n), F32)],
        compiler_params=pltpu.CompilerParams(vmem_limit_bytes=VMEM_LIMIT),
        name="sample_select",
    )(scores, pos)


def _sample_attend_kernel(idx_ref, pt_ref, q_ref, knew_ref, vnew_ref, ck_ref, cv_ref, o_ref, kbuf, vbuf, ksem, vsem,
                          *, topk, past, n_batch):
    b = pl.program_id(0)

    def row_copies(bb, slot, r):
        i = jnp.minimum(idx_ref[bb, r], past - 1)
        phys = pt_ref[bb, i // PAGE_SIZE]
        off = i % PAGE_SIZE
        return (pltpu.make_async_copy(ck_ref.at[phys, off], kbuf.at[slot, r], ksem.at[slot]),
                pltpu.make_async_copy(cv_ref.at[phys, off], vbuf.at[slot, r], vsem.at[slot]))

    def start_all(bb, slot):
        def body(r, c):
            for cp in row_copies(bb, slot, r):
                cp.start()
            return c

        lax.fori_loop(0, topk, body, 0, unroll=8)

    @pl.when(b == 0)
    def _():
        start_all(0, 0)

    @pl.when(b + 1 < n_batch)
    def _():
        start_all(b + 1, (b + 1) % 2)

    slot = b % 2

    def wait_body(r, c):
        for cp in row_copies(b, slot, r):
            cp.wait()
        return c

    lax.fori_loop(0, topk, wait_body, 0, unroll=8)

    @pl.when(idx_ref[b, topk - 1] == past)
    def _():
        kbuf[slot, topk - 1] = knew_ref[0]
        vbuf[slot, topk - 1] = vnew_ref[0]

    q = q_ref[0]
    k = kbuf[slot]
    logits = jnp.sum(k * q[None], axis=-1, keepdims=True) * (HEAD_DIM ** -0.5)
    m = jnp.max(logits, axis=0, keepdims=True)
    p = jnp.exp(logits - m)
    den = jnp.sum(p, axis=0)
    o_ref[0] = jnp.sum(p * vbuf[slot], axis=0) / den


def sample_attend(idx, page_table, q, k_new, v_new, cache_k, cache_v):
    bsz, h, dh = q.shape
    topk = idx.shape[1]
    past = page_table.shape[1] * PAGE_SIZE
    grid_spec = pltpu.PrefetchScalarGridSpec(
        num_scalar_prefetch=2,
        grid=(bsz,),
        in_specs=[pl.BlockSpec((1, h, dh), lambda i, a, c: (i, 0, 0)), pl.BlockSpec((1, h, dh), lambda i, a, c: (i, 0, 0)),
                  pl.BlockSpec((1, h, dh), lambda i, a, c: (i, 0, 0)),
                  pl.BlockSpec(memory_space=pl.ANY), pl.BlockSpec(memory_space=pl.ANY)],
        out_specs=pl.BlockSpec((1, h, dh), lambda i, a, c: (i, 0, 0)),
        scratch_shapes=[pltpu.VMEM((2, topk, h, dh), F32), pltpu.VMEM((2, topk, h, dh), F32),
                        pltpu.SemaphoreType.DMA((2,)), pltpu.SemaphoreType.DMA((2,))],
    )
    return pl.pallas_call(
        functools.partial(_sample_attend_kernel, topk=topk, past=past, n_batch=bsz),
        grid_spec=grid_spec,
        out_shape=jax.ShapeDtypeStruct((bsz, h, dh), F32),
        compiler_params=_cparams(("arbitrary",)),
        name="sample_attend",
    )(idx, page_table, q, k_new, v_new, cache_k, cache_v)


def _merge_kernel(h_ref, s_ref, a_ref, x_ref, wgs_ref, wga_ref, wbs_ref, wba_ref, wo_ref, gn_ref, o_ref, hn_ref,
                  *, n_tiles):
    j = pl.program_id(1)
    dot = functools.partial(jnp.dot, preferred_element_type=F32)
    tm = h_ref.shape[0]
    sub = min(tm, MERGE_SUB_ROWS)

    @pl.when(j == 0)
    def _():
        o_ref[...] = x_ref[...]

    for r0 in range(0, tm, sub):
        rows = slice(r0, r0 + sub)
        h = h_ref[rows, :]
        merged = (jax.nn.sigmoid(dot(h, wgs_ref[0])) * dot(s_ref[rows, :], wbs_ref[0])
                  + jax.nn.sigmoid(dot(h, wga_ref[0])) * dot(a_ref[rows, :], wba_ref[0]))
        o_ref[rows, :] += dot(merged.astype(BF16), wo_ref[...])

    @pl.when(j == n_tiles - 1)
    def _():
        for r0 in range(0, tm, sub):
            rows = slice(r0, r0 + sub)
            x1 = o_ref[rows, :]
            ms = jnp.mean(x1 * x1, axis=-1, keepdims=True)
            hn_ref[rows, :] = (x1 * lax.rsqrt(ms + RMS_EPS) * gn_ref[...]).astype(hn_ref.dtype)


def merge_branches(h, ssm_out, att, x2d, w_gates, w_bs, w_ba, w_out, g_next, tm, tn):
    m, d = x2d.shape
    kdim = ssm_out.shape[1]
    nj = d // tn
    row = lambda c: pl.BlockSpec((tm, c), lambda i, j: (i, 0), pipeline_mode=pl.Buffered(1))
    return pl.pallas_call(
        functools.partial(_merge_kernel, n_tiles=nj),
        grid=(m // tm, nj),
        in_specs=[row(d), row(kdim), row(kdim), row(d),
                  pl.BlockSpec((1, d, tn), lambda i, j: (j, 0, 0)), pl.BlockSpec((1, d, tn), lambda i, j: (j + nj, 0, 0)),
                  pl.BlockSpec((1, kdim, tn), lambda i, j: (j, 0, 0)), pl.BlockSpec((1, kdim, tn), lambda i, j: (j, 0, 0)),
                  pl.BlockSpec((tn, d), lambda i, j: (j, 0)), pl.BlockSpec((1, d), lambda i, j: (0, 0))],
        out_specs=[row(d), row(d)],
        out_shape=[jax.ShapeDtypeStruct((m, d), F32), jax.ShapeDtypeStruct((m, d), BF16)],
        compiler_params=_cparams(("parallel", "arbitrary")),
        name="merge_branches",
    )(h, ssm_out, att, x2d, w_gates, w_gates, w_bs, w_ba, w_out, g_next.reshape(1, d).astype(F32))


def _ffn_step_kernel(x_ref, gf_ref, gfin_ref, wa_ref, wb_ref, cwa_ref, cwb_ref, cba_ref, cbb_ref, wd_ref,
                     p2a_ref, p1a_ref, p2b_ref, p1b_ref, y_ref, ca_ref, cb_ref, h_scr, acc_scr, *, n_ff_tiles):
    j = pl.program_id(1)

    @pl.when(j == 0)
    def _():
        x = x_ref[0]
        ms = jnp.mean(x * x, axis=-1, keepdims=True)
        h_scr[...] = (x * lax.rsqrt(ms + RMS_EPS) * gf_ref[...]).astype(BF16)
        acc_scr[...] = jnp.zeros_like(acc_scr)

    h = h_scr[...]

    def conv_half(w_ref, cw_ref, cbias_ref, out_ref, p2_ref, p1_ref):
        up = jnp.dot(h, w_ref[0], preferred_element_type=F32)
        cw = cw_ref[...]
        out_ref[...] = up
        return cbias_ref[...] + cw[0:1, :] * p2_ref[...] + cw[1:2, :] * p1_ref[...] + cw[2:3, :] * up

    a = conv_half(wa_ref, cwa_ref, cba_ref, ca_ref, p2a_ref, p1a_ref)
    g = conv_half(wb_ref, cwb_ref, cbb_ref, cb_ref, p2b_ref, p1b_ref)
    act = (jax.nn.silu(a) * g).astype(BF16)
    acc_scr[...] += jnp.dot(act, wd_ref[...], preferred_element_type=F32)

    @pl.when(j == n_ff_tiles - 1)
    def _():
        x2 = x_ref[0] + acc_scr[...]
        ms = jnp.mean(x2 * x2, axis=-1, keepdims=True)
        y_ref[0] = x2 * lax.rsqrt(ms + RMS_EPS) * gfin_ref[...]


def conv_ffn_step(x, conv_prev, g_ffn, g_final, w_up, conv_w, conv_b, w_down, tm):
    b, t, d = x.shape
    tf = w_up.shape[2]
    f = w_down.shape[0]
    nf = f // tf
    gf = g_ffn.reshape(1, d).astype(F32)
    gfin = g_final.reshape(1, d).astype(F32)
    cb2 = conv_b.reshape(1, 2 * f).astype(F32)
    conv_w = conv_w.astype(F32)
    kern = functools.partial(_ffn_step_kernel, n_ff_tiles=nf)
    p2, p1 = conv_prev[:, 0, :], conv_prev[:, 1, :]
    wspec = lambda blk, off: pl.BlockSpec(blk, lambda i, j: (0, j + off))
    upspec = lambda off: pl.BlockSpec((1, d, tf), lambda i, j: (j + off, 0, 0))
    in_specs = [pl.BlockSpec((1, tm, d), lambda i, j: (0, i, 0)),
                pl.BlockSpec((1, d), lambda i, j: (0, 0)), pl.BlockSpec((1, d), lambda i, j: (0, 0)),
                upspec(0), upspec(nf), wspec((CONV_W, tf), 0), wspec((CONV_W, tf), nf),
                wspec((1, tf), 0), wspec((1, tf), nf),
                pl.BlockSpec((tf, d), lambda i, j: (j, 0)),
                pl.BlockSpec((tm, tf), lambda i, j: (i, j)), pl.BlockSpec((tm, tf), lambda i, j: (i, j)),
                pl.BlockSpec((tm, tf), lambda i, j: (i, j + nf)), pl.BlockSpec((tm, tf), lambda i, j: (i, j + nf))]
    out_specs = [pl.BlockSpec((1, tm, d), lambda i, j: (0, i, 0)),
                 pl.BlockSpec((tm, tf), lambda i, j: (i, j)), pl.BlockSpec((tm, tf), lambda i, j: (i, j))]
    out_shape = [jax.ShapeDtypeStruct((1, t, d), F32), jax.ShapeDtypeStruct((t, f), F32),
                 jax.ShapeDtypeStruct((t, f), F32)]
    scratch = [pltpu.VMEM((tm, d), BF16), pltpu.VMEM((tm, d), F32)]
    y, ua, ub = pl.pallas_call(
        kern, grid=(t // tm, nf), in_specs=in_specs, out_specs=out_specs, out_shape=out_shape,
        scratch_shapes=scratch, compiler_params=_cparams(("arbitrary", "arbitrary")),
        name="conv_ffn_step",
    )(x, gf, gfin, w_up, w_up, conv_w, conv_w, cb2, cb2, w_down, p2, p1, p2, p1)
    return y, jnp.stack([p1, jnp.concatenate([ua, ub], axis=-1)], axis=1)


def _ffn_seq_kernel(h_ref, xt_ref, gfin_ref, wa_ref, wb_ref, cwa_ref, cwb_ref, cba_ref, cbb_ref, wd_ref, pa_ref, pb_ref,
                    y_ref, ca_ref, cb_ref, act_scr, car_a, car_b, *, tm, tf, nf, tn, nd):
    i = pl.program_id(1)
    j = pl.program_id(2)

    @pl.when(j < nf)
    def _():
        sub = min(tm, FFN_SUB_ROWS)
        row = lax.broadcasted_iota(jnp.int32, (sub, tf), 0)

        def conv_half(w_ref, cw_ref, cbias_ref, out_ref, prev_ref, car):
            cw = cw_ref[...]
            cbias = cbias_ref[...]

            @pl.when(i == 0)
            def _():
                car[j] = prev_ref[0]

            c2 = car[j]
            rm2, rm1 = c2[0:1, :], c2[1:2, :]
            outs = []
            for r0 in range(0, tm, sub):
                up = jnp.dot(h_ref[0, r0:r0 + sub, :], w_ref[0], preferred_element_type=F32)
                p1 = jnp.where(row == 0, rm1, pltpu.roll(up, 1, 0))
                p2 = jnp.where(row == 0, rm2, jnp.where(row == 1, rm1, pltpu.roll(up, 2, 0)))
                outs.append(cbias + cw[0:1, :] * p2 + cw[1:2, :] * p1 + cw[2:3, :] * up)
                rm2, rm1 = up[sub - 2:sub - 1, :], up[sub - 1:sub, :]
            last2 = jnp.concatenate([rm2, rm1], axis=0)
            car[j] = last2
            out_ref[0, 0] = last2
            return outs

        a = conv_half(wa_ref, cwa_ref, cba_ref, ca_ref, pa_ref, car_a)
        g = conv_half(wb_ref, cwb_ref, cbb_ref, cb_ref, pb_ref, car_b)
        act = [(jax.nn.silu(ar) * gr).astype(BF16) for ar, gr in zip(a, g)]
        for k in range(nf):

            @pl.when(j == k)
            def _(k=k):
                for r, act_r in enumerate(act):
                    act_scr[r * sub:(r + 1) * sub, k * tf:(k + 1) * tf] = act_r

    @pl.when(j >= nf)
    def _():
        sub = min(tm, 2 * FFN_SUB_ROWS)
        for r0 in range(0, tm, sub):
            part = xt_ref[0, r0:r0 + sub, :] + jnp.dot(act_scr[r0:r0 + sub, :], wd_ref[0], preferred_element_type=F32)
            for n in range(nd):

                @pl.when(j == nf + n)
                def _(n=n, part=part, r0=r0):
                    y_ref[0, r0:r0 + sub, n * tn:(n + 1) * tn] = part

    @pl.when(j == nf + nd - 1)
    def _():
        sub = min(tm, FFN_SUB_ROWS)
        for r0 in range(0, tm, sub):
            x2 = y_ref[0, r0:r0 + sub, :]
            ms = jnp.mean(x2 * x2, axis=-1, keepdims=True)
            y_ref[0, r0:r0 + sub, :] = x2 * lax.rsqrt(ms + RMS_EPS) * gfin_ref[...]


def conv_ffn_sequence(h, x, conv_prev, g_final, w_up, conv_w, conv_b, w_down, tm):
    b, t, d = x.shape
    tf = w_up.shape[2]
    nd, f, tn = w_down.shape
    nf, nt = f // tf, t // tm
    gfin = g_final.reshape(1, d).astype(F32)
    cb2 = conv_b.reshape(1, 2 * f).astype(F32)
    conv_w = conv_w.astype(F32)
    up_tile = lambda j: jnp.minimum(j, nf - 1)
    down_tile = lambda j: jnp.clip(j - nf, 0, nd - 1)
    wspec = lambda blk, off: pl.BlockSpec(blk, lambda bi, i, j: (0, up_tile(j) + off))
    upspec = lambda off: pl.BlockSpec((1, d, tf), lambda bi, i, j: (up_tile(j) + off, 0, 0))
    prev_spec = lambda off: pl.BlockSpec((1, CONV_W - 1, tf), lambda bi, i, j: (bi, 0, up_tile(j) + off))
    state_spec = pl.BlockSpec((1, 1, CONV_W - 1, tf), lambda bi, i, j: (bi, i, 0, up_tile(j)))
    y, ca, cb = pl.pallas_call(
        functools.partial(_ffn_seq_kernel, tm=tm, tf=tf, nf=nf, tn=tn, nd=nd),
        grid=(b, nt, nf + nd),
        in_specs=[pl.BlockSpec((1, tm, d), lambda bi, i, j: (bi, i, 0), pipeline_mode=pl.Buffered(1)),
                  pl.BlockSpec((1, tm, tn), lambda bi, i, j: (bi, i, down_tile(j))),
                  pl.BlockSpec((1, d), lambda bi, i, j: (0, 0)),
                  upspec(0), upspec(nf), wspec((CONV_W, tf), 0), wspec((CONV_W, tf), nf),
                  wspec((1, tf), 0), wspec((1, tf), nf),
                  pl.BlockSpec((1, f, tn), lambda bi, i, j: (down_tile(j), 0, 0)),
                  prev_spec(0), prev_spec(nf)],
        out_specs=[pl.BlockSpec((1, tm, d), lambda bi, i, j: (bi, i, 0), pipeline_mode=pl.Buffered(1)),
                   state_spec, state_spec],
        out_shape=[jax.ShapeDtypeStruct((b, t, d), F32), jax.ShapeDtypeStruct((b, nt, CONV_W - 1, f), F32),
                   jax.ShapeDtypeStruct((b, nt, CONV_W - 1, f), F32)],
        scratch_shapes=[pltpu.VMEM((tm, f), BF16),
                        pltpu.VMEM((nf, CONV_W - 1, tf), F32), pltpu.VMEM((nf, CONV_W - 1, tf), F32)],
        compiler_params=_cparams(("arbitrary", "arbitrary", "arbitrary")),
        name="conv_ffn_seq",
    )(h, x, gfin, w_up, w_up, conv_w, conv_w, cb2, cb2, w_down, conv_prev, conv_prev)
    return y, jnp.concatenate([ca[:, -1], cb[:, -1]], axis=-1)


def column_tiles(w, tn):
    k, n = w.shape
    return w.astype(BF16).reshape(k, n // tn, tn).transpose(1, 0, 2)


def _split_w_in(w_in, d_model, n_ssm, n_att):
    sizes = [n_ssm, n_att, n_att, n_att, H_IDX * D_IDX, D_IDX, H_IDX, d_model, d_model]
    offs = np.concatenate([[0], np.cumsum(sizes)]).tolist()
    col = lambda i: w_in[:, offs[i]:offs[i + 1]]
    w_uv = column_tiles(jnp.concatenate([col(0), col(3)], axis=1), COL_TILE)
    w_qk = column_tiles(jnp.concatenate([col(1), col(2)], axis=1), COL_TILE)
    w_qi = column_tiles(col(4), COL_TILE)
    pad = jnp.zeros((w_in.shape[0], LANES - D_IDX - H_IDX), w_in.dtype)
    w_kw = column_tiles(jnp.concatenate([col(5), col(6), pad], axis=1), LANES)
    w_g = column_tiles(jnp.concatenate([col(7), col(8)], axis=1), COL_TILE)
    return w_uv, w_qk, w_qi, w_kw, w_g


def _mix_inputs(x2d, pos, n_pos_tiles, tm, norm_mix, w_groups, n_ssm, n_att, sequence):
    w_uv, w_qk, w_qi, w_kw, w_g = w_groups
    h = rmsnorm_bf16(x2d, norm_mix, tm)
    tn = COL_TILE
    nu, na = n_ssm // tn, n_att // tn
    half_qk = HEAD_DIM // ROPE_FRACTION // 2
    half_idx = D_IDX // ROPE_FRACTION // 2
    out = {}
    if sequence:
        out["u"], out["v"], out["v16"] = project(
            h, w_uv, tm, tn, [(0, nu, F32, "flat"), (nu, na, F32, "flat"), (nu, na, BF16, "flat")])
        out["q16"], out["k"], out["k16"] = project(
            h, w_qk, tm, tn, [(0, na, BF16, "flat"), (na, na, F32, "flat"), (na, na, BF16, "flat")],
            "rope", half_qk, rope_tables(pos, HEAD_DIM, tn), n_pos_tiles)
        out["qi16"], = project(h, w_qi, tm, tn, [(0, H_IDX * D_IDX // tn, BF16, "heads")],
                               "rope", half_idx, rope_tables(pos, D_IDX, tn), n_pos_tiles)
    else:
        out["u"], out["v"] = project(h, w_uv, tm, tn, [(0, nu, F32, "flat"), (nu, na, F32, "flat")])
        out["q"], out["k"] = project(h, w_qk, tm, tn, [(0, na, F32, "flat"), (na, na, F32, "flat")],
                                     "rope", half_qk, rope_tables(pos, HEAD_DIM, tn), n_pos_tiles)
        out["qi"], = project(h, w_qi, tm, tn, [(0, H_IDX * D_IDX // tn, F32, "flat")],
                             "rope", half_idx, rope_tables(pos, D_IDX, tn), n_pos_tiles)
    extra = jnp.concatenate([jnp.full((H_IDX,), H_IDX ** -0.5, F32), jnp.zeros((LANES - D_IDX - H_IDX,), F32)])
    kw, = project(h, w_kw, tm, LANES, [(0, 1, F32, "flat")], "rope", half_idx,
                  rope_tables(pos, D_IDX, LANES, extra), n_pos_tiles)
    out["ki"], out["wi"] = kw[:, :D_IDX], kw[:, D_IDX:D_IDX + H_IDX]
    out["h"] = h
    return out


def kernel(x_prompt, x_sample, cache_k, cache_v, cache_idx_k, page_table, state_ssm_re, state_ssm_im,
           state_ffn_conv, norm_mix, w_in, ssm_A_re, ssm_A_im, ssm_log_dt, ssm_B_re, ssm_B_im, ssm_C_re,
           ssm_C_im, ssm_D, w_glu, w_branch_ssm, w_branch_att, w_out, norm_ffn, w_up, ffn_conv_w,
           ffn_conv_b, w_down, norm_final):
    bp, t, d = x_prompt.shape
    bs, ts, _ = x_sample.shape
    n_ssm = ssm_D.shape[0]
    n_att = w_branch_att.shape[0]
    g = n_ssm // SSM_GROUP
    h_att = n_att // HEAD_DIM
    past = page_table.shape[1] * PAGE_SIZE
    assert ts == 1, "sample group is a single decode step"
    assert bp == SUBLANES, "the S5 sequence kernel keeps one batch row per sublane"

    w_groups = _split_w_in(w_in, d, n_ssm, n_att)
    w_glu_b, w_out_b, w_down_b = (w.astype(BF16) for w in (w_glu, w_out, w_down))
    w_bs, w_ba = column_tiles(w_branch_ssm, COL_TILE), column_tiles(w_branch_att, COL_TILE)
    w_up_t, w_down_t = column_tiles(w_up, COL_TILE), column_tiles(w_down, COL_TILE)
    ssm_params = (ssm_A_re, ssm_A_im, ssm_log_dt, ssm_B_re, ssm_B_im, ssm_C_re, ssm_C_im)

    mp = bp * t
    tm = 1024
    xp2 = x_prompt.reshape(mp, d)
    pos_p = jnp.arange(t, dtype=jnp.int32)
    pr = _mix_inputs(xp2, pos_p, t // tm, tm, norm_mix, w_groups, n_ssm, n_att, True)
    zeros_s = jnp.zeros((bp, g, SSM_STATE), F32)
    y_act, s_re_p, s_im_p = s5_sequence(pr["u"].reshape(bp, t, n_ssm), zeros_s, zeros_s, ssm_D,
                                        ssm_tile_matrices(*ssm_params, SSM_CHUNK, BF16), SSM_CHUNK, 512)
    ssm_out = ssm_glu(y_act.reshape(mp, n_ssm), w_glu_b, 512)
    att = prompt_attention(pr["qi16"], pr["ki"].astype(BF16).reshape(bp, t, D_IDX), pr["wi"].reshape(bp, t, H_IDX),
                           pr["q16"].reshape(bp, t, n_att), pr["k16"].reshape(bp, t, n_att),
                           pr["v16"].reshape(bp, t, n_att), 256, 4)
    x1, h_ffn = merge_branches(pr["h"], ssm_out, att.reshape(mp, n_att), xp2, w_groups[4], w_bs, w_ba, w_out_b,
                               norm_ffn, 1024, COL_TILE)
    zeros_c = jnp.zeros((bp, CONV_W - 1, w_up.shape[1]), F32)
    y_prompt, conv_p = conv_ffn_sequence(h_ffn.reshape(bp, t, d), x1.reshape(bp, t, d), zeros_c, norm_final, w_up_t,
                                         ffn_conv_w, ffn_conv_b, w_down_t, 1024)
    k_p = pr["k"].reshape(bp, t, h_att, HEAD_DIM)
    v_p = pr["v"].reshape(bp, t, h_att, HEAD_DIM)
    ki_p = pr["ki"].reshape(bp, t, D_IDX)

    xs2 = x_sample.reshape(bs, d)
    pos_s = jnp.full((bs,), past, jnp.int32)
    sm = _mix_inputs(xs2, pos_s, 1, bs, norm_mix, w_groups, n_ssm, n_att, False)
    y_act, s_re_s, s_im_s = s5_step(sm["u"], state_ssm_re, state_ssm_im, ssm_D, ssm_tile_matrices(*ssm_params, 1, F32))
    ssm_out = ssm_glu(y_act, w_glu_b, bs)
    scores = sample_scores(page_table, sm["qi"].astype(BF16).reshape(bs, H_IDX, D_IDX), sm["wi"].reshape(bs, H_IDX, 1),
                           sm["ki"].reshape(bs, 1, D_IDX), cache_idx_k)
    topk = min(TOPK_MAX, (past + 1) // 4)
    hl = sample_select(scores.reshape(bs, past + LANES), topk)
    idx = (hl[:, :, 0] * PAGE_SIZE + hl[:, :, 1]).astype(jnp.int32)
    att = sample_attend(idx, page_table, sm["q"].reshape(bs, h_att, HEAD_DIM), sm["k"].reshape(bs, h_att, HEAD_DIM),
                        sm["v"].reshape(bs, h_att, HEAD_DIM), cache_k, cache_v)
    x1, _ = merge_branches(sm["h"], ssm_out, att.reshape(bs, n_att).astype(BF16), xs2, w_groups[4], w_bs, w_ba, w_out_b,
                           norm_ffn, bs, 512)
    y_s, conv_s = conv_ffn_step(x1.reshape(1, bs, d), state_ffn_conv, norm_ffn, norm_final, w_up_t, ffn_conv_w,
                                ffn_conv_b, w_down_b, bs)
    y_sample = y_s.reshape(bs, 1, d)
    k_s = sm["k"].reshape(bs, 1, h_att, HEAD_DIM)
    v_s = sm["v"].reshape(bs, 1, h_att, HEAD_DIM)
    ki_s = sm["ki"].reshape(bs, 1, D_IDX)

    return (y_prompt, y_sample, k_p, v_p, ki_p, k_s, v_s, ki_s, s_re_p, s_im_p, s_re_s, s_im_s, conv_p, conv_s)
```

```python
import functools

import numpy as np
import jax
import jax.numpy as jnp
from jax import lax
from jax.experimental import pallas as pl
from jax.experimental.pallas import tpu as pltpu

F32 = jnp.float32
BF16 = jnp.bfloat16

HEAD_DIM = 128
SSM_GROUP = 16
SSM_STATE = 64
H_IDX = 16
D_IDX = 64
TOPK_MAX = 256
ROPE_THETA = 500000.0
ROPE_FRACTION = 4
CONV_W = 3
RMS_EPS = 1e-6
NEG_INF = -1e30
PAGE_SIZE = 128

LANES = 128
SUBLANES = 8
MXU_WIDTH = 256
GROUPS_PER_TILE = LANES // SSM_GROUP
SSM_CHUNK = 8
PROJ_SUB_ROWS = 256
FFN_SUB_ROWS = 256
COL_TILE = 512
MERGE_SUB_ROWS = 256
VMEM_LIMIT = 56 * 1024 * 1024
INT_MIN = -2 ** 31
NT_DIMS = (((1,), (1,)), ((), ()))
TN_DIMS = (((0,), (0,)), ((), ()))


def _cparams(sem):
    return pltpu.CompilerParams(dimension_semantics=sem, vmem_limit_bytes=VMEM_LIMIT)


def _rmsnorm_kernel(x_ref, g_ref, o_ref):
    x = x_ref[...]
    ms = jnp.mean(x * x, axis=-1, keepdims=True)
    o_ref[...] = (x * lax.rsqrt(ms + RMS_EPS) * g_ref[...]).astype(o_ref.dtype)


def rmsnorm_bf16(x2d, g, tm):
    m, d = x2d.shape
    return pl.pallas_call(
        _rmsnorm_kernel,
        grid=(m // tm,),
        in_specs=[pl.BlockSpec((tm, d), lambda i: (i, 0)), pl.BlockSpec((1, d), lambda i: (0, 0))],
        out_specs=pl.BlockSpec((tm, d), lambda i: (i, 0)),
        out_shape=jax.ShapeDtypeStruct((m, d), BF16),
        compiler_params=_cparams(("parallel",)),
        name="rmsnorm",
    )(x2d, g.reshape(1, d))


def _proj_kernel(h_ref, w_ref, *rest, mode, shift, outs, n_tab, n_col_tiles):
    tabs, o_refs = rest[:n_tab], rest[n_tab:]
    j = pl.program_id(1)
    tm, tn = h_ref.shape[0], w_ref.shape[2]
    sub = min(tm, PROJ_SUB_ROWS)
    for r0 in range(0, tm, sub):
        rows = slice(r0, r0 + sub)
        z = jnp.dot(h_ref[rows, :], w_ref[0], preferred_element_type=F32)
        if mode == "rope":
            c_ref, s1_ref, s2_ref = tabs
            z = (z * c_ref[rows, :] + pltpu.roll(z, tn - shift, 1) * s1_ref[rows, :]
                 + pltpu.roll(z, shift, 1) * s2_ref[rows, :])
        for (start, cnt, _, kind), o_ref in zip(outs, o_refs):
            if kind == "heads":
                assert start + cnt == n_col_tiles
                for hh in range(tn // D_IDX):
                    o_ref[0, hh, rows, :] = z[:, hh * D_IDX:(hh + 1) * D_IDX].astype(o_ref.dtype)
            elif start + cnt == n_col_tiles:
                o_ref[rows, :] = z.astype(o_ref.dtype)
            else:
                o_ref[rows, :] = jnp.where(j >= start + cnt, o_ref[rows, :], z.astype(o_ref.dtype))


def project(h, w, tm, tn, outs, mode="plain", shift=0, tables=None, n_pos_tiles=1):
    m, k = h.shape
    n = w.shape[0] * tn
    in_specs = [pl.BlockSpec((tm, k), lambda i, j: (i, 0)), pl.BlockSpec((1, k, tn), lambda i, j: (j, 0, 0))]
    args = [h, w]
    tables = tables or ()
    for t in tables:
        in_specs.append(pl.BlockSpec((tm, tn), lambda i, j: (i % n_pos_tiles, 0)))
        args.append(t)
    out_specs, out_shape = [], []
    for start, cnt, dtype, kind in outs:
        if kind == "heads":
            hpt = tn // D_IDX
            out_specs.append(pl.BlockSpec(
                (1, hpt, tm, D_IDX),
                lambda i, j, s=start, c=cnt: (i // n_pos_tiles, jnp.clip(j - s, 0, c - 1), i % n_pos_tiles, 0)))
            out_shape.append(jax.ShapeDtypeStruct((m // (n_pos_tiles * tm), cnt * hpt, n_pos_tiles * tm, D_IDX), dtype))
        else:
            out_specs.append(pl.BlockSpec((tm, tn), lambda i, j, s=start, c=cnt: (i, jnp.clip(j - s, 0, c - 1))))
            out_shape.append(jax.ShapeDtypeStruct((m, cnt * tn), dtype))
    return pl.pallas_call(
        functools.partial(_proj_kernel, mode=mode, shift=shift, outs=tuple(outs), n_tab=len(tables),
                          n_col_tiles=n // tn),
        grid=(m // tm, n // tn),
        in_specs=in_specs,
        out_specs=out_specs,
        out_shape=out_shape,
        compiler_params=_cparams(("parallel", "arbitrary")),
        name="proj_" + mode,
    )(*args)


def rope_tables(pos, head, tn, extra=None):
    r = head // ROPE_FRACTION
    half = r // 2
    inv = ROPE_THETA ** (-jnp.arange(half, dtype=F32) * 2.0 / r)
    ang = pos.astype(F32)[:, None] * inv[None, :]
    cos, sin = jnp.cos(ang), jnp.sin(ang)
    t = pos.shape[0]
    zeros = jnp.zeros((t, head - r), F32)
    zh = jnp.zeros((t, half), F32)
    c = jnp.concatenate([cos, cos, jnp.ones((t, head - r), F32)], axis=1)
    s1 = jnp.concatenate([-sin, zh, zeros], axis=1)
    s2 = jnp.concatenate([zh, sin, zeros], axis=1)
    if extra is None:
        reps = tn // head
        return tuple(jnp.tile(a, (1, reps)) for a in (c, s1, s2))
    pad = jnp.zeros((t, tn - head), F32)
    return (jnp.concatenate([c, jnp.broadcast_to(extra[None, :], (t, tn - head))], axis=1),
            jnp.concatenate([s1, pad], axis=1), jnp.concatenate([s2, pad], axis=1))


def ssm_tile_matrices(a_re, a_im, log_dt, b_re, b_im, c_re, c_im, chunk, dtype):
    hp = lax.Precision.HIGHEST
    g, p = a_re.shape
    gt = GROUPS_PER_TILE
    nq = g // gt
    a = lax.complex(a_re.astype(F32), a_im.astype(F32))
    dt = jnp.exp(log_dt.astype(F32))[:, None]
    adt = a * dt
    a_bar = jnp.exp(adt)
    b_bar = ((a_bar - 1.0) / a)[..., None] * lax.complex(b_re.astype(F32), b_im.astype(F32))
    cc = lax.complex(c_re.astype(F32), c_im.astype(F32))
    steps = jnp.arange(chunk + 1, dtype=F32)
    pw = jnp.exp(adt[:, None, :] * steps[None, :, None].astype(jnp.complex64))
    kd = jnp.real(jnp.einsum("gcp,gdp,gpe->gdce", cc, pw[:, :chunk], b_bar, precision=hp))
    w1g = pw[:, chunk - 1 - jnp.arange(chunk)][:, :, None, :] * b_bar.transpose(0, 2, 1)[:, None]
    m2g = cc.transpose(0, 2, 1)[:, :, None, :] * pw[:, 1:chunk + 1].transpose(0, 2, 1)[:, :, :, None]
    kdc = kd.reshape(nq, gt, chunk, SSM_GROUP, SSM_GROUP).transpose(0, 2, 4, 1, 3).reshape(nq, chunk, SSM_GROUP, LANES)
    w1c = w1g.reshape(nq, gt, chunk, SSM_GROUP, p).transpose(0, 2, 3, 1, 4).reshape(nq, chunk, SSM_GROUP, gt * p)
    m2c = m2g.reshape(nq, gt, p, chunk, SSM_GROUP).transpose(0, 3, 2, 1, 4).reshape(nq, chunk, p, LANES)
    al = pw[:, chunk].reshape(nq, 1, gt * p)
    kt, w1r, w1i, w2r, w2i = _ssm_expand(kdc, jnp.real(w1c), jnp.imag(w1c), jnp.real(m2c), -jnp.imag(m2c), chunk, dtype)
    return dict(w1r=w1r, w1i=w1i, kt=kt, w2r=w2r, w2i=w2i, ar=jnp.real(al), ai=jnp.imag(al))


def _ssm_expand_kernel(kd_ref, w1r_ref, w1i_ref, w2r_ref, w2i_ref, kt_o, w1r_o, w1i_o, w2r_o, w2i_o, *, chunk):
    gt, p = GROUPS_PER_TILE, SSM_STATE

    def same_group(rows_per_group, cols_per_group):
        shape = (gt * rows_per_group, gt * cols_per_group)
        r = jnp.right_shift(lax.broadcasted_iota(jnp.int32, shape, 0), rows_per_group.bit_length() - 1)
        c = jnp.right_shift(lax.broadcasted_iota(jnp.int32, shape, 1), cols_per_group.bit_length() - 1)
        return r == c

    def block_diag(small, mask, dtype):
        return jnp.where(mask, jnp.tile(small, (gt, 1)), 0.0).astype(dtype)

    m_cc, m_cp, m_pc = same_group(SSM_GROUP, SSM_GROUP), same_group(SSM_GROUP, p), same_group(p, SSM_GROUP)
    for j in range(chunk):
        rows = slice(j * LANES, (j + 1) * LANES)
        w1r_o[0, rows, :] = block_diag(w1r_ref[0, j], m_cp, w1r_o.dtype)
        w1i_o[0, rows, :] = block_diag(w1i_ref[0, j], m_cp, w1i_o.dtype)
        w2r_o[0, :, rows] = block_diag(w2r_ref[0, j], m_pc, w2r_o.dtype)
        w2i_o[0, :, rows] = block_diag(w2i_ref[0, j], m_pc, w2i_o.dtype)
        for t in range(chunk):
            cols = slice(t * LANES, (t + 1) * LANES)
            if t >= j:
                kt_o[0, rows, cols] = block_diag(kd_ref[0, t - j], m_cc, kt_o.dtype)
            else:
                kt_o[0, rows, cols] = jnp.zeros((LANES, LANES), kt_o.dtype)


def _ssm_expand(kdc, w1r, w1i, w2r, w2i, chunk, dtype):
    nq = kdc.shape[0]
    ns = w1r.shape[-1]
    cw = chunk * LANES
    spec4 = lambda a: pl.BlockSpec((1,) + a.shape[1:], lambda q: (q, 0, 0, 0))
    spec3 = lambda r, c: pl.BlockSpec((1, r, c), lambda q: (q, 0, 0))
    shapes = [(cw, cw), (cw, ns), (cw, ns), (ns, cw), (ns, cw)]
    return pl.pallas_call(
        functools.partial(_ssm_expand_kernel, chunk=chunk),
        grid=(nq,),
        in_specs=[spec4(a) for a in (kdc, w1r, w1i, w2r, w2i)],
        out_specs=[spec3(r, c) for r, c in shapes],
        out_shape=[jax.ShapeDtypeStruct((nq, r, c), dtype) for r, c in shapes],
        compiler_params=_cparams(("parallel",)),
        name="ssm_expand",
    )(kdc, w1r, w1i, w2r, w2i)


def _s5_seq_kernel(u_ref, w1r_ref, w1i_ref, kt_ref, w2r_ref, w2i_ref, ar_ref, ai_ref, d_ref, s0r_ref, s0i_ref,
                   o_ref, sfr_ref, sfi_ref, ucat, slr, sli, spr, spi, yc, yb, st_r, st_i, *, chunk, n_rows, nb):
    ts = pl.program_id(1)
    ns = st_r.shape[-1]

    @pl.when(ts == 0)
    def _():
        st_r[...] = s0r_ref[...]
        st_i[...] = s0i_ref[...]

    for b in range(nb):
        for tl in range(chunk):
            ucat[:, b, tl * LANES:(tl + 1) * LANES] = u_ref[b, pl.ds(tl, n_rows, stride=chunk), :]
    x = ucat[...].reshape(n_rows * nb, chunk * LANES).astype(BF16)
    slr[...] = jnp.dot(x, w1r_ref[0], preferred_element_type=F32).reshape(n_rows, nb, ns)
    sli[...] = jnp.dot(x, w1i_ref[0], preferred_element_type=F32).reshape(n_rows, nb, ns)
    ar = jnp.broadcast_to(ar_ref[0], (nb, ns))
    ai = jnp.broadcast_to(ai_ref[0], (nb, ns))

    def step(n, carry):
        sr, si = carry
        spr[n] = sr
        spi[n] = si
        return ar * sr - ai * si + slr[n], ar * si + ai * sr + sli[n]

    sr, si = lax.fori_loop(0, n_rows, step, (st_r[...], st_i[...]))
    st_r[...] = sr
    st_i[...] = si
    sfr_ref[...] = sr
    sfi_ref[...] = si
    y = (jnp.dot(x, kt_ref[0], preferred_element_type=F32)
         + jnp.dot(spr[...].reshape(n_rows * nb, ns).astype(BF16), w2r_ref[0], preferred_element_type=F32)
         + jnp.dot(spi[...].reshape(n_rows * nb, ns).astype(BF16), w2i_ref[0], preferred_element_type=F32))
    yc[...] = y.reshape(n_rows, nb, chunk * LANES)
    for b in range(nb):
        for tl in range(chunk):
            yb[b, pl.ds(tl, n_rows, stride=chunk), :] = yc[:, b, tl * LANES:(tl + 1) * LANES]
    o_ref[...] = jax.nn.gelu(yb[...] + d_ref[...] * u_ref[...]).astype(o_ref.dtype)


def s5_sequence(u, s0_re, s0_im, d_skip, mats, chunk, t_seg):
    b, t, n = u.shape
    g, p = s0_re.shape[1:]
    nq = n // LANES
    ns = GROUPS_PER_TILE * p
    cw = chunk * LANES
    n_rows = t_seg // chunk
    wspec = lambda r, c: pl.BlockSpec((1, r, c), lambda q, s: (q, 0, 0))
    state_spec = pl.BlockSpec((b, ns), lambda q, s: (0, q))
    y, sr, si = pl.pallas_call(
        functools.partial(_s5_seq_kernel, chunk=chunk, n_rows=n_rows, nb=b),
        grid=(nq, t // t_seg),
        in_specs=[pl.BlockSpec((b, t_seg, LANES), lambda q, s: (0, s, q)),
                  wspec(cw, ns), wspec(cw, ns), wspec(cw, cw), wspec(ns, cw), wspec(ns, cw),
                  wspec(1, ns), wspec(1, ns), pl.BlockSpec((1, LANES), lambda q, s: (0, q)), state_spec, state_spec],
        out_specs=[pl.BlockSpec((b, t_seg, LANES), lambda q, s: (0, s, q)), state_spec, state_spec],
        out_shape=[jax.ShapeDtypeStruct((b, t, n), BF16), jax.ShapeDtypeStruct((b, g * p), F32),
                   jax.ShapeDtypeStruct((b, g * p), F32)],
        scratch_shapes=[pltpu.VMEM((n_rows, b, cw), F32),
                        pltpu.VMEM((n_rows, b, ns), F32), pltpu.VMEM((n_rows, b, ns), F32),
                        pltpu.VMEM((n_rows, b, ns), F32), pltpu.VMEM((n_rows, b, ns), F32),
                        pltpu.VMEM((n_rows, b, cw), F32), pltpu.VMEM((b, t_seg, LANES), F32),
                        pltpu.VMEM((b, ns), F32), pltpu.VMEM((b, ns), F32)],
        compiler_params=_cparams(("parallel", "arbitrary")),
        name="s5_sequence",
    )(u, mats["w1r"], mats["w1i"], mats["kt"], mats["w2r"], mats["w2i"], mats["ar"], mats["ai"],
      d_skip.reshape(1, n).astype(F32),
      s0_re.astype(F32).reshape(b, g * p), s0_im.astype(F32).reshape(b, g * p))
    return y, sr.reshape(b, g, p), si.reshape(b, g, p)


def _s5_step_kernel(u_ref, w1r_ref, w1i_ref, kt_ref, w2r_ref, w2i_ref, ar_ref, ai_ref, d_ref, s0r_ref, s0i_ref,
                    o_ref, sfr_ref, sfi_ref):
    dot = functools.partial(jnp.dot, preferred_element_type=F32, precision=lax.Precision.HIGHEST)
    x = u_ref[...]
    sr, si = s0r_ref[...], s0i_ref[...]
    ar, ai = ar_ref[0], ai_ref[0]
    sfr_ref[...] = ar * sr - ai * si + dot(x, w1r_ref[0])
    sfi_ref[...] = ar * si + ai * sr + dot(x, w1i_ref[0])
    y = dot(x, kt_ref[0]) + dot(sr, w2r_ref[0]) + dot(si, w2i_ref[0])
    o_ref[...] = jax.nn.gelu(y + d_ref[...] * x).astype(o_ref.dtype)


def s5_step(u, s0_re, s0_im, d_skip, mats):
    b, n = u.shape
    g, p = s0_re.shape[1:]
    nq = n // LANES
    ns = GROUPS_PER_TILE * p
    wspec = lambda r, c: pl.BlockSpec((1, r, c), lambda q: (q, 0, 0))
    state_spec = pl.BlockSpec((b, ns), lambda q: (0, q))
    lane_spec = pl.BlockSpec((b, LANES), lambda q: (0, q))
    y, sr, si = pl.pallas_call(
        _s5_step_kernel,
        grid=(nq,),
        in_specs=[lane_spec, wspec(LANES, ns), wspec(LANES, ns), wspec(LANES, LANES), wspec(ns, LANES),
                  wspec(ns, LANES), wspec(1, ns), wspec(1, ns), pl.BlockSpec((1, LANES), lambda q: (0, q)),
                  state_spec, state_spec],
        out_specs=[lane_spec, state_spec, state_spec],
        out_shape=[jax.ShapeDtypeStruct((b, n), BF16), jax.ShapeDtypeStruct((b, g * p), F32),
                   jax.ShapeDtypeStruct((b, g * p), F32)],
        compiler_params=_cparams(("parallel",)),
        name="s5_step",
    )(u, mats["w1r"], mats["w1i"], mats["kt"], mats["w2r"], mats["w2i"], mats["ar"], mats["ai"],
      d_skip.reshape(1, n).astype(F32), s0_re.astype(F32).reshape(b, g * p), s0_im.astype(F32).reshape(b, g * p))
    return y, sr.reshape(b, g, p), si.reshape(b, g, p)


def _glu_kernel(y_ref, w_ref, o_ref):
    n = o_ref.shape[-1]
    z = jnp.dot(y_ref[...], w_ref[...], preferred_element_type=F32)
    o_ref[...] = (z[:, :n] * jax.nn.sigmoid(z[:, n:])).astype(o_ref.dtype)


def ssm_glu(y, w_glu, tm):
    m, n = y.shape
    return pl.pallas_call(
        _glu_kernel,
        grid=(m // tm,),
        in_specs=[pl.BlockSpec((tm, n), lambda i: (i, 0)), pl.BlockSpec((n, 2 * n), lambda i: (0, 0))],
        out_specs=pl.BlockSpec((tm, n), lambda i: (i, 0)),
        out_shape=jax.ShapeDtypeStruct((m, n), BF16),
        compiler_params=_cparams(("parallel",)),
        name="ssm_glu",
    )(y, w_glu)


def _sortable_key(score):
    bits = pltpu.bitcast(score, jnp.int32)
    return jnp.where(bits < 0, bits ^ jnp.int32(0x7FFFFFFF), bits)


def _kth_largest_key(key, k, axis=-1):
    shape = list(key.shape)
    shape[axis] = 1

    def body(i, res):
        cand = res + jnp.left_shift(jnp.int32(1), 31 - i)
        cnt = jnp.sum(jnp.where(key >= cand, 1.0, 0.0), axis=axis, keepdims=True)
        return jnp.where(cnt >= float(k), cand, res)

    return lax.fori_loop(0, 32, body, jnp.full(tuple(shape), INT_MIN, jnp.int32))


def _exclusive_cumsum_rows(flags_bf16):
    n, cols = flags_bf16.shape
    r = lax.broadcasted_iota(jnp.int32, (LANES, LANES), 0)
    c = lax.broadcasted_iota(jnp.int32, (LANES, LANES), 1)
    tri = jnp.where(c < r, 1.0, 0.0).astype(BF16)
    off = jnp.zeros((1, cols), F32)
    out = []
    for i in range(n // LANES):
        chunk = flags_bf16[i * LANES:(i + 1) * LANES, :]
        out.append(jnp.dot(tri, chunk, preferred_element_type=F32) + off)
        off = off + jnp.sum(chunk.astype(F32), axis=0, keepdims=True)
    return jnp.concatenate(out, axis=0)


def _exclusive_cumsum_lanes(flags_bf16):
    rows, n = flags_bf16.shape
    r = lax.broadcasted_iota(jnp.int32, (LANES, LANES), 0)
    c = lax.broadcasted_iota(jnp.int32, (LANES, LANES), 1)
    tri = jnp.where(r < c, 1.0, 0.0).astype(BF16)
    off = jnp.zeros((rows, 1), F32)
    out = []
    for i in range(n // LANES):
        chunk = flags_bf16[:, i * LANES:(i + 1) * LANES]
        out.append(jnp.dot(chunk, tri, preferred_element_type=F32) + off)
        off = off + jnp.sum(chunk.astype(F32), axis=-1, keepdims=True)
    return jnp.concatenate(out, axis=-1)


def _attend_keys(qi_ref, ki_ref, w_ref, q_ref, k_ref, v_ref, o_ref, score_scr, bias_scr, *, qb, tq, lk, topk):
    n_chunks = lk // MXU_WIDTH
    qi_all = qi_ref[0].reshape(H_IDX * tq, D_IDX)
    w = w_ref[0] * (D_IDX ** -0.5)

    def chunk_body(c, carry):
        k0 = pl.multiple_of(c * MXU_WIDTH, MXU_WIDTH)
        s = lax.dot_general(ki_ref[0, pl.ds(k0, MXU_WIDTH), :], qi_all, NT_DIMS, preferred_element_type=F32)
        acc = jnp.zeros((MXU_WIDTH, tq), F32)
        for h in range(H_IDX):
            acc = acc + jnp.maximum(s[:, h * tq:(h + 1) * tq], 0.0) * w[h:h + 1, :]
        score_scr[c] = acc
        return carry

    lax.fori_loop(0, n_chunks, chunk_body, 0)
    score = jnp.concatenate([score_scr[c] for c in range(n_chunks)], axis=0)
    kpos = lax.broadcasted_iota(jnp.int32, (lk, tq), 0)
    qpos = qb * tq + lax.broadcasted_iota(jnp.int32, (lk, tq), 1)
    causal = kpos <= qpos
    key = _sortable_key(jnp.where(causal, score, -jnp.inf))
    thr = _kth_largest_key(key, topk, axis=0)
    gt = key > thr
    eq = jnp.logical_and(key == thr, causal)
    need = float(topk) - jnp.sum(jnp.where(gt, 1.0, 0.0), axis=0, keepdims=True)
    n_eq = jnp.sum(jnp.where(eq, 1.0, 0.0), axis=0, keepdims=True)
    bias_scr[:lk, :] = jnp.where(jnp.logical_and(causal, key >= thr), 0.0, NEG_INF)

    @pl.when(jnp.max(n_eq - need) > 0.0)
    def _():
        rank = _exclusive_cumsum_rows(jnp.where(eq, 1.0, 0.0).astype(BF16))
        keep = jnp.logical_or(gt, jnp.logical_and(eq, rank < need))
        bias_scr[:lk, :] = jnp.where(jnp.logical_and(causal, keep), 0.0, NEG_INF)

    bias = bias_scr[:lk, :]
    scale = HEAD_DIM ** -0.5
    for h in range(q_ref.shape[-1] // HEAD_DIM):
        sl = slice(h * HEAD_DIM, (h + 1) * HEAD_DIM)
        logits = lax.dot_general(k_ref[0, :lk, sl], q_ref[0, :, sl], NT_DIMS, preferred_element_type=F32) * scale + bias
        m = jnp.max(logits, axis=0, keepdims=True)
        p = jnp.exp(logits - m)
        den = jnp.sum(p, axis=0, keepdims=True)
        den_col = jnp.transpose(jnp.broadcast_to(den, (SUBLANES, tq)))[:, 0:1]
        o = lax.dot_general(p.astype(BF16), v_ref[0, :lk, sl], TN_DIMS, preferred_element_type=F32)
        o_ref[0, :, sl] = (o / den_col).astype(o_ref.dtype)


def _prompt_attn_kernel(qi_ref, ki_ref, w_ref, q_ref, k_ref, v_ref, o_ref, score_scr, bias_scr,
                        *, tq, seq, topk, n_buckets):
    qb = pl.program_id(1)
    per = (seq // tq) // n_buckets
    for bkt in range(n_buckets):
        body = functools.partial(_attend_keys, qi_ref, ki_ref, w_ref, q_ref, k_ref, v_ref, o_ref, score_scr, bias_scr,
                                 qb=qb, tq=tq, lk=(bkt + 1) * per * tq, topk=topk)
        pl.when(qb // per == bkt)(body)


def prompt_attention(qi, ki, wi, q, k, v, tq, n_buckets):
    b, t, n = q.shape
    topk = min(TOPK_MAX, t // 4)
    return pl.pallas_call(
        functools.partial(_prompt_attn_kernel, tq=tq, seq=t, topk=topk, n_buckets=n_buckets),
        grid=(b, t // tq),
        in_specs=[pl.BlockSpec((1, H_IDX, tq, D_IDX), lambda i, j: (i, 0, j, 0)),
                  pl.BlockSpec((1, t, D_IDX), lambda i, j: (i, 0, 0)),
                  pl.BlockSpec((1, H_IDX, tq), lambda i, j: (i, 0, j)),
                  pl.BlockSpec((1, tq, n), lambda i, j: (i, j, 0)),
                  pl.BlockSpec((1, t, n), lambda i, j: (i, 0, 0)),
                  pl.BlockSpec((1, t, n), lambda i, j: (i, 0, 0))],
        out_specs=pl.BlockSpec((1, tq, n), lambda i, j: (i, j, 0)),
        out_shape=jax.ShapeDtypeStruct((b, t, n), BF16),
        scratch_shapes=[pltpu.VMEM((t // MXU_WIDTH, MXU_WIDTH, tq), F32), pltpu.VMEM((t, tq), F32)],
        compiler_params=_cparams(("parallel", "parallel")),
        name="prompt_attn",
    )(qi, ki, wi, q, k, v)


def _sample_score_kernel(pt_ref, qi_ref, w_ref, kin_ref, cache_ref, o_ref, kbuf, sem, *, n_pages, n_batch):
    b = pl.program_id(0)
    past = n_pages * PAGE_SIZE

    def page_copy(bb, slot, pg):
        return pltpu.make_async_copy(cache_ref.at[pt_ref[bb, pg]],
                                     kbuf.at[slot, :, pl.ds(pl.multiple_of(pg * PAGE_SIZE, PAGE_SIZE), PAGE_SIZE)],
                                     sem.at[slot])

    def start_all(bb, slot):
        lax.fori_loop(0, n_pages, lambda pg, c: (page_copy(bb, slot, pg).start(), c)[1], 0)

    @pl.when(b == 0)
    def _():
        start_all(0, 0)

    @pl.when(b + 1 < n_batch)
    def _():
        start_all(b + 1, (b + 1) % 2)

    slot = b % 2
    lax.fori_loop(0, n_pages, lambda pg, c: (page_copy(b, slot, pg).wait(), c)[1], 0)

    qi = qi_ref[0]
    w = w_ref[0] * (D_IDX ** -0.5)
    step = 2048
    for c in range(past // step):
        kc = kbuf[slot, :, c * step:(c + 1) * step].astype(BF16)
        s = jnp.dot(qi, kc, preferred_element_type=F32)
        o_ref[0, :, c * step:(c + 1) * step] = jnp.sum(jnp.maximum(s, 0.0) * w, axis=0, keepdims=True)
    s_new = jnp.sum(qi.astype(F32) * kin_ref[0].astype(BF16).astype(F32), axis=-1, keepdims=True)
    s_new = jnp.sum(jnp.maximum(s_new, 0.0) * w, axis=0, keepdims=True)
    lane = lax.broadcasted_iota(jnp.int32, (1, LANES), 1)
    o_ref[0, :, past:] = jnp.where(lane == 0, s_new, -jnp.inf)


def sample_scores(page_table, qi, wi, ki_new, cache_idx_k):
    bsz, n_pages = page_table.shape
    past = n_pages * PAGE_SIZE
    grid_spec = pltpu.PrefetchScalarGridSpec(
        num_scalar_prefetch=1,
        grid=(bsz,),
        in_specs=[pl.BlockSpec((1, H_IDX, D_IDX), lambda i, pt: (i, 0, 0)),
                  pl.BlockSpec((1, H_IDX, 1), lambda i, pt: (i, 0, 0)),
                  pl.BlockSpec((1, 1, D_IDX), lambda i, pt: (i, 0, 0)),
                  pl.BlockSpec(memory_space=pl.ANY)],
        out_specs=pl.BlockSpec((1, 1, past + LANES), lambda i, pt: (i, 0, 0)),
        scratch_shapes=[pltpu.VMEM((2, D_IDX, past), F32), pltpu.SemaphoreType.DMA((2,))],
    )
    return pl.pallas_call(
        functools.partial(_sample_score_kernel, n_pages=n_pages, n_batch=bsz),
        grid_spec=grid_spec,
        out_shape=jax.ShapeDtypeStruct((bsz, 1, past + LANES), F32),
        compiler_params=_cparams(("arbitrary",)),
        name="sample_scores",
    )(page_table, qi, wi, ki_new, jnp.swapaxes(cache_idx_k, 1, 2))


def _sample_select_kernel(score_ref, pos_ref, o_ref, rank_scr, *, topk):
    score = score_ref[...]
    bsz, n = score.shape
    key = _sortable_key(score)
    thr = _kth_largest_key(key, topk)
    gt = key > thr
    eq = key == thr
    need = topk - jnp.sum(gt.astype(jnp.int32), axis=-1, keepdims=True)
    rank_eq = _exclusive_cumsum_lanes(jnp.where(eq, 1.0, 0.0).astype(BF16))
    sel = jnp.logical_or(gt, jnp.logical_and(eq, rank_eq < need.astype(F32)))
    self32 = jnp.where(sel, 1.0, 0.0)
    rank = _exclusive_cumsum_lanes(self32.astype(BF16))
    rank_scr[...] = jnp.where(sel, rank, -1.0)
    slot_id = lax.broadcasted_iota(jnp.int32, (topk, 1), 0).astype(F32)
    n_split = 3
    step = n // n_split

    def body(bb, c):
        acc = jnp.zeros((topk, LANES), F32)
        for s in range(n_split):
            onehot = jnp.where(rank_scr[pl.ds(bb, 1), s * step:(s + 1) * step] == slot_id, 1.0, 0.0).astype(BF16)
            acc = acc + jnp.dot(onehot, pos_ref[s * step:(s + 1) * step, :], preferred_element_type=F32)
        o_ref[bb] = acc
        return c

    lax.fori_loop(0, bsz, body, 0)


def sample_select(scores, topk):
    bsz, n = scores.shape
    s = jnp.arange(n, dtype=jnp.int32)
    lane = jnp.arange(LANES, dtype=jnp.int32)[None, :]
    pos = jnp.where(lane == 0, (s // PAGE_SIZE)[:, None], jnp.where(lane == 1, (s % PAGE_SIZE)[:, None], 0)).astype(BF16)
    return pl.pallas_call(
        functools.partial(_sample_select_kernel, topk=topk),
        out_shape=jax.ShapeDtypeStruct((bsz, topk, LANES), F32),
        scratch_shapes=[pltpu.VMEM((bsz, n), F32)],
        compiler_params=pltpu.CompilerParams(vmem_limit_bytes=VMEM_LIMIT),
        name="sample_select",
    )(scores, pos)


def _sample_attend_kernel(idx_ref, pt_ref, q_ref, knew_ref, vnew_ref, ck_ref, cv_ref, o_ref, kbuf, vbuf, ksem, vsem,
                          *, topk, past, n_batch):
    b = pl.program_id(0)

    def row_copies(bb, slot, r):
        i = jnp.minimum(idx_ref[bb, r], past - 1)
        phys = pt_ref[bb, i // PAGE_SIZE]
        off = i % PAGE_SIZE
        return (pltpu.make_async_copy(ck_ref.at[phys, off], kbuf.at[slot, r], ksem.at[slot]),
                pltpu.make_async_copy(cv_ref.at[phys, off], vbuf.at[slot, r], vsem.at[slot]))

    def start_all(bb, slot):
        def body(r, c):
            for cp in row_copies(bb, slot, r):
                cp.start()
            return c

        lax.fori_loop(0, topk, body, 0, unroll=8)

    @pl.when(b == 0)
    def _():
        start_all(0, 0)

    @pl.when(b + 1 < n_batch)
    def _():
        start_all(b + 1, (b + 1) % 2)

    slot = b % 2

    def wait_body(r, c):
        for cp in row_copies(b, slot, r):
            cp.wait()
        return c

    lax.fori_loop(0, topk, wait_body, 0, unroll=8)

    @pl.when(idx_ref[b, topk - 1] == past)
    def _():
        kbuf[slot, topk - 1] = knew_ref[0]
        vbuf[slot, topk - 1] = vnew_ref[0]

    q = q_ref[0]
    k = kbuf[slot]
    logits = jnp.sum(k * q[None], axis=-1, keepdims=True) * (HEAD_DIM ** -0.5)
    m = jnp.max(logits, axis=0, keepdims=True)
    p = jnp.exp(logits - m)
    den = jnp.sum(p, axis=0)
    o_ref[0] = jnp.sum(p * vbuf[slot], axis=0) / den


def sample_attend(idx, page_table, q, k_new, v_new, cache_k, cache_v):
    bsz, h, dh = q.shape
    topk = idx.shape[1]
    past = page_table.shape[1] * PAGE_SIZE
    grid_spec = pltpu.PrefetchScalarGridSpec(
        num_scalar_prefetch=2,
        grid=(bsz,),
        in_specs=[pl.BlockSpec((1, h, dh), lambda i, a, c: (i, 0, 0)), pl.BlockSpec((1, h, dh), lambda i, a, c: (i, 0, 0)),
                  pl.BlockSpec((1, h, dh), lambda i, a, c: (i, 0, 0)),
                  pl.BlockSpec(memory_space=pl.ANY), pl.BlockSpec(memory_space=pl.ANY)],
        out_specs=pl.BlockSpec((1, h, dh), lambda i, a, c: (i, 0, 0)),
        scratch_shapes=[pltpu.VMEM((2, topk, h, dh), F32), pltpu.VMEM((2, topk, h, dh), F32),
                        pltpu.SemaphoreType.DMA((2,)), pltpu.SemaphoreType.DMA((2,))],
    )
    return pl.pallas_call(
        functools.partial(_sample_attend_kernel, topk=topk, past=past, n_batch=bsz),
        grid_spec=grid_spec,
        out_shape=jax.ShapeDtypeStruct((bsz, h, dh), F32),
        compiler_params=_cparams(("arbitrary",)),
        name="sample_attend",
    )(idx, page_table, q, k_new, v_new, cache_k, cache_v)


def _merge_kernel(h_ref, s_ref, a_ref, x_ref, wgs_ref, wga_ref, wbs_ref, wba_ref, wo_ref, gn_ref, o_ref, hn_ref,
                  *, n_tiles):
    j = pl.program_id(1)
    dot = functools.partial(jnp.dot, preferred_element_type=F32)
    tm = h_ref.shape[0]
    sub = min(tm, MERGE_SUB_ROWS)

    @pl.when(j == 0)
    def _():
        o_ref[...] = x_ref[...]

    for r0 in range(0, tm, sub):
        rows = slice(r0, r0 + sub)
        h = h_ref[rows, :]
        merged = (jax.nn.sigmoid(dot(h, wgs_ref[0])) * dot(s_ref[rows, :], wbs_ref[0])
                  + jax.nn.sigmoid(dot(h, wga_ref[0])) * dot(a_ref[rows, :], wba_ref[0]))
        o_ref[rows, :] += dot(merged.astype(BF16), wo_ref[...])

    @pl.when(j == n_tiles - 1)
    def _():
        for r0 in range(0, tm, sub):
            rows = slice(r0, r0 + sub)
            x1 = o_ref[rows, :]
            ms = jnp.mean(x1 * x1, axis=-1, keepdims=True)
            hn_ref[rows, :] = (x1 * lax.rsqrt(ms + RMS_EPS) * gn_ref[...]).astype(hn_ref.dtype)


def merge_branches(h, ssm_out, att, x2d, w_gates, w_bs, w_ba, w_out, g_next, tm, tn):
    m, d = x2d.shape
    kdim = ssm_out.shape[1]
    nj = d // tn
    row = lambda c: pl.BlockSpec((tm, c), lambda i, j: (i, 0))
    return pl.pallas_call(
        functools.partial(_merge_kernel, n_tiles=nj),
        grid=(m // tm, nj),
        in_specs=[row(d), row(kdim), row(kdim), row(d),
                  pl.BlockSpec((1, d, tn), lambda i, j: (j, 0, 0)), pl.BlockSpec((1, d, tn), lambda i, j: (j + nj, 0, 0)),
                  pl.BlockSpec((1, kdim, tn), lambda i, j: (j, 0, 0)), pl.BlockSpec((1, kdim, tn), lambda i, j: (j, 0, 0)),
                  pl.BlockSpec((tn, d), lambda i, j: (j, 0)), pl.BlockSpec((1, d), lambda i, j: (0, 0))],
        out_specs=[row(d), row(d)],
        out_shape=[jax.ShapeDtypeStruct((m, d), F32), jax.ShapeDtypeStruct((m, d), BF16)],
        compiler_params=_cparams(("parallel", "arbitrary")),
        name="merge_branches",
    )(h, ssm_out, att, x2d, w_gates, w_gates, w_bs, w_ba, w_out, g_next.reshape(1, d).astype(F32))


def _ffn_step_kernel(x_ref, gf_ref, gfin_ref, wa_ref, wb_ref, cwa_ref, cwb_ref, cba_ref, cbb_ref, wd_ref,
                     p2a_ref, p1a_ref, p2b_ref, p1b_ref, y_ref, ca_ref, cb_ref, h_scr, acc_scr, *, n_ff_tiles):
    j = pl.program_id(1)

    @pl.when(j == 0)
    def _():
        x = x_ref[0]
        ms = jnp.mean(x * x, axis=-1, keepdims=True)
        h_scr[...] = (x * lax.rsqrt(ms + RMS_EPS) * gf_ref[...]).astype(BF16)
        acc_scr[...] = jnp.zeros_like(acc_scr)

    h = h_scr[...]

    def conv_half(w_ref, cw_ref, cbias_ref, out_ref, p2_ref, p1_ref):
        up = jnp.dot(h, w_ref[0], preferred_element_type=F32)
        cw = cw_ref[...]
        out_ref[...] = up
        return cbias_ref[...] + cw[0:1, :] * p2_ref[...] + cw[1:2, :] * p1_ref[...] + cw[2:3, :] * up

    a = conv_half(wa_ref, cwa_ref, cba_ref, ca_ref, p2a_ref, p1a_ref)
    g = conv_half(wb_ref, cwb_ref, cbb_ref, cb_ref, p2b_ref, p1b_ref)
    act = (jax.nn.silu(a) * g).astype(BF16)
    acc_scr[...] += jnp.dot(act, wd_ref[...], preferred_element_type=F32)

    @pl.when(j == n_ff_tiles - 1)
    def _():
        x2 = x_ref[0] + acc_scr[...]
        ms = jnp.mean(x2 * x2, axis=-1, keepdims=True)
        y_ref[0] = x2 * lax.rsqrt(ms + RMS_EPS) * gfin_ref[...]


def conv_ffn_step(x, conv_prev, g_ffn, g_final, w_up, conv_w, conv_b, w_down, tm):
    b, t, d = x.shape
    tf = w_up.shape[2]
    f = w_down.shape[0]
    nf = f // tf
    gf = g_ffn.reshape(1, d).astype(F32)
    gfin = g_final.reshape(1, d).astype(F32)
    cb2 = conv_b.reshape(1, 2 * f).astype(F32)
    conv_w = conv_w.astype(F32)
    kern = functools.partial(_ffn_step_kernel, n_ff_tiles=nf)
    p2, p1 = conv_prev[:, 0, :], conv_prev[:, 1, :]
    wspec = lambda blk, off: pl.BlockSpec(blk, lambda i, j: (0, j + off))
    upspec = lambda off: pl.BlockSpec((1, d, tf), lambda i, j: (j + off, 0, 0))
    in_specs = [pl.BlockSpec((1, tm, d), lambda i, j: (0, i, 0)),
                pl.BlockSpec((1, d), lambda i, j: (0, 0)), pl.BlockSpec((1, d), lambda i, j: (0, 0)),
                upspec(0), upspec(nf), wspec((CONV_W, tf), 0), wspec((CONV_W, tf), nf),
                wspec((1, tf), 0), wspec((1, tf), nf),
                pl.BlockSpec((tf, d), lambda i, j: (j, 0)),
                pl.BlockSpec((tm, tf), lambda i, j: (i, j)), pl.BlockSpec((tm, tf), lambda i, j: (i, j)),
                pl.BlockSpec((tm, tf), lambda i, j: (i, j + nf)), pl.BlockSpec((tm, tf), lambda i, j: (i, j + nf))]
    out_specs = [pl.BlockSpec((1, tm, d), lambda i, j: (0, i, 0)),
                 pl.BlockSpec((tm, tf), lambda i, j: (i, j)), pl.BlockSpec((tm, tf), lambda i, j: (i, j))]
    out_shape = [jax.ShapeDtypeStruct((1, t, d), F32), jax.ShapeDtypeStruct((t, f), F32),
                 jax.ShapeDtypeStruct((t, f), F32)]
    scratch = [pltpu.VMEM((tm, d), BF16), pltpu.VMEM((tm, d), F32)]
    y, ua, ub = pl.pallas_call(
        kern, grid=(t // tm, nf), in_specs=in_specs, out_specs=out_specs, out_shape=out_shape,
        scratch_shapes=scratch, compiler_params=_cparams(("arbitrary", "arbitrary")),
        name="conv_ffn_step",
    )(x, gf, gfin, w_up, w_up, conv_w, conv_w, cb2, cb2, w_down, p2, p1, p2, p1)
    return y, jnp.stack([p1, jnp.concatenate([ua, ub], axis=-1)], axis=1)


def _ffn_seq_kernel(h_ref, xt_ref, gfin_ref, wa_ref, wb_ref, cwa_ref, cwb_ref, cba_ref, cbb_ref, wd_ref, pa_ref, pb_ref,
                    y_ref, ca_ref, cb_ref, act_scr, car_a, car_b, *, tm, tf, nf, tn, nd):
    i = pl.program_id(1)
    j = pl.program_id(2)

    @pl.when(j < nf)
    def _():
        sub = min(tm, FFN_SUB_ROWS)
        row = lax.broadcasted_iota(jnp.int32, (sub, tf), 0)

        def conv_half(w_ref, cw_ref, cbias_ref, out_ref, prev_ref, car):
            cw = cw_ref[...]
            cbias = cbias_ref[...]

            @pl.when(i == 0)
            def _():
                car[j] = prev_ref[0]

            c2 = car[j]
            rm2, rm1 = c2[0:1, :], c2[1:2, :]
            outs = []
            for r0 in range(0, tm, sub):
                up = jnp.dot(h_ref[0, r0:r0 + sub, :], w_ref[0], preferred_element_type=F32)
                p1 = jnp.where(row == 0, rm1, pltpu.roll(up, 1, 0))
                p2 = jnp.where(row == 0, rm2, jnp.where(row == 1, rm1, pltpu.roll(up, 2, 0)))
                outs.append(cbias + cw[0:1, :] * p2 + cw[1:2, :] * p1 + cw[2:3, :] * up)
                rm2, rm1 = up[sub - 2:sub - 1, :], up[sub - 1:sub, :]
            last2 = jnp.concatenate([rm2, rm1], axis=0)
            car[j] = last2
            out_ref[0, 0] = last2
            return outs

        a = conv_half(wa_ref, cwa_ref, cba_ref, ca_ref, pa_ref, car_a)
        g = conv_half(wb_ref, cwb_ref, cbb_ref, cb_ref, pb_ref, car_b)
        act = [(jax.nn.silu(ar) * gr).astype(BF16) for ar, gr in zip(a, g)]
        for k in range(nf):

            @pl.when(j == k)
            def _(k=k):
                for r, act_r in enumerate(act):
                    act_scr[r * sub:(r + 1) * sub, k * tf:(k + 1) * tf] = act_r

    @pl.when(j >= nf)
    def _():
        sub = min(tm, 2 * FFN_SUB_ROWS)
        for r0 in range(0, tm, sub):
            part = xt_ref[0, r0:r0 + sub, :] + jnp.dot(act_scr[r0:r0 + sub, :], wd_ref[0], preferred_element_type=F32)
            for n in range(nd):

                @pl.when(j == nf + n)
                def _(n=n, part=part, r0=r0):
                    y_ref[0, r0:r0 + sub, n * tn:(n + 1) * tn] = part

    @pl.when(j == nf + nd - 1)
    def _():
        sub = min(tm, FFN_SUB_ROWS)
        for r0 in range(0, tm, sub):
            x2 = y_ref[0, r0:r0 + sub, :]
            ms = jnp.mean(x2 * x2, axis=-1, keepdims=True)
            y_ref[0, r0:r0 + sub, :] = x2 * lax.rsqrt(ms + RMS_EPS) * gfin_ref[...]


def conv_ffn_sequence(h, x, conv_prev, g_final, w_up, conv_w, conv_b, w_down, tm):
    b, t, d = x.shape
    tf = w_up.shape[2]
    nd, f, tn = w_down.shape
    nf, nt = f // tf, t // tm
    gfin = g_final.reshape(1, d).astype(F32)
    cb2 = conv_b.reshape(1, 2 * f).astype(F32)
    conv_w = conv_w.astype(F32)
    up_tile = lambda j: jnp.minimum(j, nf - 1)
    down_tile = lambda j: jnp.clip(j - nf, 0, nd - 1)
    wspec = lambda blk, off: pl.BlockSpec(blk, lambda bi, i, j: (0, up_tile(j) + off))
    upspec = lambda off: pl.BlockSpec((1, d, tf), lambda bi, i, j: (up_tile(j) + off, 0, 0))
    prev_spec = lambda off: pl.BlockSpec((1, CONV_W - 1, tf), lambda bi, i, j: (bi, 0, up_tile(j) + off))
    state_spec = pl.BlockSpec((1, 1, CONV_W - 1, tf), lambda bi, i, j: (bi, i, 0, up_tile(j)))
    y, ca, cb = pl.pallas_call(
        functools.partial(_ffn_seq_kernel, tm=tm, tf=tf, nf=nf, tn=tn, nd=nd),
        grid=(b, nt, nf + nd),
        in_specs=[pl.BlockSpec((1, tm, d), lambda bi, i, j: (bi, i, 0), pipeline_mode=pl.Buffered(1)),
                  pl.BlockSpec((1, tm, tn), lambda bi, i, j: (bi, i, down_tile(j))),
                  pl.BlockSpec((1, d), lambda bi, i, j: (0, 0)),
                  upspec(0), upspec(nf), wspec((CONV_W, tf), 0), wspec((CONV_W, tf), nf),
                  wspec((1, tf), 0), wspec((1, tf), nf),
                  pl.BlockSpec((1, f, tn), lambda bi, i, j: (down_tile(j), 0, 0)),
                  prev_spec(0), prev_spec(nf)],
        out_specs=[pl.BlockSpec((1, tm, d), lambda bi, i, j: (bi, i, 0), pipeline_mode=pl.Buffered(1)),
                   state_spec, state_spec],
        out_shape=[jax.ShapeDtypeStruct((b, t, d), F32), jax.ShapeDtypeStruct((b, nt, CONV_W - 1, f), F32),
                   jax.ShapeDtypeStruct((b, nt, CONV_W - 1, f), F32)],
        scratch_shapes=[pltpu.VMEM((tm, f), BF16),
                        pltpu.VMEM((nf, CONV_W - 1, tf), F32), pltpu.VMEM((nf, CONV_W - 1, tf), F32)],
        compiler_params=_cparams(("arbitrary", "arbitrary", "arbitrary")),
        name="conv_ffn_seq",
    )(h, x, gfin, w_up, w_up, conv_w, conv_w, cb2, cb2, w_down, conv_prev, conv_prev)
    return y, jnp.concatenate([ca[:, -1], cb[:, -1]], axis=-1)


def column_tiles(w, tn):
    k, n = w.shape
    return w.astype(BF16).reshape(k, n // tn, tn).transpose(1, 0, 2)


def _split_w_in(w_in, d_model, n_ssm, n_att):
    sizes = [n_ssm, n_att, n_att, n_att, H_IDX * D_IDX, D_IDX, H_IDX, d_model, d_model]
    offs = np.concatenate([[0], np.cumsum(sizes)]).tolist()
    col = lambda i: w_in[:, offs[i]:offs[i + 1]]
    w_uv = column_tiles(jnp.concatenate([col(0), col(3)], axis=1), COL_TILE)
    w_qk = column_tiles(jnp.concatenate([col(1), col(2)], axis=1), COL_TILE)
    w_qi = column_tiles(col(4), COL_TILE)
    pad = jnp.zeros((w_in.shape[0], LANES - D_IDX - H_IDX), w_in.dtype)
    w_kw = column_tiles(jnp.concatenate([col(5), col(6), pad], axis=1), LANES)
    w_g = column_tiles(jnp.concatenate([col(7), col(8)], axis=1), COL_TILE)
    return w_uv, w_qk, w_qi, w_kw, w_g


def _mix_inputs(x2d, pos, n_pos_tiles, tm, norm_mix, w_groups, n_ssm, n_att, sequence):
    w_uv, w_qk, w_qi, w_kw, w_g = w_groups
    h = rmsnorm_bf16(x2d, norm_mix, tm)
    tn = COL_TILE
    nu, na = n_ssm // tn, n_att // tn
    half_qk = HEAD_DIM // ROPE_FRACTION // 2
    half_idx = D_IDX // ROPE_FRACTION // 2
    out = {}
    if sequence:
        out["u"], out["v"], out["v16"] = project(
            h, w_uv, tm, tn, [(0, nu, F32, "flat"), (nu, na, F32, "flat"), (nu, na, BF16, "flat")])
        out["q16"], out["k"], out["k16"] = project(
            h, w_qk, tm, tn, [(0, na, BF16, "flat"), (na, na, F32, "flat"), (na, na, BF16, "flat")],
            "rope", half_qk, rope_tables(pos, HEAD_DIM, tn), n_pos_tiles)
        out["qi16"], = project(h, w_qi, tm, tn, [(0, H_IDX * D_IDX // tn, BF16, "heads")],
                               "rope", half_idx, rope_tables(pos, D_IDX, tn), n_pos_tiles)
    else:
        out["u"], out["v"] = project(h, w_uv, tm, tn, [(0, nu, F32, "flat"), (nu, na, F32, "flat")])
        out["q"], out["k"] = project(h, w_qk, tm, tn, [(0, na, F32, "flat"), (na, na, F32, "flat")],
                                     "rope", half_qk, rope_tables(pos, HEAD_DIM, tn), n_pos_tiles)
        out["qi"], = project(h, w_qi, tm, tn, [(0, H_IDX * D_IDX // tn, F32, "flat")],
                             "rope", half_idx, rope_tables(pos, D_IDX, tn), n_pos_tiles)
    extra = jnp.concatenate([jnp.full((H_IDX,), H_IDX ** -0.5, F32), jnp.zeros((LANES - D_IDX - H_IDX,), F32)])
    kw, = project(h, w_kw, tm, LANES, [(0, 1, F32, "flat")], "rope", half_idx,
                  rope_tables(pos, D_IDX, LANES, extra), n_pos_tiles)
    out["ki"], out["wi"] = kw[:, :D_IDX], kw[:, D_IDX:D_IDX + H_IDX]
    out["h"] = h
    return out


def kernel(x_prompt, x_sample, cache_k, cache_v, cache_idx_k, page_table, state_ssm_re, state_ssm_im,
           state_ffn_conv, norm_mix, w_in, ssm_A_re, ssm_A_im, ssm_log_dt, ssm_B_re, ssm_B_im, ssm_C_re,
           ssm_C_im, ssm_D, w_glu, w_branch_ssm, w_branch_att, w_out, norm_ffn, w_up, ffn_conv_w,
           ffn_conv_b, w_down, norm_final):
    bp, t, d = x_prompt.shape
    bs, ts, _ = x_sample.shape
    n_ssm = ssm_D.shape[0]
    n_att = w_branch_att.shape[0]
    g = n_ssm // SSM_GROUP
    h_att = n_att // HEAD_DIM
    past = page_table.shape[1] * PAGE_SIZE
    assert ts == 1, "sample group is a single decode step"
    assert bp == SUBLANES, "the S5 sequence kernel keeps one batch row per sublane"

    w_groups = _split_w_in(w_in, d, n_ssm, n_att)
    w_glu_b, w_out_b, w_down_b = (w.astype(BF16) for w in (w_glu, w_out, w_down))
    w_bs, w_ba = column_tiles(w_branch_ssm, COL_TILE), column_tiles(w_branch_att, COL_TILE)
    w_up_t, w_down_t = column_tiles(w_up, COL_TILE), column_tiles(w_down, COL_TILE)
    ssm_params = (ssm_A_re, ssm_A_im, ssm_log_dt, ssm_B_re, ssm_B_im, ssm_C_re, ssm_C_im)

    mp = bp * t
    tm = 1024
    xp2 = x_prompt.reshape(mp, d)
    pos_p = jnp.arange(t, dtype=jnp.int32)
    pr = _mix_inputs(xp2, pos_p, t // tm, tm, norm_mix, w_groups, n_ssm, n_att, True)
    zeros_s = jnp.zeros((bp, g, SSM_STATE), F32)
    y_act, s_re_p, s_im_p = s5_sequence(pr["u"].reshape(bp, t, n_ssm), zeros_s, zeros_s, ssm_D,
                                        ssm_tile_matrices(*ssm_params, SSM_CHUNK, BF16), SSM_CHUNK, 512)
    ssm_out = ssm_glu(y_act.reshape(mp, n_ssm), w_glu_b, 512)
    wi_t = pr["wi"].reshape(bp, t, H_IDX).transpose(0, 2, 1)
    att = prompt_attention(pr["qi16"], pr["ki"].astype(BF16).reshape(bp, t, D_IDX), wi_t,
                           pr["q16"].reshape(bp, t, n_att), pr["k16"].reshape(bp, t, n_att),
                           pr["v16"].reshape(bp, t, n_att), 256, 4)
    x1, h_ffn = merge_branches(pr["h"], ssm_out, att.reshape(mp, n_att), xp2, w_groups[4], w_bs, w_ba, w_out_b,
                               norm_ffn, 512, COL_TILE)
    zeros_c = jnp.zeros((bp, CONV_W - 1, w_up.shape[1]), F32)
    y_prompt, conv_p = conv_ffn_sequence(h_ffn.reshape(bp, t, d), x1.reshape(bp, t, d), zeros_c, norm_final, w_up_t,
                                         ffn_conv_w, ffn_conv_b, w_down_t, 1024)
    k_p = pr["k"].reshape(bp, t, h_att, HEAD_DIM)
    v_p = pr["v"].reshape(bp, t, h_att, HEAD_DIM)
    ki_p = pr["ki"].reshape(bp, t, D_IDX)

    xs2 = x_sample.reshape(bs, d)
    pos_s = jnp.full((bs,), past, jnp.int32)
    sm = _mix_inputs(xs2, pos_s, 1, bs, norm_mix, w_groups, n_ssm, n_att, False)
    y_act, s_re_s, s_im_s = s5_step(sm["u"], state_ssm_re, state_ssm_im, ssm_D, ssm_tile_matrices(*ssm_params, 1, F32))
    ssm_out = ssm_glu(y_act, w_glu_b, bs)
    scores = sample_scores(page_table, sm["qi"].astype(BF16).reshape(bs, H_IDX, D_IDX), sm["wi"].reshape(bs, H_IDX, 1),
                           sm["ki"].reshape(bs, 1, D_IDX), cache_idx_k)
    topk = min(TOPK_MAX, (past + 1) // 4)
    hl = sample_select(scores.reshape(bs, past + LANES), topk)
    idx = (hl[:, :, 0] * PAGE_SIZE + hl[:, :, 1]).astype(jnp.int32)
    att = sample_attend(idx, page_table, sm["q"].reshape(bs, h_att, HEAD_DIM), sm["k"].reshape(bs, h_att, HEAD_DIM),
                        sm["v"].reshape(bs, h_att, HEAD_DIM), cache_k, cache_v)
    x1, _ = merge_branches(sm["h"], ssm_out, att.reshape(bs, n_att).astype(BF16), xs2, w_groups[4], w_bs, w_ba, w_out_b,
                           norm_ffn, bs, 512)
    y_s, conv_s = conv_ffn_step(x1.reshape(1, bs, d), state_ffn_conv, norm_ffn, norm_final, w_up_t, ffn_conv_w,
                                ffn_conv_b, w_down_b, bs)
    y_sample = y_s.reshape(bs, 1, d)
    k_s = sm["k"].reshape(bs, 1, h_att, HEAD_DIM)
    v_s = sm["v"].reshape(bs, 1, h_att, HEAD_DIM)
    ki_s = sm["ki"].reshape(bs, 1, D_IDX)

    return (y_prompt, y_sample, k_p, v_p, ki_p, k_s, v_s, ki_s, s_re_p, s_im_p, s_re_s, s_im_s, conv_p, conv_s)
```

```python
import functools

import numpy as np
import jax
import jax.numpy as jnp
from jax import lax
from jax.experimental import pallas as pl
from jax.experimental.pallas import tpu as pltpu

F32 = jnp.float32
BF16 = jnp.bfloat16

HEAD_DIM = 128
SSM_GROUP = 16
SSM_STATE = 64
H_IDX = 16
D_IDX = 64
TOPK_MAX = 256
ROPE_THETA = 500000.0
ROPE_FRACTION = 4
CONV_W = 3
RMS_EPS = 1e-6
NEG_INF = -1e30
PAGE_SIZE = 128

LANES = 128
SUBLANES = 8
MXU_WIDTH = 256
GROUPS_PER_TILE = LANES // SSM_GROUP
SSM_CHUNK = 8
PROJ_SUB_ROWS = 256
FFN_SUB_ROWS = 256
COL_TILE = 512
MERGE_SUB_ROWS = 256
VMEM_LIMIT = 56 * 1024 * 1024
INT_MIN = -2 ** 31
NT_DIMS = (((1,), (1,)), ((), ()))
TN_DIMS = (((0,), (0,)), ((), ()))


def _cparams(sem):
    return pltpu.CompilerParams(dimension_semantics=sem, vmem_limit_bytes=VMEM_LIMIT)


def _rmsnorm_kernel(x_ref, g_ref, o_ref):
    x = x_ref[...]
    ms = jnp.mean(x * x, axis=-1, keepdims=True)
    o_ref[...] = (x * lax.rsqrt(ms + RMS_EPS) * g_ref[...]).astype(o_ref.dtype)


def rmsnorm_bf16(x2d, g, tm):
    m, d = x2d.shape
    return pl.pallas_call(
        _rmsnorm_kernel,
        grid=(m // tm,),
        in_specs=[pl.BlockSpec((tm, d), lambda i: (i, 0)), pl.BlockSpec((1, d), lambda i: (0, 0))],
        out_specs=pl.BlockSpec((tm, d), lambda i: (i, 0)),
        out_shape=jax.ShapeDtypeStruct((m, d), BF16),
        compiler_params=_cparams(("parallel",)),
        name="rmsnorm",
    )(x2d, g.reshape(1, d))


def _proj_kernel(h_ref, w_ref, *rest, mode, shift, outs, n_tab, n_col_tiles):
    tabs, o_refs = rest[:n_tab], rest[n_tab:]
    j = pl.program_id(1)
    tm, tn = h_ref.shape[0], w_ref.shape[2]
    sub = min(tm, PROJ_SUB_ROWS)
    for r0 in range(0, tm, sub):
        rows = slice(r0, r0 + sub)
        z = jnp.dot(h_ref[rows, :], w_ref[0], preferred_element_type=F32)
        if mode == "rope":
            c_ref, s1_ref, s2_ref = tabs
            z = (z * c_ref[rows, :] + pltpu.roll(z, tn - shift, 1) * s1_ref[rows, :]
                 + pltpu.roll(z, shift, 1) * s2_ref[rows, :])
        for (start, cnt, _, kind), o_ref in zip(outs, o_refs):
            if kind == "heads":
                assert start + cnt == n_col_tiles
                for hh in range(tn // D_IDX):
                    o_ref[0, hh, rows, :] = z[:, hh * D_IDX:(hh + 1) * D_IDX].astype(o_ref.dtype)
            elif start + cnt == n_col_tiles:
                o_ref[rows, :] = z.astype(o_ref.dtype)
            else:
                o_ref[rows, :] = jnp.where(j >= start + cnt, o_ref[rows, :], z.astype(o_ref.dtype))


def project(h, w, tm, tn, outs, mode="plain", shift=0, tables=None, n_pos_tiles=1):
    m, k = h.shape
    n = w.shape[0] * tn
    in_specs = [pl.BlockSpec((tm, k), lambda i, j: (i, 0)), pl.BlockSpec((1, k, tn), lambda i, j: (j, 0, 0))]
    args = [h, w]
    tables = tables or ()
    for t in tables:
        in_specs.append(pl.BlockSpec((tm, tn), lambda i, j: (i % n_pos_tiles, 0)))
        args.append(t)
    out_specs, out_shape = [], []
    for start, cnt, dtype, kind in outs:
        if kind == "heads":
            hpt = tn // D_IDX
            out_specs.append(pl.BlockSpec(
                (1, hpt, tm, D_IDX),
                lambda i, j, s=start, c=cnt: (i // n_pos_tiles, jnp.clip(j - s, 0, c - 1), i % n_pos_tiles, 0)))
            out_shape.append(jax.ShapeDtypeStruct((m // (n_pos_tiles * tm), cnt * hpt, n_pos_tiles * tm, D_IDX), dtype))
        else:
            out_specs.append(pl.BlockSpec((tm, tn), lambda i, j, s=start, c=cnt: (i, jnp.clip(j - s, 0, c - 1))))
            out_shape.append(jax.ShapeDtypeStruct((m, cnt * tn), dtype))
    return pl.pallas_call(
        functools.partial(_proj_kernel, mode=mode, shift=shift, outs=tuple(outs), n_tab=len(tables),
                          n_col_tiles=n // tn),
        grid=(m // tm, n // tn),
        in_specs=in_specs,
        out_specs=out_specs,
        out_shape=out_shape,
        compiler_params=_cparams(("parallel", "arbitrary")),
        name="proj_" + mode,
    )(*args)


def rope_tables(pos, head, tn, extra=None):
    r = head // ROPE_FRACTION
    half = r // 2
    inv = ROPE_THETA ** (-jnp.arange(half, dtype=F32) * 2.0 / r)
    ang = pos.astype(F32)[:, None] * inv[None, :]
    cos, sin = jnp.cos(ang), jnp.sin(ang)
    t = pos.shape[0]
    zeros = jnp.zeros((t, head - r), F32)
    zh = jnp.zeros((t, half), F32)
    c = jnp.concatenate([cos, cos, jnp.ones((t, head - r), F32)], axis=1)
    s1 = jnp.concatenate([-sin, zh, zeros], axis=1)
    s2 = jnp.concatenate([zh, sin, zeros], axis=1)
    if extra is None:
        reps = tn // head
        return tuple(jnp.tile(a, (1, reps)) for a in (c, s1, s2))
    pad = jnp.zeros((t, tn - head), F32)
    return (jnp.concatenate([c, jnp.broadcast_to(extra[None, :], (t, tn - head))], axis=1),
            jnp.concatenate([s1, pad], axis=1), jnp.concatenate([s2, pad], axis=1))


def ssm_tile_matrices(a_re, a_im, log_dt, b_re, b_im, c_re, c_im, chunk, dtype):
    hp = lax.Precision.HIGHEST
    g, p = a_re.shape
    gt = GROUPS_PER_TILE
    nq = g // gt
    a = lax.complex(a_re.astype(F32), a_im.astype(F32))
    dt = jnp.exp(log_dt.astype(F32))[:, None]
    adt = a * dt
    a_bar = jnp.exp(adt)
    b_bar = ((a_bar - 1.0) / a)[..., None] * lax.complex(b_re.astype(F32), b_im.astype(F32))
    cc = lax.complex(c_re.astype(F32), c_im.astype(F32))
    steps = jnp.arange(chunk + 1, dtype=F32)
    pw = jnp.exp(adt[:, None, :] * steps[None, :, None].astype(jnp.complex64))
    kd = jnp.real(jnp.einsum("gcp,gdp,gpe->gdce", cc, pw[:, :chunk], b_bar, precision=hp))
    w1g = pw[:, chunk - 1 - jnp.arange(chunk)][:, :, None, :] * b_bar.transpose(0, 2, 1)[:, None]
    m2g = cc.transpose(0, 2, 1)[:, :, None, :] * pw[:, 1:chunk + 1].transpose(0, 2, 1)[:, :, :, None]
    kdc = kd.reshape(nq, gt, chunk, SSM_GROUP, SSM_GROUP).transpose(0, 2, 4, 1, 3).reshape(nq, chunk, SSM_GROUP, LANES)
    w1c = w1g.reshape(nq, gt, chunk, SSM_GROUP, p).transpose(0, 2, 3, 1, 4).reshape(nq, chunk, SSM_GROUP, gt * p)
    m2c = m2g.reshape(nq, gt, p, chunk, SSM_GROUP).transpose(0, 3, 2, 1, 4).reshape(nq, chunk, p, LANES)
    al = pw[:, chunk].reshape(nq, 1, gt * p)
    kt, w1r, w1i, w2r, w2i = _ssm_expand(kdc, jnp.real(w1c), jnp.imag(w1c), jnp.real(m2c), -jnp.imag(m2c), chunk, dtype)
    return dict(w1r=w1r, w1i=w1i, kt=kt, w2r=w2r, w2i=w2i, ar=jnp.real(al), ai=jnp.imag(al))


def _ssm_expand_kernel(kd_ref, w1r_ref, w1i_ref, w2r_ref, w2i_ref, kt_o, w1r_o, w1i_o, w2r_o, w2i_o, *, chunk):
    gt, p = GROUPS_PER_TILE, SSM_STATE

    def same_group(rows_per_group, cols_per_group):
        shape = (gt * rows_per_group, gt * cols_per_group)
        r = jnp.right_shift(lax.broadcasted_iota(jnp.int32, shape, 0), rows_per_group.bit_length() - 1)
        c = jnp.right_shift(lax.broadcasted_iota(jnp.int32, shape, 1), cols_per_group.bit_length() - 1)
        return r == c

    def block_diag(small, mask, dtype):
        return jnp.where(mask, jnp.tile(small, (gt, 1)), 0.0).astype(dtype)

    m_cc, m_cp, m_pc = same_group(SSM_GROUP, SSM_GROUP), same_group(SSM_GROUP, p), same_group(p, SSM_GROUP)
    for j in range(chunk):
        rows = slice(j * LANES, (j + 1) * LANES)
        w1r_o[0, rows, :] = block_diag(w1r_ref[0, j], m_cp, w1r_o.dtype)
        w1i_o[0, rows, :] = block_diag(w1i_ref[0, j], m_cp, w1i_o.dtype)
        w2r_o[0, :, rows] = block_diag(w2r_ref[0, j], m_pc, w2r_o.dtype)
        w2i_o[0, :, rows] = block_diag(w2i_ref[0, j], m_pc, w2i_o.dtype)
        for t in range(chunk):
            cols = slice(t * LANES, (t + 1) * LANES)
            if t >= j:
                kt_o[0, rows, cols] = block_diag(kd_ref[0, t - j], m_cc, kt_o.dtype)
            else:
                kt_o[0, rows, cols] = jnp.zeros((LANES, LANES), kt_o.dtype)


def _ssm_expand(kdc, w1r, w1i, w2r, w2i, chunk, dtype):
    nq = kdc.shape[0]
    ns = w1r.shape[-1]
    cw = chunk * LANES
    spec4 = lambda a: pl.BlockSpec((1,) + a.shape[1:], lambda q: (q, 0, 0, 0))
    spec3 = lambda r, c: pl.BlockSpec((1, r, c), lambda q: (q, 0, 0))
    shapes = [(cw, cw), (cw, ns), (cw, ns), (ns, cw), (ns, cw)]
    return pl.pallas_call(
        functools.partial(_ssm_expand_kernel, chunk=chunk),
        grid=(nq,),
        in_specs=[spec4(a) for a in (kdc, w1r, w1i, w2r, w2i)],
        out_specs=[spec3(r, c) for r, c in shapes],
        out_shape=[jax.ShapeDtypeStruct((nq, r, c), dtype) for r, c in shapes],
        compiler_params=_cparams(("parallel",)),
        name="ssm_expand",
    )(kdc, w1r, w1i, w2r, w2i)


def _s5_seq_kernel(u_ref, w1r_ref, w1i_ref, kt_ref, w2r_ref, w2i_ref, ar_ref, ai_ref, d_ref, s0r_ref, s0i_ref,
                   o_ref, sfr_ref, sfi_ref, ucat, slr, sli, spr, spi, yc, yb, st_r, st_i, *, chunk, n_rows, nb):
    ts = pl.program_id(1)
    ns = st_r.shape[-1]

    @pl.when(ts == 0)
    def _():
        st_r[...] = s0r_ref[...]
        st_i[...] = s0i_ref[...]

    for b in range(nb):
        for tl in range(chunk):
            ucat[:, b, tl * LANES:(tl + 1) * LANES] = u_ref[b, pl.ds(tl, n_rows, stride=chunk), :]
    x = ucat[...].reshape(n_rows * nb, chunk * LANES).astype(BF16)
    slr[...] = jnp.dot(x, w1r_ref[0], preferred_element_type=F32).reshape(n_rows, nb, ns)
    sli[...] = jnp.dot(x, w1i_ref[0], preferred_element_type=F32).reshape(n_rows, nb, ns)
    ar = jnp.broadcast_to(ar_ref[0], (nb, ns))
    ai = jnp.broadcast_to(ai_ref[0], (nb, ns))

    def step(n, carry):
        sr, si = carry
        spr[n] = sr
        spi[n] = si
        return ar * sr - ai * si + slr[n], ar * si + ai * sr + sli[n]

    sr, si = lax.fori_loop(0, n_rows, step, (st_r[...], st_i[...]))
    st_r[...] = sr
    st_i[...] = si
    sfr_ref[...] = sr
    sfi_ref[...] = si
    y = (jnp.dot(x, kt_ref[0], preferred_element_type=F32)
         + jnp.dot(spr[...].reshape(n_rows * nb, ns).astype(BF16), w2r_ref[0], preferred_element_type=F32)
         + jnp.dot(spi[...].reshape(n_rows * nb, ns).astype(BF16), w2i_ref[0], preferred_element_type=F32))
    yc[...] = y.reshape(n_rows, nb, chunk * LANES)
    for b in range(nb):
        for tl in range(chunk):
            yb[b, pl.ds(tl, n_rows, stride=chunk), :] = yc[:, b, tl * LANES:(tl + 1) * LANES]
    o_ref[...] = jax.nn.gelu(yb[...] + d_ref[...] * u_ref[...]).astype(o_ref.dtype)


def s5_sequence(u, s0_re, s0_im, d_skip, mats, chunk, t_seg):
    b, t, n = u.shape
    g, p = s0_re.shape[1:]
    nq = n // LANES
    ns = GROUPS_PER_TILE * p
    cw = chunk * LANES
    n_rows = t_seg // chunk
    wspec = lambda r, c: pl.BlockSpec((1, r, c), lambda q, s: (q, 0, 0))
    state_spec = pl.BlockSpec((b, ns), lambda q, s: (0, q))
    y, sr, si = pl.pallas_call(
        functools.partial(_s5_seq_kernel, chunk=chunk, n_rows=n_rows, nb=b),
        grid=(nq, t // t_seg),
        in_specs=[pl.BlockSpec((b, t_seg, LANES), lambda q, s: (0, s, q)),
                  wspec(cw, ns), wspec(cw, ns), wspec(cw, cw), wspec(ns, cw), wspec(ns, cw),
                  wspec(1, ns), wspec(1, ns), pl.BlockSpec((1, LANES), lambda q, s: (0, q)), state_spec, state_spec],
        out_specs=[pl.BlockSpec((b, t_seg, LANES), lambda q, s: (0, s, q)), state_spec, state_spec],
        out_shape=[jax.ShapeDtypeStruct((b, t, n), BF16), jax.ShapeDtypeStruct((b, g * p), F32),
                   jax.ShapeDtypeStruct((b, g * p), F32)],
        scratch_shapes=[pltpu.VMEM((n_rows, b, cw), F32),
                        pltpu.VMEM((n_rows, b, ns), F32), pltpu.VMEM((n_rows, b, ns), F32),
                        pltpu.VMEM((n_rows, b, ns), F32), pltpu.VMEM((n_rows, b, ns), F32),
                        pltpu.VMEM((n_rows, b, cw), F32), pltpu.VMEM((b, t_seg, LANES), F32),
                        pltpu.VMEM((b, ns), F32), pltpu.VMEM((b, ns), F32)],
        compiler_params=_cparams(("parallel", "arbitrary")),
        name="s5_sequence",
    )(u, mats["w1r"], mats["w1i"], mats["kt"], mats["w2r"], mats["w2i"], mats["ar"], mats["ai"],
      d_skip.reshape(1, n).astype(F32),
      s0_re.astype(F32).reshape(b, g * p), s0_im.astype(F32).reshape(b, g * p))
    return y, sr.reshape(b, g, p), si.reshape(b, g, p)


def _s5_step_kernel(u_ref, w1r_ref, w1i_ref, kt_ref, w2r_ref, w2i_ref, ar_ref, ai_ref, d_ref, s0r_ref, s0i_ref,
                    o_ref, sfr_ref, sfi_ref):
    dot = functools.partial(jnp.dot, preferred_element_type=F32, precision=lax.Precision.HIGHEST)
    x = u_ref[...]
    sr, si = s0r_ref[...], s0i_ref[...]
    ar, ai = ar_ref[0], ai_ref[0]
    sfr_ref[...] = ar * sr - ai * si + dot(x, w1r_ref[0])
    sfi_ref[...] = ar * si + ai * sr + dot(x, w1i_ref[0])
    y = dot(x, kt_ref[0]) + dot(sr, w2r_ref[0]) + dot(si, w2i_ref[0])
    o_ref[...] = jax.nn.gelu(y + d_ref[...] * x).astype(o_ref.dtype)


def s5_step(u, s0_re, s0_im, d_skip, mats):
    b, n = u.shape
    g, p = s0_re.shape[1:]
    nq = n // LANES
    ns = GROUPS_PER_TILE * p
    wspec = lambda r, c: pl.BlockSpec((1, r, c), lambda q: (q, 0, 0))
    state_spec = pl.BlockSpec((b, ns), lambda q: (0, q))
    lane_spec = pl.BlockSpec((b, LANES), lambda q: (0, q))
    y, sr, si = pl.pallas_call(
        _s5_step_kernel,
        grid=(nq,),
        in_specs=[lane_spec, wspec(LANES, ns), wspec(LANES, ns), wspec(LANES, LANES), wspec(ns, LANES),
                  wspec(ns, LANES), wspec(1, ns), wspec(1, ns), pl.BlockSpec((1, LANES), lambda q: (0, q)),
                  state_spec, state_spec],
        out_specs=[lane_spec, state_spec, state_spec],
        out_shape=[jax.ShapeDtypeStruct((b, n), BF16), jax.ShapeDtypeStruct((b, g * p), F32),
                   jax.ShapeDtypeStruct((b, g * p), F32)],
        compiler_params=_cparams(("parallel",)),
        name="s5_step",
    )(u, mats["w1r"], mats["w1i"], mats["kt"], mats["w2r"], mats["w2i"], mats["ar"], mats["ai"],
      d_skip.reshape(1, n).astype(F32), s0_re.astype(F32).reshape(b, g * p), s0_im.astype(F32).reshape(b, g * p))
    return y, sr.reshape(b, g, p), si.reshape(b, g, p)


def _glu_kernel(y_ref, w_ref, o_ref):
    n = o_ref.shape[-1]
    z = jnp.dot(y_ref[...], w_ref[...], preferred_element_type=F32)
    o_ref[...] = (z[:, :n] * jax.nn.sigmoid(z[:, n:])).astype(o_ref.dtype)


def ssm_glu(y, w_glu, tm):
    m, n = y.shape
    return pl.pallas_call(
        _glu_kernel,
        grid=(m // tm,),
        in_specs=[pl.BlockSpec((tm, n), lambda i: (i, 0)), pl.BlockSpec((n, 2 * n), lambda i: (0, 0))],
        out_specs=pl.BlockSpec((tm, n), lambda i: (i, 0)),
        out_shape=jax.ShapeDtypeStruct((m, n), BF16),
        compiler_params=_cparams(("parallel",)),
        name="ssm_glu",
    )(y, w_glu)


def _sortable_key(score):
    bits = pltpu.bitcast(score, jnp.int32)
    return jnp.where(bits < 0, bits ^ jnp.int32(0x7FFFFFFF), bits)


def _kth_largest_key(key, k, axis=-1):
    shape = list(key.shape)
    shape[axis] = 1

    def body(i, res):
        cand = res + jnp.left_shift(jnp.int32(1), 31 - i)
        cnt = jnp.sum(jnp.where(key >= cand, 1.0, 0.0), axis=axis, keepdims=True)
        return jnp.where(cnt >= float(k), cand, res)

    return lax.fori_loop(0, 32, body, jnp.full(tuple(shape), INT_MIN, jnp.int32))


def _exclusive_cumsum_rows(flags_bf16):
    n, cols = flags_bf16.shape
    r = lax.broadcasted_iota(jnp.int32, (LANES, LANES), 0)
    c = lax.broadcasted_iota(jnp.int32, (LANES, LANES), 1)
    tri = jnp.where(c < r, 1.0, 0.0).astype(BF16)
    off = jnp.zeros((1, cols), F32)
    out = []
    for i in range(n // LANES):
        chunk = flags_bf16[i * LANES:(i + 1) * LANES, :]
        out.append(jnp.dot(tri, chunk, preferred_element_type=F32) + off)
        off = off + jnp.sum(chunk.astype(F32), axis=0, keepdims=True)
    return jnp.concatenate(out, axis=0)


def _exclusive_cumsum_lanes(flags_bf16):
    rows, n = flags_bf16.shape
    r = lax.broadcasted_iota(jnp.int32, (LANES, LANES), 0)
    c = lax.broadcasted_iota(jnp.int32, (LANES, LANES), 1)
    tri = jnp.where(r < c, 1.0, 0.0).astype(BF16)
    off = jnp.zeros((rows, 1), F32)
    out = []
    for i in range(n // LANES):
        chunk = flags_bf16[:, i * LANES:(i + 1) * LANES]
        out.append(jnp.dot(chunk, tri, preferred_element_type=F32) + off)
        off = off + jnp.sum(chunk.astype(F32), axis=-1, keepdims=True)
    return jnp.concatenate(out, axis=-1)


def _attend_keys(qi_ref, ki_ref, w_ref, q_ref, k_ref, v_ref, o_ref, score_scr, bias_scr, *, qb, tq, lk, topk):
    n_chunks = lk // MXU_WIDTH
    qi_all = qi_ref[0].reshape(H_IDX * tq, D_IDX)
    w = w_ref[0] * (D_IDX ** -0.5)

    def chunk_body(c, carry):
        k0 = pl.multiple_of(c * MXU_WIDTH, MXU_WIDTH)
        s = lax.dot_general(ki_ref[0, pl.ds(k0, MXU_WIDTH), :], qi_all, NT_DIMS, preferred_element_type=F32)
        acc = jnp.zeros((MXU_WIDTH, tq), F32)
        for h in range(H_IDX):
            acc = acc + jnp.maximum(s[:, h * tq:(h + 1) * tq], 0.0) * w[h:h + 1, :]
        score_scr[c] = acc
        return carry

    lax.fori_loop(0, n_chunks, chunk_body, 0)
    score = jnp.concatenate([score_scr[c] for c in range(n_chunks)], axis=0)
    kpos = lax.broadcasted_iota(jnp.int32, (lk, tq), 0)
    qpos = qb * tq + lax.broadcasted_iota(jnp.int32, (lk, tq), 1)
    causal = kpos <= qpos
    key = _sortable_key(jnp.where(causal, score, -jnp.inf))
    thr = _kth_largest_key(key, topk, axis=0)
    gt = key > thr
    eq = jnp.logical_and(key == thr, causal)
    need = float(topk) - jnp.sum(jnp.where(gt, 1.0, 0.0), axis=0, keepdims=True)
    n_eq = jnp.sum(jnp.where(eq, 1.0, 0.0), axis=0, keepdims=True)
    bias_scr[:lk, :] = jnp.where(jnp.logical_and(causal, key >= thr), 0.0, NEG_INF)

    @pl.when(jnp.max(n_eq - need) > 0.0)
    def _():
        rank = _exclusive_cumsum_rows(jnp.where(eq, 1.0, 0.0).astype(BF16))
        keep = jnp.logical_or(gt, jnp.logical_and(eq, rank < need))
        bias_scr[:lk, :] = jnp.where(jnp.logical_and(causal, keep), 0.0, NEG_INF)

    bias = bias_scr[:lk, :]
    scale = HEAD_DIM ** -0.5
    for h in range(q_ref.shape[-1] // HEAD_DIM):
        sl = slice(h * HEAD_DIM, (h + 1) * HEAD_DIM)
        logits = lax.dot_general(k_ref[0, :lk, sl], q_ref[0, :, sl], NT_DIMS, preferred_element_type=F32) * scale + bias
        m = jnp.max(logits, axis=0, keepdims=True)
        p = jnp.exp(logits - m)
        den = jnp.sum(p, axis=0, keepdims=True)
        den_col = jnp.transpose(jnp.broadcast_to(den, (SUBLANES, tq)))[:, 0:1]
        o = lax.dot_general(p.astype(BF16), v_ref[0, :lk, sl], TN_DIMS, preferred_element_type=F32)
        o_ref[0, :, sl] = (o / den_col).astype(o_ref.dtype)


def _prompt_attn_kernel(qi_ref, ki_ref, w_ref, q_ref, k_ref, v_ref, o_ref, score_scr, bias_scr,
                        *, tq, seq, topk, n_buckets):
    qb = pl.program_id(1)
    per = (seq // tq) // n_buckets
    for bkt in range(n_buckets):
        body = functools.partial(_attend_keys, qi_ref, ki_ref, w_ref, q_ref, k_ref, v_ref, o_ref, score_scr, bias_scr,
                                 qb=qb, tq=tq, lk=(bkt + 1) * per * tq, topk=topk)
        pl.when(qb // per == bkt)(body)


def prompt_attention(qi, ki, wi, q, k, v, tq, n_buckets):
    b, t, n = q.shape
    topk = min(TOPK_MAX, t // 4)
    return pl.pallas_call(
        functools.partial(_prompt_attn_kernel, tq=tq, seq=t, topk=topk, n_buckets=n_buckets),
        grid=(b, t // tq),
        in_specs=[pl.BlockSpec((1, H_IDX, tq, D_IDX), lambda i, j: (i, 0, j, 0)),
                  pl.BlockSpec((1, t, D_IDX), lambda i, j: (i, 0, 0)),
                  pl.BlockSpec((1, H_IDX, tq), lambda i, j: (i, 0, j)),
                  pl.BlockSpec((1, tq, n), lambda i, j: (i, j, 0)),
                  pl.BlockSpec((1, t, n), lambda i, j: (i, 0, 0)),
                  pl.BlockSpec((1, t, n), lambda i, j: (i, 0, 0))],
        out_specs=pl.BlockSpec((1, tq, n), lambda i, j: (i, j, 0)),
        out_shape=jax.ShapeDtypeStruct((b, t, n), BF16),
        scratch_shapes=[pltpu.VMEM((t // MXU_WIDTH, MXU_WIDTH, tq), F32), pltpu.VMEM((t, tq), F32)],
        compiler_params=_cparams(("parallel", "parallel")),
        name="prompt_attn",
    )(qi, ki, wi, q, k, v)


def _sample_score_kernel(pt_ref, qi_ref, w_ref, kin_ref, cache_ref, o_ref, kbuf, sem, *, n_pages, n_batch):
    b = pl.program_id(0)
    past = n_pages * PAGE_SIZE

    def page_copy(bb, slot, pg):
        return pltpu.make_async_copy(cache_ref.at[pt_ref[bb, pg]],
                                     kbuf.at[slot, :, pl.ds(pl.multiple_of(pg * PAGE_SIZE, PAGE_SIZE), PAGE_SIZE)],
                                     sem.at[slot])

    def start_all(bb, slot):
        lax.fori_loop(0, n_pages, lambda pg, c: (page_copy(bb, slot, pg).start(), c)[1], 0)

    @pl.when(b == 0)
    def _():
        start_all(0, 0)

    @pl.when(b + 1 < n_batch)
    def _():
        start_all(b + 1, (b + 1) % 2)

    slot = b % 2
    lax.fori_loop(0, n_pages, lambda pg, c: (page_copy(b, slot, pg).wait(), c)[1], 0)

    qi = qi_ref[0]
    w = w_ref[0] * (D_IDX ** -0.5)
    step = 2048
    for c in range(past // step):
        kc = kbuf[slot, :, c * step:(c + 1) * step].astype(BF16)
        s = jnp.dot(qi, kc, preferred_element_type=F32)
        o_ref[0, :, c * step:(c + 1) * step] = jnp.sum(jnp.maximum(s, 0.0) * w, axis=0, keepdims=True)
    s_new = jnp.sum(qi.astype(F32) * kin_ref[0].astype(BF16).astype(F32), axis=-1, keepdims=True)
    s_new = jnp.sum(jnp.maximum(s_new, 0.0) * w, axis=0, keepdims=True)
    lane = lax.broadcasted_iota(jnp.int32, (1, LANES), 1)
    o_ref[0, :, past:] = jnp.where(lane == 0, s_new, -jnp.inf)


def sample_scores(page_table, qi, wi, ki_new, cache_idx_k):
    bsz, n_pages = page_table.shape
    past = n_pages * PAGE_SIZE
    grid_spec = pltpu.PrefetchScalarGridSpec(
        num_scalar_prefetch=1,
        grid=(bsz,),
        in_specs=[pl.BlockSpec((1, H_IDX, D_IDX), lambda i, pt: (i, 0, 0)),
                  pl.BlockSpec((1, H_IDX, 1), lambda i, pt: (i, 0, 0)),
                  pl.BlockSpec((1, 1, D_IDX), lambda i, pt: (i, 0, 0)),
                  pl.BlockSpec(memory_space=pl.ANY)],
        out_specs=pl.BlockSpec((1, 1, past + LANES), lambda i, pt: (i, 0, 0)),
        scratch_shapes=[pltpu.VMEM((2, D_IDX, past), F32), pltpu.SemaphoreType.DMA((2,))],
    )
    return pl.pallas_call(
        functools.partial(_sample_score_kernel, n_pages=n_pages, n_batch=bsz),
        grid_spec=grid_spec,
        out_shape=jax.ShapeDtypeStruct((bsz, 1, past + LANES), F32),
        compiler_params=_cparams(("arbitrary",)),
        name="sample_scores",
    )(page_table, qi, wi, ki_new, jnp.swapaxes(cache_idx_k, 1, 2))


def _sample_select_kernel(score_ref, pos_ref, o_ref, rank_scr, *, topk):
    score = score_ref[...]
    bsz, n = score.shape
    key = _sortable_key(score)
    thr = _kth_largest_key(key, topk)
    gt = key > thr
    eq = key == thr
    need = topk - jnp.sum(gt.astype(jnp.int32), axis=-1, keepdims=True)
    rank_eq = _exclusive_cumsum_lanes(jnp.where(eq, 1.0, 0.0).astype(BF16))
    sel = jnp.logical_or(gt, jnp.logical_and(eq, rank_eq < need.astype(F32)))
    self32 = jnp.where(sel, 1.0, 0.0)
    rank = _exclusive_cumsum_lanes(self32.astype(BF16))
    rank_scr[...] = jnp.where(sel, rank, -1.0)
    slot_id = lax.broadcasted_iota(jnp.int32, (topk, 1), 0).astype(F32)
    n_split = 3
    step = n // n_split

    def body(bb, c):
        acc = jnp.zeros((topk, LANES), F32)
        for s in range(n_split):
            onehot = jnp.where(rank_scr[pl.ds(bb, 1), s * step:(s + 1) * step] == slot_id, 1.0, 0.0).astype(BF16)
            acc = acc + jnp.dot(onehot, pos_ref[s * step:(s + 1) * step, :], preferred_element_type=F32)
        o_ref[bb] = acc
        return c

    lax.fori_loop(0, bsz, body, 0)


def sample_select(scores, topk):
    bsz, n = scores.shape
    s = jnp.arange(n, dtype=jnp.int32)
    lane = jnp.arange(LANES, dtype=jnp.int32)[None, :]
    pos = jnp.where(lane == 0, (s // PAGE_SIZE)[:, None], jnp.where(lane == 1, (s % PAGE_SIZE)[:, None], 0)).astype(BF16)
    return pl.pallas_call(
        functools.partial(_sample_select_kernel, topk=topk),
        out_shape=jax.ShapeDtypeStruct((bsz, topk, LANES), F32),
        scratch_shapes=[pltpu.VMEM((bsz, n), F32)],
        compiler_params=pltpu.CompilerParams(vmem_limit_bytes=VMEM_LIMIT),
        name="sample_select",
    )(scores, pos)


def _sample_attend_kernel(idx_ref, pt_ref, q_ref, knew_ref, vnew_ref, ck_ref, cv_ref, o_ref, kbuf, vbuf, ksem, vsem,
                          *, topk, past, n_batch):
    b = pl.program_id(0)

    def row_copies(bb, slot, r):
        i = jnp.minimum(idx_ref[bb, r], past - 1)
        phys = pt_ref[bb, i // PAGE_SIZE]
        off = i % PAGE_SIZE
        return (pltpu.make_async_copy(ck_ref.at[phys, off], kbuf.at[slot, r], ksem.at[slot]),
                pltpu.make_async_copy(cv_ref.at[phys, off], vbuf.at[slot, r], vsem.at[slot]))

    def start_all(bb, slot):
        def body(r, c):
            for cp in row_copies(bb, slot, r):
                cp.start()
            return c

        lax.fori_loop(0, topk, body, 0, unroll=8)

    @pl.when(b == 0)
    def _():
        start_all(0, 0)

    @pl.when(b + 1 < n_batch)
    def _():
        start_all(b + 1, (b + 1) % 2)

    slot = b % 2

    def wait_body(r, c):
        for cp in row_copies(b, slot, r):
            cp.wait()
        return c

    lax.fori_loop(0, topk, wait_body, 0, unroll=8)

    @pl.when(idx_ref[b, topk - 1] == past)
    def _():
        kbuf[slot, topk - 1] = knew_ref[0]
        vbuf[slot, topk - 1] = vnew_ref[0]

    q = q_ref[0]
    k = kbuf[slot]
    logits = jnp.sum(k * q[None], axis=-1, keepdims=True) * (HEAD_DIM ** -0.5)
    m = jnp.max(logits, axis=0, keepdims=True)
    p = jnp.exp(logits - m)
    den = jnp.sum(p, axis=0)
    o_ref[0] = jnp.sum(p * vbuf[slot], axis=0) / den


def sample_attend(idx, page_table, q, k_new, v_new, cache_k, cache_v):
    bsz, h, dh = q.shape
    topk = idx.shape[1]
    past = page_table.shape[1] * PAGE_SIZE
    grid_spec = pltpu.PrefetchScalarGridSpec(
        num_scalar_prefetch=2,
        grid=(bsz,),
        in_specs=[pl.BlockSpec((1, h, dh), lambda i, a, c: (i, 0, 0)), pl.BlockSpec((1, h, dh), lambda i, a, c: (i, 0, 0)),
                  pl.BlockSpec((1, h, dh), lambda i, a, c: (i, 0, 0)),
                  pl.BlockSpec(memory_space=pl.ANY), pl.BlockSpec(memory_space=pl.ANY)],
        out_specs=pl.BlockSpec((1, h, dh), lambda i, a, c: (i, 0, 0)),
        scratch_shapes=[pltpu.VMEM((2, topk, h, dh), F32), pltpu.VMEM((2, topk, h, dh), F32),
                        pltpu.SemaphoreType.DMA((2,)), pltpu.SemaphoreType.DMA((2,))],
    )
    return pl.pallas_call(
        functools.partial(_sample_attend_kernel, topk=topk, past=past, n_batch=bsz),
        grid_spec=grid_spec,
        out_shape=jax.ShapeDtypeStruct((bsz, h, dh), F32),
        compiler_params=_cparams(("arbitrary",)),
        name="sample_attend",
    )(idx, page_table, q, k_new, v_new, cache_k, cache_v)


def _merge_kernel(h_ref, s_ref, a_ref, x_ref, wgs_ref, wga_ref, wbs_ref, wba_ref, wo_ref, gn_ref, o_ref, hn_ref,
                  *, n_tiles):
    j = pl.program_id(1)
    dot = functools.partial(jnp.dot, preferred_element_type=F32)
    tm = h_ref.shape[0]
    sub = min(tm, MERGE_SUB_ROWS)

    @pl.when(j == 0)
    def _():
        o_ref[...] = x_ref[...]

    for r0 in range(0, tm, sub):
        rows = slice(r0, r0 + sub)
        h = h_ref[rows, :]
        merged = (jax.nn.sigmoid(dot(h, wgs_ref[0])) * dot(s_ref[rows, :], wbs_ref[0])
                  + jax.nn.sigmoid(dot(h, wga_ref[0])) * dot(a_ref[rows, :], wba_ref[0]))
        o_ref[rows, :] += dot(merged.astype(BF16), wo_ref[...])

    @pl.when(j == n_tiles - 1)
    def _():
        for r0 in range(0, tm, sub):
            rows = slice(r0, r0 + sub)
            x1 = o_ref[rows, :]
            ms = jnp.mean(x1 * x1, axis=-1, keepdims=True)
            hn_ref[rows, :] = (x1 * lax.rsqrt(ms + RMS_EPS) * gn_ref[...]).astype(hn_ref.dtype)


def merge_branches(h, ssm_out, att, x2d, w_gates, w_bs, w_ba, w_out, g_next, tm, tn):
    m, d = x2d.shape
    kdim = ssm_out.shape[1]
    nj = d // tn
    row = lambda c: pl.BlockSpec((tm, c), lambda i, j: (i, 0))
    return pl.pallas_call(
        functools.partial(_merge_kernel, n_tiles=nj),
        grid=(m // tm, nj),
        in_specs=[row(d), row(kdim), row(kdim), row(d),
                  pl.BlockSpec((1, d, tn), lambda i, j: (j, 0, 0)), pl.BlockSpec((1, d, tn), lambda i, j: (j + nj, 0, 0)),
                  pl.BlockSpec((1, kdim, tn), lambda i, j: (j, 0, 0)), pl.BlockSpec((1, kdim, tn), lambda i, j: (j, 0, 0)),
                  pl.BlockSpec((tn, d), lambda i, j: (j, 0)), pl.BlockSpec((1, d), lambda i, j: (0, 0))],
        out_specs=[row(d), row(d)],
        out_shape=[jax.ShapeDtypeStruct((m, d), F32), jax.ShapeDtypeStruct((m, d), BF16)],
        compiler_params=_cparams(("parallel", "arbitrary")),
        name="merge_branches",
    )(h, ssm_out, att, x2d, w_gates, w_gates, w_bs, w_ba, w_out, g_next.reshape(1, d).astype(F32))


def _ffn_step_kernel(x_ref, gf_ref, gfin_ref, wa_ref, wb_ref, cwa_ref, cwb_ref, cba_ref, cbb_ref, wd_ref,
                     p2a_ref, p1a_ref, p2b_ref, p1b_ref, y_ref, ca_ref, cb_ref, h_scr, acc_scr, *, n_ff_tiles):
    j = pl.program_id(1)

    @pl.when(j == 0)
    def _():
        x = x_ref[0]
        ms = jnp.mean(x * x, axis=-1, keepdims=True)
        h_scr[...] = (x * lax.rsqrt(ms + RMS_EPS) * gf_ref[...]).astype(BF16)
        acc_scr[...] = jnp.zeros_like(acc_scr)

    h = h_scr[...]

    def conv_half(w_ref, cw_ref, cbias_ref, out_ref, p2_ref, p1_ref):
        up = jnp.dot(h, w_ref[0], preferred_element_type=F32)
        cw = cw_ref[...]
        out_ref[...] = up
        return cbias_ref[...] + cw[0:1, :] * p2_ref[...] + cw[1:2, :] * p1_ref[...] + cw[2:3, :] * up

    a = conv_half(wa_ref, cwa_ref, cba_ref, ca_ref, p2a_ref, p1a_ref)
    g = conv_half(wb_ref, cwb_ref, cbb_ref, cb_ref, p2b_ref, p1b_ref)
    act = (jax.nn.silu(a) * g).astype(BF16)
    acc_scr[...] += jnp.dot(act, wd_ref[...], preferred_element_type=F32)

    @pl.when(j == n_ff_tiles - 1)
    def _():
        x2 = x_ref[0] + acc_scr[...]
        ms = jnp.mean(x2 * x2, axis=-1, keepdims=True)
        y_ref[0] = x2 * lax.rsqrt(ms + RMS_EPS) * gfin_ref[...]


def conv_ffn_step(x, conv_prev, g_ffn, g_final, w_up, conv_w, conv_b, w_down, tm):
    b, t, d = x.shape
    tf = w_up.shape[2]
    f = w_down.shape[0]
    nf = f // tf
    gf = g_ffn.reshape(1, d).astype(F32)
    gfin = g_final.reshape(1, d).astype(F32)
    cb2 = conv_b.reshape(1, 2 * f).astype(F32)
    conv_w = conv_w.astype(F32)
    kern = functools.partial(_ffn_step_kernel, n_ff_tiles=nf)
    p2, p1 = conv_prev[:, 0, :], conv_prev[:, 1, :]
    wspec = lambda blk, off: pl.BlockSpec(blk, lambda i, j: (0, j + off))
    upspec = lambda off: pl.BlockSpec((1, d, tf), lambda i, j: (j + off, 0, 0))
    in_specs = [pl.BlockSpec((1, tm, d), lambda i, j: (0, i, 0)),
                pl.BlockSpec((1, d), lambda i, j: (0, 0)), pl.BlockSpec((1, d), lambda i, j: (0, 0)),
                upspec(0), upspec(nf), wspec((CONV_W, tf), 0), wspec((CONV_W, tf), nf),
                wspec((1, tf), 0), wspec((1, tf), nf),
                pl.BlockSpec((tf, d), lambda i, j: (j, 0)),
                pl.BlockSpec((tm, tf), lambda i, j: (i, j)), pl.BlockSpec((tm, tf), lambda i, j: (i, j)),
                pl.BlockSpec((tm, tf), lambda i, j: (i, j + nf)), pl.BlockSpec((tm, tf), lambda i, j: (i, j + nf))]
    out_specs = [pl.BlockSpec((1, tm, d), lambda i, j: (0, i, 0)),
                 pl.BlockSpec((tm, tf), lambda i, j: (i, j)), pl.BlockSpec((tm, tf), lambda i, j: (i, j))]
    out_shape = [jax.ShapeDtypeStruct((1, t, d), F32), jax.ShapeDtypeStruct((t, f), F32),
                 jax.ShapeDtypeStruct((t, f), F32)]
    scratch = [pltpu.VMEM((tm, d), BF16), pltpu.VMEM((tm, d), F32)]
    y, ua, ub = pl.pallas_call(
        kern, grid=(t // tm, nf), in_specs=in_specs, out_specs=out_specs, out_shape=out_shape,
        scratch_shapes=scratch, compiler_params=_cparams(("arbitrary", "arbitrary")),
        name="conv_ffn_step",
    )(x, gf, gfin, w_up, w_up, conv_w, conv_w, cb2, cb2, w_down, p2, p1, p2, p1)
    return y, jnp.stack([p1, jnp.concatenate([ua, ub], axis=-1)], axis=1)


def _ffn_seq_kernel(h_ref, xt_ref, gfin_ref, wa_ref, wb_ref, cwa_ref, cwb_ref, cba_ref, cbb_ref, wd_ref, pa_ref, pb_ref,
                    y_ref, ca_ref, cb_ref, act_scr, car_a, car_b, *, tm, tf, nf, tn, nd):
    i = pl.program_id(1)
    j = pl.program_id(2)

    @pl.when(j < nf)
    def _():
        sub = min(tm, FFN_SUB_ROWS)
        row = lax.broadcasted_iota(jnp.int32, (sub, tf), 0)

        def conv_half(w_ref, cw_ref, cbias_ref, out_ref, prev_ref, car):
            cw = cw_ref[...]
            cbias = cbias_ref[...]

            @pl.when(i == 0)
            def _():
                car[j] = prev_ref[0]

            c2 = car[j]
            rm2, rm1 = c2[0:1, :], c2[1:2, :]
            outs = []
            for r0 in range(0, tm, sub):
                up = jnp.dot(h_ref[0, r0:r0 + sub, :], w_ref[0], preferred_element_type=F32)
                p1 = jnp.where(row == 0, rm1, pltpu.roll(up, 1, 0))
                p2 = jnp.where(row == 0, rm2, jnp.where(row == 1, rm1, pltpu.roll(up, 2, 0)))
                outs.append(cbias + cw[0:1, :] * p2 + cw[1:2, :] * p1 + cw[2:3, :] * up)
                rm2, rm1 = up[sub - 2:sub - 1, :], up[sub - 1:sub, :]
            last2 = jnp.concatenate([rm2, rm1], axis=0)
            car[j] = last2
            out_ref[0, 0] = last2
            return outs

        a = conv_half(wa_ref, cwa_ref, cba_ref, ca_ref, pa_ref, car_a)
        g = conv_half(wb_ref, cwb_ref, cbb_ref, cb_ref, pb_ref, car_b)
        act = [(jax.nn.silu(ar) * gr).astype(BF16) for ar, gr in zip(a, g)]
        for k in range(nf):

            @pl.when(j == k)
            def _(k=k):
                for r, act_r in enumerate(act):
                    act_scr[r * sub:(r + 1) * sub, k * tf:(k + 1) * tf] = act_r

    @pl.when(j >= nf)
    def _():
        sub = min(tm, 2 * FFN_SUB_ROWS)
        for r0 in range(0, tm, sub):
            part = xt_ref[0, r0:r0 + sub, :] + jnp.dot(act_scr[r0:r0 + sub, :], wd_ref[0], preferred_element_type=F32)
            for n in range(nd):

                @pl.when(j == nf + n)
                def _(n=n, part=part, r0=r0):
                    y_ref[0, r0:r0 + sub, n * tn:(n + 1) * tn] = part

    @pl.when(j == nf + nd - 1)
    def _():
        sub = min(tm, FFN_SUB_ROWS)
        for r0 in range(0, tm, sub):
            x2 = y_ref[0, r0:r0 + sub, :]
            ms = jnp.mean(x2 * x2, axis=-1, keepdims=True)
            y_ref[0, r0:r0 + sub, :] = x2 * lax.rsqrt(ms + RMS_EPS) * gfin_ref[...]


def conv_ffn_sequence(h, x, conv_prev, g_final, w_up, conv_w, conv_b, w_down, tm):
    b, t, d = x.shape
    tf = w_up.shape[2]
    nd, f, tn = w_down.shape
    nf, nt = f // tf, t // tm
    gfin = g_final.reshape(1, d).astype(F32)
    cb2 = conv_b.reshape(1, 2 * f).astype(F32)
    conv_w = conv_w.astype(F32)
    up_tile = lambda j: jnp.minimum(j, nf - 1)
    down_tile = lambda j: jnp.clip(j - nf, 0, nd - 1)
    wspec = lambda blk, off: pl.BlockSpec(blk, lambda bi, i, j: (0, up_tile(j) + off))
    upspec = lambda off: pl.BlockSpec((1, d, tf), lambda bi, i, j: (up_tile(j) + off, 0, 0))
    prev_spec = lambda off: pl.BlockSpec((1, CONV_W - 1, tf), lambda bi, i, j: (bi, 0, up_tile(j) + off))
    state_spec = pl.BlockSpec((1, 1, CONV_W - 1, tf), lambda bi, i, j: (bi, i, 0, up_tile(j)))
    y, ca, cb = pl.pallas_call(
        functools.partial(_ffn_seq_kernel, tm=tm, tf=tf, nf=nf, tn=tn, nd=nd),
        grid=(b, nt, nf + nd),
        in_specs=[pl.BlockSpec((1, tm, d), lambda bi, i, j: (bi, i, 0), pipeline_mode=pl.Buffered(1)),
                  pl.BlockSpec((1, tm, tn), lambda bi, i, j: (bi, i, down_tile(j))),
                  pl.BlockSpec((1, d), lambda bi, i, j: (0, 0)),
                  upspec(0), upspec(nf), wspec((CONV_W, tf), 0), wspec((CONV_W, tf), nf),
                  wspec((1, tf), 0), wspec((1, tf), nf),
                  pl.BlockSpec((1, f, tn), lambda bi, i, j: (down_tile(j), 0, 0)),
                  prev_spec(0), prev_spec(nf)],
        out_specs=[pl.BlockSpec((1, tm, d), lambda bi, i, j: (bi, i, 0), pipeline_mode=pl.Buffered(1)),
                   state_spec, state_spec],
        out_shape=[jax.ShapeDtypeStruct((b, t, d), F32), jax.ShapeDtypeStruct((b, nt, CONV_W - 1, f), F32),
                   jax.ShapeDtypeStruct((b, nt, CONV_W - 1, f), F32)],
        scratch_shapes=[pltpu.VMEM((tm, f), BF16),
                        pltpu.VMEM((nf, CONV_W - 1, tf), F32), pltpu.VMEM((nf, CONV_W - 1, tf), F32)],
        compiler_params=_cparams(("arbitrary", "arbitrary", "arbitrary")),
        name="conv_ffn_seq",
    )(h, x, gfin, w_up, w_up, conv_w, conv_w, cb2, cb2, w_down, conv_prev, conv_prev)
    return y, jnp.concatenate([ca[:, -1], cb[:, -1]], axis=-1)


def column_tiles(w, tn):
    k, n = w.shape
    return w.astype(BF16).reshape(k, n // tn, tn).transpose(1, 0, 2)


def _split_w_in(w_in, d_model, n_ssm, n_att):
    sizes = [n_ssm, n_att, n_att, n_att, H_IDX * D_IDX, D_IDX, H_IDX, d_model, d_model]
    offs = np.concatenate([[0], np.cumsum(sizes)]).tolist()
    col = lambda i: w_in[:, offs[i]:offs[i + 1]]
    w_uv = column_tiles(jnp.concatenate([col(0), col(3)], axis=1), COL_TILE)
    w_qk = column_tiles(jnp.concatenate([col(1), col(2)], axis=1), COL_TILE)
    w_qi = column_tiles(col(4), COL_TILE)
    pad = jnp.zeros((w_in.shape[0], LANES - D_IDX - H_IDX), w_in.dtype)
    w_kw = column_tiles(jnp.concatenate([col(5), col(6), pad], axis=1), LANES)
    w_g = column_tiles(jnp.concatenate([col(7), col(8)], axis=1), COL_TILE)
    return w_uv, w_qk, w_qi, w_kw, w_g


def _mix_inputs(x2d, pos, n_pos_tiles, tm, norm_mix, w_groups, n_ssm, n_att, sequence):
    w_uv, w_qk, w_qi, w_kw, w_g = w_groups
    h = rmsnorm_bf16(x2d, norm_mix, tm)
    tn = COL_TILE
    nu, na = n_ssm // tn, n_att // tn
    half_qk = HEAD_DIM // ROPE_FRACTION // 2
    half_idx = D_IDX // ROPE_FRACTION // 2
    out = {}
    if sequence:
        out["u"], out["v"], out["v16"] = project(
            h, w_uv, tm, tn, [(0, nu, F32, "flat"), (nu, na, F32, "flat"), (nu, na, BF16, "flat")])
        out["q16"], out["k"], out["k16"] = project(
            h, w_qk, tm, tn, [(0, na, BF16, "flat"), (na, na, F32, "flat"), (na, na, BF16, "flat")],
            "rope", half_qk, rope_tables(pos, HEAD_DIM, tn), n_pos_tiles)
        out["qi16"], = project(h, w_qi, tm, tn, [(0, H_IDX * D_IDX // tn, BF16, "heads")],
                               "rope", half_idx, rope_tables(pos, D_IDX, tn), n_pos_tiles)
    else:
        out["u"], out["v"] = project(h, w_uv, tm, tn, [(0, nu, F32, "flat"), (nu, na, F32, "flat")])
        out["q"], out["k"] = project(h, w_qk, tm, tn, [(0, na, F32, "flat"), (na, na, F32, "flat")],
                                     "rope", half_qk, rope_tables(pos, HEAD_DIM, tn), n_pos_tiles)
        out["qi"], = project(h, w_qi, tm, tn, [(0, H_IDX * D_IDX // tn, F32, "flat")],
                             "rope", half_idx, rope_tables(pos, D_IDX, tn), n_pos_tiles)
    extra = jnp.concatenate([jnp.full((H_IDX,), H_IDX ** -0.5, F32), jnp.zeros((LANES - D_IDX - H_IDX,), F32)])
    kw, = project(h, w_kw, tm, LANES, [(0, 1, F32, "flat")], "rope", half_idx,
                  rope_tables(pos, D_IDX, LANES, extra), n_pos_tiles)
    out["ki"], out["wi"] = kw[:, :D_IDX], kw[:, D_IDX:D_IDX + H_IDX]
    out["h"] = h
    return out


def kernel(x_prompt, x_sample, cache_k, cache_v, cache_idx_k, page_table, state_ssm_re, state_ssm_im,
           state_ffn_conv, norm_mix, w_in, ssm_A_re, ssm_A_im, ssm_log_dt, ssm_B_re, ssm_B_im, ssm_C_re,
           ssm_C_im, ssm_D, w_glu, w_branch_ssm, w_branch_att, w_out, norm_ffn, w_up, ffn_conv_w,
           ffn_conv_b, w_down, norm_final):
    bp, t, d = x_prompt.shape
    bs, ts, _ = x_sample.shape
    n_ssm = ssm_D.shape[0]
    n_att = w_branch_att.shape[0]
    g = n_ssm // SSM_GROUP
    h_att = n_att // HEAD_DIM
    past = page_table.shape[1] * PAGE_SIZE
    assert ts == 1, "sample group is a single decode step"
    assert bp == SUBLANES, "the S5 sequence kernel keeps one batch row per sublane"

    w_groups = _split_w_in(w_in, d, n_ssm, n_att)
    w_glu_b, w_out_b, w_down_b = (w.astype(BF16) for w in (w_glu, w_out, w_down))
    w_bs, w_ba = column_tiles(w_branch_ssm, COL_TILE), column_tiles(w_branch_att, COL_TILE)
    w_up_t, w_down_t = column_tiles(w_up, COL_TILE), column_tiles(w_down, COL_TILE)
    ssm_params = (ssm_A_re, ssm_A_im, ssm_log_dt, ssm_B_re, ssm_B_im, ssm_C_re, ssm_C_im)

    mp = bp * t
    tm = 1024
    xp2 = x_prompt.reshape(mp, d)
    pos_p = jnp.arange(t, dtype=jnp.int32)
    pr = _mix_inputs(xp2, pos_p, t // tm, tm, norm_mix, w_groups, n_ssm, n_att, True)
    zeros_s = jnp.zeros((bp, g, SSM_STATE), F32)
    y_act, s_re_p, s_im_p = s5_sequence(pr["u"].reshape(bp, t, n_ssm), zeros_s, zeros_s, ssm_D,
                                        ssm_tile_matrices(*ssm_params, SSM_CHUNK, BF16), SSM_CHUNK, 512)
    ssm_out = ssm_glu(y_act.reshape(mp, n_ssm), w_glu_b, 512)
    wi_t = pr["wi"].reshape(bp, t, H_IDX).transpose(0, 2, 1)
    att = prompt_attention(pr["qi16"], pr["ki"].astype(BF16).reshape(bp, t, D_IDX), wi_t,
                           pr["q16"].reshape(bp, t, n_att), pr["k16"].reshape(bp, t, n_att),
                           pr["v16"].reshape(bp, t, n_att), 256, 8)
    x1, h_ffn = merge_branches(pr["h"], ssm_out, att.reshape(mp, n_att), xp2, w_groups[4], w_bs, w_ba, w_out_b,
                               norm_ffn, 512, COL_TILE)
    zeros_c = jnp.zeros((bp, CONV_W - 1, w_up.shape[1]), F32)
    y_prompt, conv_p = conv_ffn_sequence(h_ffn.reshape(bp, t, d), x1.reshape(bp, t, d), zeros_c, norm_final, w_up_t,
                                         ffn_conv_w, ffn_conv_b, w_down_t, 1024)
    k_p = pr["k"].reshape(bp, t, h_att, HEAD_DIM)
    v_p = pr["v"].reshape(bp, t, h_att, HEAD_DIM)
    ki_p = pr["ki"].reshape(bp, t, D_IDX)

    xs2 = x_sample.reshape(bs, d)
    pos_s = jnp.full((bs,), past, jnp.int32)
    sm = _mix_inputs(xs2, pos_s, 1, bs, norm_mix, w_groups, n_ssm, n_att, False)
    y_act, s_re_s, s_im_s = s5_step(sm["u"], state_ssm_re, state_ssm_im, ssm_D, ssm_tile_matrices(*ssm_params, 1, F32))
    ssm_out = ssm_glu(y_act, w_glu_b, bs)
    scores = sample_scores(page_table, sm["qi"].astype(BF16).reshape(bs, H_IDX, D_IDX), sm["wi"].reshape(bs, H_IDX, 1),
                           sm["ki"].reshape(bs, 1, D_IDX), cache_idx_k)
    topk = min(TOPK_MAX, (past + 1) // 4)
    hl = sample_select(scores.reshape(bs, past + LANES), topk)
    idx = (hl[:, :, 0] * PAGE_SIZE + hl[:, :, 1]).astype(jnp.int32)
    att = sample_attend(idx, page_table, sm["q"].reshape(bs, h_att, HEAD_DIM), sm["k"].reshape(bs, h_att, HEAD_DIM),
                        sm["v"].reshape(bs, h_att, HEAD_DIM), cache_k, cache_v)
    x1, _ = merge_branches(sm["h"], ssm_out, att.reshape(bs, n_att).astype(BF16), xs2, w_groups[4], w_bs, w_ba, w_out_b,
                           norm_ffn, bs, 512)
    y_s, conv_s = conv_ffn_step(x1.reshape(1, bs, d), state_ffn_conv, norm_ffn, norm_final, w_up_t, ffn_conv_w,
                                ffn_conv_b, w_down_b, bs)
    y_sample = y_s.reshape(bs, 1, d)
    k_s = sm["k"].reshape(bs, 1, h_att, HEAD_DIM)
    v_s = sm["v"].reshape(bs, 1, h_att, HEAD_DIM)
    ki_s = sm["ki"].reshape(bs, 1, D_IDX)

    return (y_prompt, y_sample, k_p, v_p, ki_p, k_s, v_s, ki_s, s_re_p, s_im_p, s_re_s, s_im_s, conv_p, conv_s)
```

```python
import functools

import numpy as np
import jax
import jax.numpy as jnp
from jax import lax
from jax.experimental import pallas as pl
from jax.experimental.pallas import tpu as pltpu

F32 = jnp.float32
BF16 = jnp.bfloat16

HEAD_DIM = 128
SSM_GROUP = 16
SSM_STATE = 64
H_IDX = 16
D_IDX = 64
TOPK_MAX = 256
ROPE_THETA = 500000.0
ROPE_FRACTION = 4
CONV_W = 3
RMS_EPS = 1e-6
NEG_INF = -1e30
PAGE_SIZE = 128

LANES = 128
SUBLANES = 8
MXU_WIDTH = 256
GROUPS_PER_TILE = LANES // SSM_GROUP
SSM_CHUNK = 8
PROJ_SUB_ROWS = 256
FFN_SUB_ROWS = 256
COL_TILE = 512
MERGE_SUB_ROWS = 256
VMEM_LIMIT = 56 * 1024 * 1024
INT_MIN = -2 ** 31
NT_DIMS = (((1,), (1,)), ((), ()))
TN_DIMS = (((0,), (0,)), ((), ()))


def _cparams(sem):
    return pltpu.CompilerParams(dimension_semantics=sem, vmem_limit_bytes=VMEM_LIMIT)


def _rmsnorm_kernel(x_ref, g_ref, o_ref):
    x = x_ref[...]
    ms = jnp.mean(x * x, axis=-1, keepdims=True)
    o_ref[...] = (x * lax.rsqrt(ms + RMS_EPS) * g_ref[...]).astype(o_ref.dtype)


def rmsnorm_bf16(x2d, g, tm):
    m, d = x2d.shape
    return pl.pallas_call(
        _rmsnorm_kernel,
        grid=(m // tm,),
        in_specs=[pl.BlockSpec((tm, d), lambda i: (i, 0)), pl.BlockSpec((1, d), lambda i: (0, 0))],
        out_specs=pl.BlockSpec((tm, d), lambda i: (i, 0)),
        out_shape=jax.ShapeDtypeStruct((m, d), BF16),
        compiler_params=_cparams(("parallel",)),
        name="rmsnorm",
    )(x2d, g.reshape(1, d))


def _proj_kernel(h_ref, w_ref, *rest, mode, shift, outs, n_tab, n_col_tiles):
    tabs, o_refs = rest[:n_tab], rest[n_tab:]
    j = pl.program_id(1)
    tm, tn = h_ref.shape[0], w_ref.shape[2]
    sub = min(tm, PROJ_SUB_ROWS)
    for r0 in range(0, tm, sub):
        rows = slice(r0, r0 + sub)
        z = jnp.dot(h_ref[rows, :], w_ref[0], preferred_element_type=F32)
        if mode == "rope":
            c_ref, s1_ref, s2_ref = tabs
            z = (z * c_ref[rows, :] + pltpu.roll(z, tn - shift, 1) * s1_ref[rows, :]
                 + pltpu.roll(z, shift, 1) * s2_ref[rows, :])
        for (start, cnt, _, kind), o_ref in zip(outs, o_refs):
            if kind == "heads":
                assert start + cnt == n_col_tiles
                for hh in range(tn // D_IDX):
                    o_ref[0, hh, rows, :] = z[:, hh * D_IDX:(hh + 1) * D_IDX].astype(o_ref.dtype)
            elif start + cnt == n_col_tiles:
                o_ref[rows, :] = z.astype(o_ref.dtype)
            else:
                o_ref[rows, :] = jnp.where(j >= start + cnt, o_ref[rows, :], z.astype(o_ref.dtype))


def project(h, w, tm, tn, outs, mode="plain", shift=0, tables=None, n_pos_tiles=1):
    m, k = h.shape
    n = w.shape[0] * tn
    in_specs = [pl.BlockSpec((tm, k), lambda i, j: (i, 0)), pl.BlockSpec((1, k, tn), lambda i, j: (j, 0, 0))]
    args = [h, w]
    tables = tables or ()
    for t in tables:
        in_specs.append(pl.BlockSpec((tm, tn), lambda i, j: (i % n_pos_tiles, 0)))
        args.append(t)
    out_specs, out_shape = [], []
    for start, cnt, dtype, kind in outs:
        if kind == "heads":
            hpt = tn // D_IDX
            out_specs.append(pl.BlockSpec(
                (1, hpt, tm, D_IDX),
                lambda i, j, s=start, c=cnt: (i // n_pos_tiles, jnp.clip(j - s, 0, c - 1), i % n_pos_tiles, 0)))
            out_shape.append(jax.ShapeDtypeStruct((m // (n_pos_tiles * tm), cnt * hpt, n_pos_tiles * tm, D_IDX), dtype))
        else:
            out_specs.append(pl.BlockSpec((tm, tn), lambda i, j, s=start, c=cnt: (i, jnp.clip(j - s, 0, c - 1))))
            out_shape.append(jax.ShapeDtypeStruct((m, cnt * tn), dtype))
    return pl.pallas_call(
        functools.partial(_proj_kernel, mode=mode, shift=shift, outs=tuple(outs), n_tab=len(tables),
                          n_col_tiles=n // tn),
        grid=(m // tm, n // tn),
        in_specs=in_specs,
        out_specs=out_specs,
        out_shape=out_shape,
        compiler_params=_cparams(("parallel", "arbitrary")),
        name="proj_" + mode,
    )(*args)


def rope_tables(pos, head, tn, extra=None):
    r = head // ROPE_FRACTION
    half = r // 2
    inv = ROPE_THETA ** (-jnp.arange(half, dtype=F32) * 2.0 / r)
    ang = pos.astype(F32)[:, None] * inv[None, :]
    cos, sin = jnp.cos(ang), jnp.sin(ang)
    t = pos.shape[0]
    zeros = jnp.zeros((t, head - r), F32)
    zh = jnp.zeros((t, half), F32)
    c = jnp.concatenate([cos, cos, jnp.ones((t, head - r), F32)], axis=1)
    s1 = jnp.concatenate([-sin, zh, zeros], axis=1)
    s2 = jnp.concatenate([zh, sin, zeros], axis=1)
    if extra is None:
        reps = tn // head
        return tuple(jnp.tile(a, (1, reps)) for a in (c, s1, s2))
    pad = jnp.zeros((t, tn - head), F32)
    return (jnp.concatenate([c, jnp.broadcast_to(extra[None, :], (t, tn - head))], axis=1),
            jnp.concatenate([s1, pad], axis=1), jnp.concatenate([s2, pad], axis=1))


def ssm_tile_matrices(a_re, a_im, log_dt, b_re, b_im, c_re, c_im, chunk, dtype):
    hp = lax.Precision.HIGHEST
    g, p = a_re.shape
    gt = GROUPS_PER_TILE
    nq = g // gt
    a = lax.complex(a_re.astype(F32), a_im.astype(F32))
    dt = jnp.exp(log_dt.astype(F32))[:, None]
    adt = a * dt
    a_bar = jnp.exp(adt)
    b_bar = ((a_bar - 1.0) / a)[..., None] * lax.complex(b_re.astype(F32), b_im.astype(F32))
    cc = lax.complex(c_re.astype(F32), c_im.astype(F32))
    steps = jnp.arange(chunk + 1, dtype=F32)
    pw = jnp.exp(adt[:, None, :] * steps[None, :, None].astype(jnp.complex64))
    kd = jnp.real(jnp.einsum("gcp,gdp,gpe->gdce", cc, pw[:, :chunk], b_bar, precision=hp))
    w1g = pw[:, chunk - 1 - jnp.arange(chunk)][:, :, None, :] * b_bar.transpose(0, 2, 1)[:, None]
    m2g = cc.transpose(0, 2, 1)[:, :, None, :] * pw[:, 1:chunk + 1].transpose(0, 2, 1)[:, :, :, None]
    kdc = kd.reshape(nq, gt, chunk, SSM_GROUP, SSM_GROUP).transpose(0, 2, 4, 1, 3).reshape(nq, chunk, SSM_GROUP, LANES)
    w1c = w1g.reshape(nq, gt, chunk, SSM_GROUP, p).transpose(0, 2, 3, 1, 4).reshape(nq, chunk, SSM_GROUP, gt * p)
    m2c = m2g.reshape(nq, gt, p, chunk, SSM_GROUP).transpose(0, 3, 2, 1, 4).reshape(nq, chunk, p, LANES)
    al = pw[:, chunk].reshape(nq, 1, gt * p)
    kt, w1r, w1i, w2r, w2i = _ssm_expand(kdc, jnp.real(w1c), jnp.imag(w1c), jnp.real(m2c), -jnp.imag(m2c), chunk, dtype)
    return dict(w1r=w1r, w1i=w1i, kt=kt, w2r=w2r, w2i=w2i, ar=jnp.real(al), ai=jnp.imag(al))


def _ssm_expand_kernel(kd_ref, w1r_ref, w1i_ref, w2r_ref, w2i_ref, kt_o, w1r_o, w1i_o, w2r_o, w2i_o, *, chunk):
    gt, p = GROUPS_PER_TILE, SSM_STATE

    def same_group(rows_per_group, cols_per_group):
        shape = (gt * rows_per_group, gt * cols_per_group)
        r = jnp.right_shift(lax.broadcasted_iota(jnp.int32, shape, 0), rows_per_group.bit_length() - 1)
        c = jnp.right_shift(lax.broadcasted_iota(jnp.int32, shape, 1), cols_per_group.bit_length() - 1)
        return r == c

    def block_diag(small, mask, dtype):
        return jnp.where(mask, jnp.tile(small, (gt, 1)), 0.0).astype(dtype)

    m_cc, m_cp, m_pc = same_group(SSM_GROUP, SSM_GROUP), same_group(SSM_GROUP, p), same_group(p, SSM_GROUP)
    for j in range(chunk):
        rows = slice(j * LANES, (j + 1) * LANES)
        w1r_o[0, rows, :] = block_diag(w1r_ref[0, j], m_cp, w1r_o.dtype)
        w1i_o[0, rows, :] = block_diag(w1i_ref[0, j], m_cp, w1i_o.dtype)
        w2r_o[0, :, rows] = block_diag(w2r_ref[0, j], m_pc, w2r_o.dtype)
        w2i_o[0, :, rows] = block_diag(w2i_ref[0, j], m_pc, w2i_o.dtype)
        for t in range(chunk):
            cols = slice(t * LANES, (t + 1) * LANES)
            if t >= j:
                kt_o[0, rows, cols] = block_diag(kd_ref[0, t - j], m_cc, kt_o.dtype)
            else:
                kt_o[0, rows, cols] = jnp.zeros((LANES, LANES), kt_o.dtype)


def _ssm_expand(kdc, w1r, w1i, w2r, w2i, chunk, dtype):
    nq = kdc.shape[0]
    ns = w1r.shape[-1]
    cw = chunk * LANES
    spec4 = lambda a: pl.BlockSpec((1,) + a.shape[1:], lambda q: (q, 0, 0, 0))
    spec3 = lambda r, c: pl.BlockSpec((1, r, c), lambda q: (q, 0, 0))
    shapes = [(cw, cw), (cw, ns), (cw, ns), (ns, cw), (ns, cw)]
    return pl.pallas_call(
        functools.partial(_ssm_expand_kernel, chunk=chunk),
        grid=(nq,),
        in_specs=[spec4(a) for a in (kdc, w1r, w1i, w2r, w2i)],
        out_specs=[spec3(r, c) for r, c in shapes],
        out_shape=[jax.ShapeDtypeStruct((nq, r, c), dtype) for r, c in shapes],
        compiler_params=_cparams(("parallel",)),
        name="ssm_expand",
    )(kdc, w1r, w1i, w2r, w2i)


def _s5_seq_kernel(u_ref, w1r_ref, w1i_ref, kt_ref, w2r_ref, w2i_ref, ar_ref, ai_ref, d_ref, s0r_ref, s0i_ref,
                   o_ref, sfr_ref, sfi_ref, ucat, slr, sli, spr, spi, yc, yb, st_r, st_i, *, chunk, n_rows, nb):
    ts = pl.program_id(1)
    ns = st_r.shape[-1]

    @pl.when(ts == 0)
    def _():
        st_r[...] = s0r_ref[...]
        st_i[...] = s0i_ref[...]

    for b in range(nb):
        for tl in range(chunk):
            ucat[:, b, tl * LANES:(tl + 1) * LANES] = u_ref[b, pl.ds(tl, n_rows, stride=chunk), :]
    x = ucat[...].reshape(n_rows * nb, chunk * LANES).astype(BF16)
    slr[...] = jnp.dot(x, w1r_ref[0], preferred_element_type=F32).reshape(n_rows, nb, ns)
    sli[...] = jnp.dot(x, w1i_ref[0], preferred_element_type=F32).reshape(n_rows, nb, ns)
    ar = jnp.broadcast_to(ar_ref[0], (nb, ns))
    ai = jnp.broadcast_to(ai_ref[0], (nb, ns))

    def step(n, carry):
        sr, si = carry
        spr[n] = sr
        spi[n] = si
        return ar * sr - ai * si + slr[n], ar * si + ai * sr + sli[n]

    sr, si = lax.fori_loop(0, n_rows, step, (st_r[...], st_i[...]))
    st_r[...] = sr
    st_i[...] = si
    sfr_ref[...] = sr
    sfi_ref[...] = si
    y = (jnp.dot(x, kt_ref[0], preferred_element_type=F32)
         + jnp.dot(spr[...].reshape(n_rows * nb, ns).astype(BF16), w2r_ref[0], preferred_element_type=F32)
         + jnp.dot(spi[...].reshape(n_rows * nb, ns).astype(BF16), w2i_ref[0], preferred_element_type=F32))
    yc[...] = y.reshape(n_rows, nb, chunk * LANES)
    for b in range(nb):
        for tl in range(chunk):
            yb[b, pl.ds(tl, n_rows, stride=chunk), :] = yc[:, b, tl * LANES:(tl + 1) * LANES]
    o_ref[...] = jax.nn.gelu(yb[...] + d_ref[...] * u_ref[...]).astype(o_ref.dtype)


def s5_sequence(u, s0_re, s0_im, d_skip, mats, chunk, t_seg):
    b, t, n = u.shape
    g, p = s0_re.shape[1:]
    nq = n // LANES
    ns = GROUPS_PER_TILE * p
    cw = chunk * LANES
    n_rows = t_seg // chunk
    wspec = lambda r, c: pl.BlockSpec((1, r, c), lambda q, s: (q, 0, 0))
    state_spec = pl.BlockSpec((b, ns), lambda q, s: (0, q))
    y, sr, si = pl.pallas_call(
        functools.partial(_s5_seq_kernel, chunk=chunk, n_rows=n_rows, nb=b),
        grid=(nq, t // t_seg),
        in_specs=[pl.BlockSpec((b, t_seg, LANES), lambda q, s: (0, s, q)),
                  wspec(cw, ns), wspec(cw, ns), wspec(cw, cw), wspec(ns, cw), wspec(ns, cw),
                  wspec(1, ns), wspec(1, ns), pl.BlockSpec((1, LANES), lambda q, s: (0, q)), state_spec, state_spec],
        out_specs=[pl.BlockSpec((b, t_seg, LANES), lambda q, s: (0, s, q)), state_spec, state_spec],
        out_shape=[jax.ShapeDtypeStruct((b, t, n), BF16), jax.ShapeDtypeStruct((b, g * p), F32),
                   jax.ShapeDtypeStruct((b, g * p), F32)],
        scratch_shapes=[pltpu.VMEM((n_rows, b, cw), F32),
                        pltpu.VMEM((n_rows, b, ns), F32), pltpu.VMEM((n_rows, b, ns), F32),
                        pltpu.VMEM((n_rows, b, ns), F32), pltpu.VMEM((n_rows, b, ns), F32),
                        pltpu.VMEM((n_rows, b, cw), F32), pltpu.VMEM((b, t_seg, LANES), F32),
                        pltpu.VMEM((b, ns), F32), pltpu.VMEM((b, ns), F32)],
        compiler_params=_cparams(("parallel", "arbitrary")),
        name="s5_sequence",
    )(u, mats["w1r"], mats["w1i"], mats["kt"], mats["w2r"], mats["w2i"], mats["ar"], mats["ai"],
      d_skip.reshape(1, n).astype(F32),
      s0_re.astype(F32).reshape(b, g * p), s0_im.astype(F32).reshape(b, g * p))
    return y, sr.reshape(b, g, p), si.reshape(b, g, p)


def _s5_step_kernel(u_ref, w1r_ref, w1i_ref, kt_ref, w2r_ref, w2i_ref, ar_ref, ai_ref, d_ref, s0r_ref, s0i_ref,
                    o_ref, sfr_ref, sfi_ref):
    dot = functools.partial(jnp.dot, preferred_element_type=F32, precision=lax.Precision.HIGHEST)
    x = u_ref[...]
    sr, si = s0r_ref[...], s0i_ref[...]
    ar, ai = ar_ref[0], ai_ref[0]
    sfr_ref[...] = ar * sr - ai * si + dot(x, w1r_ref[0])
    sfi_ref[...] = ar * si + ai * sr + dot(x, w1i_ref[0])
    y = dot(x, kt_ref[0]) + dot(sr, w2r_ref[0]) + dot(si, w2i_ref[0])
    o_ref[...] = jax.nn.gelu(y + d_ref[...] * x).astype(o_ref.dtype)


def s5_step(u, s0_re, s0_im, d_skip, mats):
    b, n = u.shape
    g, p = s0_re.shape[1:]
    nq = n // LANES
    ns = GROUPS_PER_TILE * p
    wspec = lambda r, c: pl.BlockSpec((1, r, c), lambda q: (q, 0, 0))
    state_spec = pl.BlockSpec((b, ns), lambda q: (0, q))
    lane_spec = pl.BlockSpec((b, LANES), lambda q: (0, q))
    y, sr, si = pl.pallas_call(
        _s5_step_kernel,
        grid=(nq,),
        in_specs=[lane_spec, wspec(LANES, ns), wspec(LANES, ns), wspec(LANES, LANES), wspec(ns, LANES),
                  wspec(ns, LANES), wspec(1, ns), wspec(1, ns), pl.BlockSpec((1, LANES), lambda q: (0, q)),
                  state_spec, state_spec],
        out_specs=[lane_spec, state_spec, state_spec],
        out_shape=[jax.ShapeDtypeStruct((b, n), BF16), jax.ShapeDtypeStruct((b, g * p), F32),
                   jax.ShapeDtypeStruct((b, g * p), F32)],
        compiler_params=_cparams(("parallel",)),
        name="s5_step",
    )(u, mats["w1r"], mats["w1i"], mats["kt"], mats["w2r"], mats["w2i"], mats["ar"], mats["ai"],
      d_skip.reshape(1, n).astype(F32), s0_re.astype(F32).reshape(b, g * p), s0_im.astype(F32).reshape(b, g * p))
    return y, sr.reshape(b, g, p), si.reshape(b, g, p)


def _glu_kernel(y_ref, w_ref, o_ref):
    n = o_ref.shape[-1]
    z = jnp.dot(y_ref[...], w_ref[...], preferred_element_type=F32)
    o_ref[...] = (z[:, :n] * jax.nn.sigmoid(z[:, n:])).astype(o_ref.dtype)


def ssm_glu(y, w_glu, tm):
    m, n = y.shape
    return pl.pallas_call(
        _glu_kernel,
        grid=(m // tm,),
        in_specs=[pl.BlockSpec((tm, n), lambda i: (i, 0)), pl.BlockSpec((n, 2 * n), lambda i: (0, 0))],
        out_specs=pl.BlockSpec((tm, n), lambda i: (i, 0)),
        out_shape=jax.ShapeDtypeStruct((m, n), BF16),
        compiler_params=_cparams(("parallel",)),
        name="ssm_glu",
    )(y, w_glu)


def _sortable_key(score):
    bits = pltpu.bitcast(score, jnp.int32)
    return jnp.where(bits < 0, bits ^ jnp.int32(0x7FFFFFFF), bits)


def _kth_largest_key(key, k, axis=-1):
    shape = list(key.shape)
    shape[axis] = 1

    def body(i, res):
        cand = res + jnp.left_shift(jnp.int32(1), 31 - i)
        cnt = jnp.sum(jnp.where(key >= cand, 1.0, 0.0), axis=axis, keepdims=True)
        return jnp.where(cnt >= float(k), cand, res)

    return lax.fori_loop(0, 32, body, jnp.full(tuple(shape), INT_MIN, jnp.int32))


def _exclusive_cumsum_rows(flags_bf16):
    n, cols = flags_bf16.shape
    r = lax.broadcasted_iota(jnp.int32, (LANES, LANES), 0)
    c = lax.broadcasted_iota(jnp.int32, (LANES, LANES), 1)
    tri = jnp.where(c < r, 1.0, 0.0).astype(BF16)
    off = jnp.zeros((1, cols), F32)
    out = []
    for i in range(n // LANES):
        chunk = flags_bf16[i * LANES:(i + 1) * LANES, :]
        out.append(jnp.dot(tri, chunk, preferred_element_type=F32) + off)
        off = off + jnp.sum(chunk.astype(F32), axis=0, keepdims=True)
    return jnp.concatenate(out, axis=0)


def _exclusive_cumsum_lanes(flags_bf16):
    rows, n = flags_bf16.shape
    r = lax.broadcasted_iota(jnp.int32, (LANES, LANES), 0)
    c = lax.broadcasted_iota(jnp.int32, (LANES, LANES), 1)
    tri = jnp.where(r < c, 1.0, 0.0).astype(BF16)
    off = jnp.zeros((rows, 1), F32)
    out = []
    for i in range(n // LANES):
        chunk = flags_bf16[:, i * LANES:(i + 1) * LANES]
        out.append(jnp.dot(chunk, tri, preferred_element_type=F32) + off)
        off = off + jnp.sum(chunk.astype(F32), axis=-1, keepdims=True)
    return jnp.concatenate(out, axis=-1)


def _attend_keys(qi_ref, ki_ref, w_ref, q_ref, k_ref, v_ref, o_ref, score_scr, bias_scr, *, qb, tq, lk, topk):
    n_chunks = lk // MXU_WIDTH
    qi_all = qi_ref[0].reshape(H_IDX * tq, D_IDX)
    w = w_ref[0] * (D_IDX ** -0.5)

    def chunk_body(c, carry):
        k0 = pl.multiple_of(c * MXU_WIDTH, MXU_WIDTH)
        s = lax.dot_general(ki_ref[0, pl.ds(k0, MXU_WIDTH), :], qi_all, NT_DIMS, preferred_element_type=F32)
        acc = jnp.zeros((MXU_WIDTH, tq), F32)
        for h in range(H_IDX):
            acc = acc + jnp.maximum(s[:, h * tq:(h + 1) * tq], 0.0) * w[h:h + 1, :]
        score_scr[c] = acc
        return carry

    lax.fori_loop(0, n_chunks, chunk_body, 0)
    score = jnp.concatenate([score_scr[c] for c in range(n_chunks)], axis=0)
    kpos = lax.broadcasted_iota(jnp.int32, (lk, tq), 0)
    qpos = qb * tq + lax.broadcasted_iota(jnp.int32, (lk, tq), 1)
    causal = kpos <= qpos
    key = _sortable_key(jnp.where(causal, score, -jnp.inf))
    thr = _kth_largest_key(key, topk, axis=0)
    gt = key > thr
    eq = jnp.logical_and(key == thr, causal)
    need = float(topk) - jnp.sum(jnp.where(gt, 1.0, 0.0), axis=0, keepdims=True)
    n_eq = jnp.sum(jnp.where(eq, 1.0, 0.0), axis=0, keepdims=True)
    bias_scr[:lk, :] = jnp.where(jnp.logical_and(causal, key >= thr), 0.0, NEG_INF)

    @pl.when(jnp.max(n_eq - need) > 0.0)
    def _():
        rank = _exclusive_cumsum_rows(jnp.where(eq, 1.0, 0.0).astype(BF16))
        keep = jnp.logical_or(gt, jnp.logical_and(eq, rank < need))
        bias_scr[:lk, :] = jnp.where(jnp.logical_and(causal, keep), 0.0, NEG_INF)

    bias = bias_scr[:lk, :]
    scale = HEAD_DIM ** -0.5
    for h in range(q_ref.shape[-1] // HEAD_DIM):
        sl = slice(h * HEAD_DIM, (h + 1) * HEAD_DIM)
        logits = lax.dot_general(k_ref[0, :lk, sl], q_ref[0, :, sl], NT_DIMS, preferred_element_type=F32) * scale + bias
        m = jnp.max(logits, axis=0, keepdims=True)
        p = jnp.exp(logits - m)
        den = jnp.sum(p, axis=0, keepdims=True)
        den_col = jnp.transpose(jnp.broadcast_to(den, (SUBLANES, tq)))[:, 0:1]
        o = lax.dot_general(p.astype(BF16), v_ref[0, :lk, sl], TN_DIMS, preferred_element_type=F32)
        o_ref[0, :, sl] = (o / den_col).astype(o_ref.dtype)


def _prompt_attn_kernel(qi_ref, ki_ref, w_ref, q_ref, k_ref, v_ref, o_ref, score_scr, bias_scr,
                        *, tq, seq, topk, n_buckets):
    qb = pl.program_id(1)
    per = (seq // tq) // n_buckets
    for bkt in range(n_buckets):
        body = functools.partial(_attend_keys, qi_ref, ki_ref, w_ref, q_ref, k_ref, v_ref, o_ref, score_scr, bias_scr,
                                 qb=qb, tq=tq, lk=(bkt + 1) * per * tq, topk=topk)
        pl.when(qb // per == bkt)(body)


def prompt_attention(qi, ki, wi, q, k, v, tq, n_buckets):
    b, t, n = q.shape
    topk = min(TOPK_MAX, t // 4)
    return pl.pallas_call(
        functools.partial(_prompt_attn_kernel, tq=tq, seq=t, topk=topk, n_buckets=n_buckets),
        grid=(b, t // tq),
        in_specs=[pl.BlockSpec((1, H_IDX, tq, D_IDX), lambda i, j: (i, 0, j, 0)),
                  pl.BlockSpec((1, t, D_IDX), lambda i, j: (i, 0, 0)),
                  pl.BlockSpec((1, H_IDX, tq), lambda i, j: (i, 0, j)),
                  pl.BlockSpec((1, tq, n), lambda i, j: (i, j, 0)),
                  pl.BlockSpec((1, t, n), lambda i, j: (i, 0, 0)),
                  pl.BlockSpec((1, t, n), lambda i, j: (i, 0, 0))],
        out_specs=pl.BlockSpec((1, tq, n), lambda i, j: (i, j, 0)),
        out_shape=jax.ShapeDtypeStruct((b, t, n), BF16),
        scratch_shapes=[pltpu.VMEM((t // MXU_WIDTH, MXU_WIDTH, tq), F32), pltpu.VMEM((t, tq), F32)],
        compiler_params=_cparams(("parallel", "parallel")),
        name="prompt_attn",
    )(qi, ki, wi, q, k, v)


def _sample_score_kernel(pt_ref, qi_ref, w_ref, kin_ref, cache_ref, o_ref, kbuf, sem, *, n_pages, n_batch):
    b = pl.program_id(0)
    past = n_pages * PAGE_SIZE

    def page_copy(bb, slot, pg):
        return pltpu.make_async_copy(cache_ref.at[pt_ref[bb, pg]],
                                     kbuf.at[slot, :, pl.ds(pl.multiple_of(pg * PAGE_SIZE, PAGE_SIZE), PAGE_SIZE)],
                                     sem.at[slot])

    def start_all(bb, slot):
        lax.fori_loop(0, n_pages, lambda pg, c: (page_copy(bb, slot, pg).start(), c)[1], 0)

    @pl.when(b == 0)
    def _():
        start_all(0, 0)

    @pl.when(b + 1 < n_batch)
    def _():
        start_all(b + 1, (b + 1) % 2)

    slot = b % 2
    lax.fori_loop(0, n_pages, lambda pg, c: (page_copy(b, slot, pg).wait(), c)[1], 0)

    qi = qi_ref[0]
    w = w_ref[0] * (D_IDX ** -0.5)
    step = 2048
    for c in range(past // step):
        kc = kbuf[slot, :, c * step:(c + 1) * step].astype(BF16)
        s = jnp.dot(qi, kc, preferred_element_type=F32)
        o_ref[0, :, c * step:(c + 1) * step] = jnp.sum(jnp.maximum(s, 0.0) * w, axis=0, keepdims=True)
    s_new = jnp.sum(qi.astype(F32) * kin_ref[0].astype(BF16).astype(F32), axis=-1, keepdims=True)
    s_new = jnp.sum(jnp.maximum(s_new, 0.0) * w, axis=0, keepdims=True)
    lane = lax.broadcasted_iota(jnp.int32, (1, LANES), 1)
    o_ref[0, :, past:] = jnp.where(lane == 0, s_new, -jnp.inf)


def sample_scores(page_table, qi, wi, ki_new, cache_idx_k):
    bsz, n_pages = page_table.shape
    past = n_pages * PAGE_SIZE
    grid_spec = pltpu.PrefetchScalarGridSpec(
        num_scalar_prefetch=1,
        grid=(bsz,),
        in_specs=[pl.BlockSpec((1, H_IDX, D_IDX), lambda i, pt: (i, 0, 0)),
                  pl.BlockSpec((1, H_IDX, 1), lambda i, pt: (i, 0, 0)),
                  pl.BlockSpec((1, 1, D_IDX), lambda i, pt: (i, 0, 0)),
                  pl.BlockSpec(memory_space=pl.ANY)],
        out_specs=pl.BlockSpec((1, 1, past + LANES), lambda i, pt: (i, 0, 0)),
        scratch_shapes=[pltpu.VMEM((2, D_IDX, past), F32), pltpu.SemaphoreType.DMA((2,))],
    )
    return pl.pallas_call(
        functools.partial(_sample_score_kernel, n_pages=n_pages, n_batch=bsz),
        grid_spec=grid_spec,
        out_shape=jax.ShapeDtypeStruct((bsz, 1, past + LANES), F32),
        compiler_params=_cparams(("arbitrary",)),
        name="sample_scores",
    )(page_table, qi, wi, ki_new, jnp.swapaxes(cache_idx_k, 1, 2))


def _sample_select_kernel(score_ref, pos_ref, o_ref, rank_scr, *, topk):
    score = score_ref[...]
    bsz, n = score.shape
    key = _sortable_key(score)
    thr = _kth_largest_key(key, topk)
    gt = key > thr
    eq = key == thr
    need = topk - jnp.sum(gt.astype(jnp.int32), axis=-1, keepdims=True)
    rank_eq = _exclusive_cumsum_lanes(jnp.where(eq, 1.0, 0.0).astype(BF16))
    sel = jnp.logical_or(gt, jnp.logical_and(eq, rank_eq < need.astype(F32)))
    self32 = jnp.where(sel, 1.0, 0.0)
    rank = _exclusive_cumsum_lanes(self32.astype(BF16))
    rank_scr[...] = jnp.where(sel, rank, -1.0)
    slot_id = lax.broadcasted_iota(jnp.int32, (topk, 1), 0).astype(F32)
    n_split = 3
    step = n // n_split

    def body(bb, c):
        acc = jnp.zeros((topk, LANES), F32)
        for s in range(n_split):
            onehot = jnp.where(rank_scr[pl.ds(bb, 1), s * step:(s + 1) * step] == slot_id, 1.0, 0.0).astype(BF16)
            acc = acc + jnp.dot(onehot, pos_ref[s * step:(s + 1) * step, :], preferred_element_type=F32)
        o_ref[bb] = acc
        return c

    lax.fori_loop(0, bsz, body, 0)


def sample_select(scores, topk):
    bsz, n = scores.shape
    s = jnp.arange(n, dtype=jnp.int32)
    lane = jnp.arange(LANES, dtype=jnp.int32)[None, :]
    pos = jnp.where(lane == 0, (s // PAGE_SIZE)[:, None], jnp.where(lane == 1, (s % PAGE_SIZE)[:, None], 0)).astype(BF16)
    return pl.pallas_call(
        functools.partial(_sample_select_kernel, topk=topk),
        out_shape=jax.ShapeDtypeStruct((bsz, topk, LANES), F32),
        scratch_shapes=[pltpu.VMEM((bsz, n), F32)],
        compiler_params=pltpu.CompilerParams(vmem_limit_bytes=VMEM_LIMIT),
        name="sample_select",
    )(scores, pos)


def _sample_attend_kernel(idx_ref, pt_ref, q_ref, knew_ref, vnew_ref, ck_ref, cv_ref, o_ref, kbuf, vbuf, ksem, vsem,
                          *, topk, past, n_batch):
    b = pl.program_id(0)

    def row_copies(bb, slot, r):
        i = jnp.minimum(idx_ref[bb, r], past - 1)
        phys = pt_ref[bb, i // PAGE_SIZE]
        off = i % PAGE_SIZE
        return (pltpu.make_async_copy(ck_ref.at[phys, off], kbuf.at[slot, r], ksem.at[slot]),
                pltpu.make_async_copy(cv_ref.at[phys, off], vbuf.at[slot, r], vsem.at[slot]))

    def start_all(bb, slot):
        def body(r, c):
            for cp in row_copies(bb, slot, r):
                cp.start()
            return c

        lax.fori_loop(0, topk, body, 0, unroll=8)

    @pl.when(b == 0)
    def _():
        start_all(0, 0)

    @pl.when(b + 1 < n_batch)
    def _():
        start_all(b + 1, (b + 1) % 2)

    slot = b % 2

    def wait_body(r, c):
        for cp in row_copies(b, slot, r):
            cp.wait()
        return c

    lax.fori_loop(0, topk, wait_body, 0, unroll=8)

    @pl.when(idx_ref[b, topk - 1] == past)
    def _():
        kbuf[slot, topk - 1] = knew_ref[0]
        vbuf[slot, topk - 1] = vnew_ref[0]

    q = q_ref[0]
    k = kbuf[slot]
    logits = jnp.sum(k * q[None], axis=-1, keepdims=True) * (HEAD_DIM ** -0.5)
    m = jnp.max(logits, axis=0, keepdims=True)
    p = jnp.exp(logits - m)
    den = jnp.sum(p, axis=0)
    o_ref[0] = jnp.sum(p * vbuf[slot], axis=0) / den


def sample_attend(idx, page_table, q, k_new, v_new, cache_k, cache_v):
    bsz, h, dh = q.shape
    topk = idx.shape[1]
    past = page_table.shape[1] * PAGE_SIZE
    grid_spec = pltpu.PrefetchScalarGridSpec(
        num_scalar_prefetch=2,
        grid=(bsz,),
        in_specs=[pl.BlockSpec((1, h, dh), lambda i, a, c: (i, 0, 0)), pl.BlockSpec((1, h, dh), lambda i, a, c: (i, 0, 0)),
                  pl.BlockSpec((1, h, dh), lambda i, a, c: (i, 0, 0)),
                  pl.BlockSpec(memory_space=pl.ANY), pl.BlockSpec(memory_space=pl.ANY)],
        out_specs=pl.BlockSpec((1, h, dh), lambda i, a, c: (i, 0, 0)),
        scratch_shapes=[pltpu.VMEM((2, topk, h, dh), F32), pltpu.VMEM((2, topk, h, dh), F32),
                        pltpu.SemaphoreType.DMA((2,)), pltpu.SemaphoreType.DMA((2,))],
    )
    return pl.pallas_call(
        functools.partial(_sample_attend_kernel, topk=topk, past=past, n_batch=bsz),
        grid_spec=grid_spec,
        out_shape=jax.ShapeDtypeStruct((bsz, h, dh), F32),
        compiler_params=_cparams(("arbitrary",)),
        name="sample_attend",
    )(idx, page_table, q, k_new, v_new, cache_k, cache_v)


def _merge_kernel(h_ref, s_ref, a_ref, x_ref, wgs_ref, wga_ref, wbs_ref, wba_ref, wo_ref, gn_ref, o_ref, hn_ref,
                  *, n_tiles):
    j = pl.program_id(1)
    dot = functools.partial(jnp.dot, preferred_element_type=F32)
    tm = h_ref.shape[0]
    sub = min(tm, MERGE_SUB_ROWS)

    @pl.when(j == 0)
    def _():
        o_ref[...] = x_ref[...]

    for r0 in range(0, tm, sub):
        rows = slice(r0, r0 + sub)
        h = h_ref[rows, :]
        merged = (jax.nn.sigmoid(dot(h, wgs_ref[0])) * dot(s_ref[rows, :], wbs_ref[0])
                  + jax.nn.sigmoid(dot(h, wga_ref[0])) * dot(a_ref[rows, :], wba_ref[0]))
        o_ref[rows, :] += dot(merged.astype(BF16), wo_ref[...])

    @pl.when(j == n_tiles - 1)
    def _():
        for r0 in range(0, tm, sub):
            rows = slice(r0, r0 + sub)
            x1 = o_ref[rows, :]
            ms = jnp.mean(x1 * x1, axis=-1, keepdims=True)
            hn_ref[rows, :] = (x1 * lax.rsqrt(ms + RMS_EPS) * gn_ref[...]).astype(hn_ref.dtype)


def merge_branches(h, ssm_out, att, x2d, w_gates, w_bs, w_ba, w_out, g_next, tm, tn):
    m, d = x2d.shape
    kdim = ssm_out.shape[1]
    nj = d // tn
    row = lambda c: pl.BlockSpec((tm, c), lambda i, j: (i, 0))
    return pl.pallas_call(
        functools.partial(_merge_kernel, n_tiles=nj),
        grid=(m // tm, nj),
        in_specs=[row(d), row(kdim), row(kdim), row(d),
                  pl.BlockSpec((1, d, tn), lambda i, j: (j, 0, 0)), pl.BlockSpec((1, d, tn), lambda i, j: (j + nj, 0, 0)),
                  pl.BlockSpec((1, kdim, tn), lambda i, j: (j, 0, 0)), pl.BlockSpec((1, kdim, tn), lambda i, j: (j, 0, 0)),
                  pl.BlockSpec((tn, d), lambda i, j: (j, 0)), pl.BlockSpec((1, d), lambda i, j: (0, 0))],
        out_specs=[row(d), row(d)],
        out_shape=[jax.ShapeDtypeStruct((m, d), F32), jax.ShapeDtypeStruct((m, d), BF16)],
        compiler_params=_cparams(("parallel", "arbitrary")),
        name="merge_branches",
    )(h, ssm_out, att, x2d, w_gates, w_gates, w_bs, w_ba, w_out, g_next.reshape(1, d).astype(F32))


def _ffn_step_kernel(x_ref, gf_ref, gfin_ref, wa_ref, wb_ref, cwa_ref, cwb_ref, cba_ref, cbb_ref, wd_ref,
                     p2a_ref, p1a_ref, p2b_ref, p1b_ref, y_ref, ca_ref, cb_ref, h_scr, acc_scr, *, n_ff_tiles):
    j = pl.program_id(1)

    @pl.when(j == 0)
    def _():
        x = x_ref[0]
        ms = jnp.mean(x * x, axis=-1, keepdims=True)
        h_scr[...] = (x * lax.rsqrt(ms + RMS_EPS) * gf_ref[...]).astype(BF16)
        acc_scr[...] = jnp.zeros_like(acc_scr)

    h = h_scr[...]

    def conv_half(w_ref, cw_ref, cbias_ref, out_ref, p2_ref, p1_ref):
        up = jnp.dot(h, w_ref[0], preferred_element_type=F32)
        cw = cw_ref[...]
        out_ref[...] = up
        return cbias_ref[...] + cw[0:1, :] * p2_ref[...] + cw[1:2, :] * p1_ref[...] + cw[2:3, :] * up

    a = conv_half(wa_ref, cwa_ref, cba_ref, ca_ref, p2a_ref, p1a_ref)
    g = conv_half(wb_ref, cwb_ref, cbb_ref, cb_ref, p2b_ref, p1b_ref)
    act = (jax.nn.silu(a) * g).astype(BF16)
    acc_scr[...] += jnp.dot(act, wd_ref[...], preferred_element_type=F32)

    @pl.when(j == n_ff_tiles - 1)
    def _():
        x2 = x_ref[0] + acc_scr[...]
        ms = jnp.mean(x2 * x2, axis=-1, keepdims=True)
        y_ref[0] = x2 * lax.rsqrt(ms + RMS_EPS) * gfin_ref[...]


def conv_ffn_step(x, conv_prev, g_ffn, g_final, w_up, conv_w, conv_b, w_down, tm):
    b, t, d = x.shape
    tf = w_up.shape[2]
    f = w_down.shape[0]
    nf = f // tf
    gf = g_ffn.reshape(1, d).astype(F32)
    gfin = g_final.reshape(1, d).astype(F32)
    cb2 = conv_b.reshape(1, 2 * f).astype(F32)
    conv_w = conv_w.astype(F32)
    kern = functools.partial(_ffn_step_kernel, n_ff_tiles=nf)
    p2, p1 = conv_prev[:, 0, :], conv_prev[:, 1, :]
    wspec = lambda blk, off: pl.BlockSpec(blk, lambda i, j: (0, j + off))
    upspec = lambda off: pl.BlockSpec((1, d, tf), lambda i, j: (j + off, 0, 0))
    in_specs = [pl.BlockSpec((1, tm, d), lambda i, j: (0, i, 0)),
                pl.BlockSpec((1, d), lambda i, j: (0, 0)), pl.BlockSpec((1, d), lambda i, j: (0, 0)),
                upspec(0), upspec(nf), wspec((CONV_W, tf), 0), wspec((CONV_W, tf), nf),
                wspec((1, tf), 0), wspec((1, tf), nf),
                pl.BlockSpec((tf, d), lambda i, j: (j, 0)),
                pl.BlockSpec((tm, tf), lambda i, j: (i, j)), pl.BlockSpec((tm, tf), lambda i, j: (i, j)),
                pl.BlockSpec((tm, tf), lambda i, j: (i, j + nf)), pl.BlockSpec((tm, tf), lambda i, j: (i, j + nf))]
    out_specs = [pl.BlockSpec((1, tm, d), lambda i, j: (0, i, 0)),
                 pl.BlockSpec((tm, tf), lambda i, j: (i, j)), pl.BlockSpec((tm, tf), lambda i, j: (i, j))]
    out_shape = [jax.ShapeDtypeStruct((1, t, d), F32), jax.ShapeDtypeStruct((t, f), F32),
                 jax.ShapeDtypeStruct((t, f), F32)]
    scratch = [pltpu.VMEM((tm, d), BF16), pltpu.VMEM((tm, d), F32)]
    y, ua, ub = pl.pallas_call(
        kern, grid=(t // tm, nf), in_specs=in_specs, out_specs=out_specs, out_shape=out_shape,
        scratch_shapes=scratch, compiler_params=_cparams(("arbitrary", "arbitrary")),
        name="conv_ffn_step",
    )(x, gf, gfin, w_up, w_up, conv_w, conv_w, cb2, cb2, w_down, p2, p1, p2, p1)
    return y, jnp.stack([p1, jnp.concatenate([ua, ub], axis=-1)], axis=1)


def _ffn_seq_kernel(h_ref, xt_ref, gfin_ref, wa_ref, wb_ref, cwa_ref, cwb_ref, cba_ref, cbb_ref, wd_ref, pa_ref, pb_ref,
                    y_ref, ca_ref, cb_ref, act_scr, car_a, car_b, *, tm, tf, nf, tn, nd):
    i = pl.program_id(1)
    j = pl.program_id(2)

    @pl.when(j < nf)
    def _():
        sub = min(tm, FFN_SUB_ROWS)
        row = lax.broadcasted_iota(jnp.int32, (sub, tf), 0)

        def conv_half(w_ref, cw_ref, cbias_ref, out_ref, prev_ref, car):
            cw = cw_ref[...]
            cbias = cbias_ref[...]

            @pl.when(i == 0)
            def _():
                car[j] = prev_ref[0]

            c2 = car[j]
            rm2, rm1 = c2[0:1, :], c2[1:2, :]
            outs = []
            for r0 in range(0, tm, sub):
                up = jnp.dot(h_ref[0, r0:r0 + sub, :], w_ref[0], preferred_element_type=F32)
                p1 = jnp.where(row == 0, rm1, pltpu.roll(up, 1, 0))
                p2 = jnp.where(row == 0, rm2, jnp.where(row == 1, rm1, pltpu.roll(up, 2, 0)))
                outs.append(cbias + cw[0:1, :] * p2 + cw[1:2, :] * p1 + cw[2:3, :] * up)
                rm2, rm1 = up[sub - 2:sub - 1, :], up[sub - 1:sub, :]
            last2 = jnp.concatenate([rm2, rm1], axis=0)
            car[j] = last2
            out_ref[0, 0] = last2
            return outs

        a = conv_half(wa_ref, cwa_ref, cba_ref, ca_ref, pa_ref, car_a)
        g = conv_half(wb_ref, cwb_ref, cbb_ref, cb_ref, pb_ref, car_b)
        act = [(jax.nn.silu(ar) * gr).astype(BF16) for ar, gr in zip(a, g)]
        for k in range(nf):

            @pl.when(j == k)
            def _(k=k):
                for r, act_r in enumerate(act):
                    act_scr[r * sub:(r + 1) * sub, k * tf:(k + 1) * tf] = act_r

    @pl.when(j >= nf)
    def _():
        sub = min(tm, 2 * FFN_SUB_ROWS)
        for r0 in range(0, tm, sub):
            part = xt_ref[0, r0:r0 + sub, :] + jnp.dot(act_scr[r0:r0 + sub, :], wd_ref[0], preferred_element_type=F32)
            for n in range(nd):

                @pl.when(j == nf + n)
                def _(n=n, part=part, r0=r0):
                    y_ref[0, r0:r0 + sub, n * tn:(n + 1) * tn] = part

    @pl.when(j == nf + nd - 1)
    def _():
        sub = min(tm, FFN_SUB_ROWS)
        for r0 in range(0, tm, sub):
            x2 = y_ref[0, r0:r0 + sub, :]
            ms = jnp.mean(x2 * x2, axis=-1, keepdims=True)
            y_ref[0, r0:r0 + sub, :] = x2 * lax.rsqrt(ms + RMS_EPS) * gfin_ref[...]


def conv_ffn_sequence(h, x, conv_prev, g_final, w_up, conv_w, conv_b, w_down, tm):
    b, t, d = x.shape
    tf = w_up.shape[2]
    nd, f, tn = w_down.shape
    nf, nt = f // tf, t // tm
    gfin = g_final.reshape(1, d).astype(F32)
    cb2 = conv_b.reshape(1, 2 * f).astype(F32)
    conv_w = conv_w.astype(F32)
    up_tile = lambda j: jnp.minimum(j, nf - 1)
    down_tile = lambda j: jnp.clip(j - nf, 0, nd - 1)
    wspec = lambda blk, off: pl.BlockSpec(blk, lambda bi, i, j: (0, up_tile(j) + off))
    upspec = lambda off: pl.BlockSpec((1, d, tf), lambda bi, i, j: (up_tile(j) + off, 0, 0))
    prev_spec = lambda off: pl.BlockSpec((1, CONV_W - 1, tf), lambda bi, i, j: (bi, 0, up_tile(j) + off))
    state_spec = pl.BlockSpec((1, 1, CONV_W - 1, tf), lambda bi, i, j: (bi, i, 0, up_tile(j)))
    y, ca, cb = pl.pallas_call(
        functools.partial(_ffn_seq_kernel, tm=tm, tf=tf, nf=nf, tn=tn, nd=nd),
        grid=(b, nt, nf + nd),
        in_specs=[pl.BlockSpec((1, tm, d), lambda bi, i, j: (bi, i, 0), pipeline_mode=pl.Buffered(1)),
                  pl.BlockSpec((1, tm, tn), lambda bi, i, j: (bi, i, down_tile(j))),
                  pl.BlockSpec((1, d), lambda bi, i, j: (0, 0)),
                  upspec(0), upspec(nf), wspec((CONV_W, tf), 0), wspec((CONV_W, tf), nf),
                  wspec((1, tf), 0), wspec((1, tf), nf),
                  pl.BlockSpec((1, f, tn), lambda bi, i, j: (down_tile(j), 0, 0)),
                  prev_spec(0), prev_spec(nf)],
        out_specs=[pl.BlockSpec((1, tm, d), lambda bi, i, j: (bi, i, 0), pipeline_mode=pl.Buffered(1)),
                   state_spec, state_spec],
        out_shape=[jax.ShapeDtypeStruct((b, t, d), F32), jax.ShapeDtypeStruct((b, nt, CONV_W - 1, f), F32),
                   jax.ShapeDtypeStruct((b, nt, CONV_W - 1, f), F32)],
        scratch_shapes=[pltpu.VMEM((tm, f), BF16),
                        pltpu.VMEM((nf, CONV_W - 1, tf), F32), pltpu.VMEM((nf, CONV_W - 1, tf), F32)],
        compiler_params=_cparams(("arbitrary", "arbitrary", "arbitrary")),
        name="conv_ffn_seq",
    )(h, x, gfin, w_up, w_up, conv_w, conv_w, cb2, cb2, w_down, conv_prev, conv_prev)
    return y, jnp.concatenate([ca[:, -1], cb[:, -1]], axis=-1)


def column_tiles(w, tn):
    k, n = w.shape
    return w.astype(BF16).reshape(k, n // tn, tn).transpose(1, 0, 2)


def _split_w_in(w_in, d_model, n_ssm, n_att):
    sizes = [n_ssm, n_att, n_att, n_att, H_IDX * D_IDX, D_IDX, H_IDX, d_model, d_model]
    offs = np.concatenate([[0], np.cumsum(sizes)]).tolist()
    col = lambda i: w_in[:, offs[i]:offs[i + 1]]
    w_uv = column_tiles(jnp.concatenate([col(0), col(3)], axis=1), COL_TILE)
    w_qk = column_tiles(jnp.concatenate([col(1), col(2)], axis=1), COL_TILE)
    w_qi = column_tiles(col(4), COL_TILE)
    pad = jnp.zeros((w_in.shape[0], LANES - D_IDX - H_IDX), w_in.dtype)
    w_kw = column_tiles(jnp.concatenate([col(5), col(6), pad], axis=1), LANES)
    w_g = column_tiles(jnp.concatenate([col(7), col(8)], axis=1), COL_TILE)
    return w_uv, w_qk, w_qi, w_kw, w_g


def _mix_inputs(x2d, pos, n_pos_tiles, tm, norm_mix, w_groups, n_ssm, n_att, sequence):
    w_uv, w_qk, w_qi, w_kw, w_g = w_groups
    h = rmsnorm_bf16(x2d, norm_mix, tm)
    tn = COL_TILE
    nu, na = n_ssm // tn, n_att // tn
    half_qk = HEAD_DIM // ROPE_FRACTION // 2
    half_idx = D_IDX // ROPE_FRACTION // 2
    out = {}
    if sequence:
        out["u"], out["v"], out["v16"] = project(
            h, w_uv, tm, tn, [(0, nu, F32, "flat"), (nu, na, F32, "flat"), (nu, na, BF16, "flat")])
        out["q16"], out["k"], out["k16"] = project(
            h, w_qk, tm, tn, [(0, na, BF16, "flat"), (na, na, F32, "flat"), (na, na, BF16, "flat")],
            "rope", half_qk, rope_tables(pos, HEAD_DIM, tn), n_pos_tiles)
        out["qi16"], = project(h, w_qi, tm, tn, [(0, H_IDX * D_IDX // tn, BF16, "heads")],
                               "rope", half_idx, rope_tables(pos, D_IDX, tn), n_pos_tiles)
    else:
        out["u"], out["v"] = project(h, w_uv, tm, tn, [(0, nu, F32, "flat"), (nu, na, F32, "flat")])
        out["q"], out["k"] = project(h, w_qk, tm, tn, [(0, na, F32, "flat"), (na, na, F32, "flat")],
                                     "rope", half_qk, rope_tables(pos, HEAD_DIM, tn), n_pos_tiles)
        out["qi"], = project(h, w_qi, tm, tn, [(0, H_IDX * D_IDX // tn, F32, "flat")],
                             "rope", half_idx, rope_tables(pos, D_IDX, tn), n_pos_tiles)
    extra = jnp.concatenate([jnp.full((H_IDX,), H_IDX ** -0.5, F32), jnp.zeros((LANES - D_IDX - H_IDX,), F32)])
    kw, = project(h, w_kw, tm, LANES, [(0, 1, F32, "flat")], "rope", half_idx,
                  rope_tables(pos, D_IDX, LANES, extra), n_pos_tiles)
    out["ki"], out["wi"] = kw[:, :D_IDX], kw[:, D_IDX:D_IDX + H_IDX]
    out["h"] = h
    return out


def kernel(x_prompt, x_sample, cache_k, cache_v, cache_idx_k, page_table, state_ssm_re, state_ssm_im,
           state_ffn_conv, norm_mix, w_in, ssm_A_re, ssm_A_im, ssm_log_dt, ssm_B_re, ssm_B_im, ssm_C_re,
           ssm_C_im, ssm_D, w_glu, w_branch_ssm, w_branch_att, w_out, norm_ffn, w_up, ffn_conv_w,
           ffn_conv_b, w_down, norm_final):
    bp, t, d = x_prompt.shape
    bs, ts, _ = x_sample.shape
    n_ssm = ssm_D.shape[0]
    n_att = w_branch_att.shape[0]
    g = n_ssm // SSM_GROUP
    h_att = n_att // HEAD_DIM
    past = page_table.shape[1] * PAGE_SIZE
    assert ts == 1, "sample group is a single decode step"
    assert bp == SUBLANES, "the S5 sequence kernel keeps one batch row per sublane"

    w_groups = _split_w_in(w_in, d, n_ssm, n_att)
    w_glu_b, w_out_b, w_down_b = (w.astype(BF16) for w in (w_glu, w_out, w_down))
    w_bs, w_ba = column_tiles(w_branch_ssm, COL_TILE), column_tiles(w_branch_att, COL_TILE)
    w_up_t, w_down_t = column_tiles(w_up, COL_TILE), column_tiles(w_down, COL_TILE)
    ssm_params = (ssm_A_re, ssm_A_im, ssm_log_dt, ssm_B_re, ssm_B_im, ssm_C_re, ssm_C_im)

    mp = bp * t
    tm = 1024
    xp2 = x_prompt.reshape(mp, d)
    pos_p = jnp.arange(t, dtype=jnp.int32)
    pr = _mix_inputs(xp2, pos_p, t // tm, tm, norm_mix, w_groups, n_ssm, n_att, True)
    zeros_s = jnp.zeros((bp, g, SSM_STATE), F32)
    y_act, s_re_p, s_im_p = s5_sequence(pr["u"].reshape(bp, t, n_ssm), zeros_s, zeros_s, ssm_D,
                                        ssm_tile_matrices(*ssm_params, SSM_CHUNK, BF16), SSM_CHUNK, 1024)
    ssm_out = ssm_glu(y_act.reshape(mp, n_ssm), w_glu_b, 1024)
    wi_t = pr["wi"].reshape(bp, t, H_IDX).transpose(0, 2, 1)
    att = prompt_attention(pr["qi16"], pr["ki"].astype(BF16).reshape(bp, t, D_IDX), wi_t,
                           pr["q16"].reshape(bp, t, n_att), pr["k16"].reshape(bp, t, n_att),
                           pr["v16"].reshape(bp, t, n_att), 256, 4)
    x1, h_ffn = merge_branches(pr["h"], ssm_out, att.reshape(mp, n_att), xp2, w_groups[4], w_bs, w_ba, w_out_b,
                               norm_ffn, 512, COL_TILE)
    zeros_c = jnp.zeros((bp, CONV_W - 1, w_up.shape[1]), F32)
    y_prompt, conv_p = conv_ffn_sequence(h_ffn.reshape(bp, t, d), x1.reshape(bp, t, d), zeros_c, norm_final, w_up_t,
                                         ffn_conv_w, ffn_conv_b, w_down_t, 1024)
    k_p = pr["k"].reshape(bp, t, h_att, HEAD_DIM)
    v_p = pr["v"].reshape(bp, t, h_att, HEAD_DIM)
    ki_p = pr["ki"].reshape(bp, t, D_IDX)

    xs2 = x_sample.reshape(bs, d)
    pos_s = jnp.full((bs,), past, jnp.int32)
    sm = _mix_inputs(xs2, pos_s, 1, bs, norm_mix, w_groups, n_ssm, n_att, False)
    y_act, s_re_s, s_im_s = s5_step(sm["u"], state_ssm_re, state_ssm_im, ssm_D, ssm_tile_matrices(*ssm_params, 1, F32))
    ssm_out = ssm_glu(y_act, w_glu_b, bs)
    scores = sample_scores(page_table, sm["qi"].astype(BF16).reshape(bs, H_IDX, D_IDX), sm["wi"].reshape(bs, H_IDX, 1),
                           sm["ki"].reshape(bs, 1, D_IDX), cache_idx_k)
    topk = min(TOPK_MAX, (past + 1) // 4)
    hl = sample_select(scores.reshape(bs, past + LANES), topk)
    idx = (hl[:, :, 0] * PAGE_SIZE + hl[:, :, 1]).astype(jnp.int32)
    att = sample_attend(idx, page_table, sm["q"].reshape(bs, h_att, HEAD_DIM), sm["k"].reshape(bs, h_att, HEAD_DIM),
                        sm["v"].reshape(bs, h_att, HEAD_DIM), cache_k, cache_v)
    x1, _ = merge_branches(sm["h"], ssm_out, att.reshape(bs, n_att).astype(BF16), xs2, w_groups[4], w_bs, w_ba, w_out_b,
                           norm_ffn, bs, 512)
    y_s, conv_s = conv_ffn_step(x1.reshape(1, bs, d), state_ffn_conv, norm_ffn, norm_final, w_up_t, ffn_conv_w,
                                ffn_conv_b, w_down_b, bs)
    y_sample = y_s.reshape(bs, 1, d)
    k_s = sm["k"].reshape(bs, 1, h_att, HEAD_DIM)
    v_s = sm["v"].reshape(bs, 1, h_att, HEAD_DIM)
    ki_s = sm["ki"].reshape(bs, 1, D_IDX)

    return (y_prompt, y_sample, k_p, v_p, ki_p, k_s, v_s, ki_s, s_re_p, s_im_p, s_re_s, s_im_s, conv_p, conv_s)
```
